```python
import math
import jax, jax.numpy as jnp
from jax import lax
import numpy as np

D_MODEL = 1024
BATCH = 32
SEQ = 2048
DEPTH = 2
DEC_BATCH = 1
DEC_SEQ = 16384
PAST_LEN = 128

F32 = jnp.float32
EPS = 1e-6
HG_HEADS = D_MODEL // 128
HG_KDIM = 64
HG_VDIM = 64
HG_WIDTH = HG_HEADS * HG_VDIM
HG_CHUNK = 32
ML_HEADS = 4
ML_HDIM = D_MODEL // 8
ML_WIDTH = ML_HEADS * ML_HDIM
ML_CHUNK = 128
HY_WIDTH = D_MODEL // 2
HY_BANDS = 16
HY_EMB = 1 + 2 * HY_BANDS
HY_FILTER_HIDDEN = 64
HY_SHORT_DECAY_PCT = 0.3
HY_LONG_DECAY_PCT = 1.5
HY_DECAY_TARGET = 1e-2
N_BRANCHES = 3
N_IN = 3 * HG_HEADS * HG_KDIM + 2 * HG_WIDTH + 4 * ML_WIDTH + 4 * ML_HEADS + 3 * HY_WIDTH + N_BRANCHES * D_MODEL
N_GROUPS = 4
EXPERTS_PER_GROUP = 8
N_EXPERTS = N_GROUPS * EXPERTS_PER_GROUP
TOP_K = 2
EXPERT_HIDDEN = D_MODEL // 2
MOE_BLOCK = 128

kernel_name = 'hgrn2_mlstm_hyena_hmoe_encoder'


def rmsnorm(x, g):
    xf = x.astype(F32)
    y = xf * lax.rsqrt(jnp.mean(xf * xf, axis=-1, keepdims=True) + EPS)
    return (y * g.astype(F32)).astype(x.dtype)


def head_rmsnorm(h, g):
    y = h * lax.rsqrt(jnp.mean(h * h, axis=-1, keepdims=True) + EPS)
    y = y * g.astype(F32).reshape(h.shape[-2], h.shape[-1])
    return y.reshape(*h.shape[:-2], h.shape[-2] * h.shape[-1])


def to_chunks(a, c):
    B, L = a.shape[0], a.shape[1]
    a = a.reshape(B, L // c, c, *a.shape[2:])
    if a.ndim == 5:
        return a.transpose(1, 0, 3, 2, 4)
    return a.transpose(1, 0, 3, 2)


def from_chunks(o):
    n, B, H, c, d = o.shape
    return o.transpose(1, 0, 3, 2, 4).reshape(B, n * c, H, d)


def flip_seq(a):
    return jnp.flip(a, axis=1)


def hgrn2_scan(q, k, v, log_f):
    B, L, H, K = q.shape
    V = v.shape[-1]
    c = HG_CHUNK
    mask = jnp.tril(jnp.ones((c, c), bool))[:, :, None]

    def step(state, inp):
        qc, kc, vc, lf = inp
        b = jnp.cumsum(lf, axis=2)
        o = jnp.einsum('bhtk,bhkv->bhtv', qc * jnp.exp(b), state)
        decay = jnp.exp(jnp.where(mask, b[:, :, :, None, :] - b[:, :, None, :, :], -jnp.inf))
        scores = jnp.einsum('bhtk,bhtsk,bhsk->bhts', qc, decay, kc)
        o = o + jnp.einsum('bhts,bhsv->bhtv', scores, vc)
        b_end = b[:, :, -1:, :]
        state = jnp.exp(b_end[:, :, 0, :, None]) * state + jnp.einsum('bhsk,bhsv->bhkv', kc * jnp.exp(b_end - b), vc)
        return state, o

    state0 = jnp.zeros((B, H, K, V), F32)
    _, o = lax.scan(step, state0, (to_chunks(q, c), to_chunks(k, c), to_chunks(v, c), to_chunks(log_f, c)))
    return from_chunks(o)


def mlstm_scan(q, k, v, i_pre, log_f):
    B, L, H, Dh = q.shape
    c = ML_CHUNK
    mask = jnp.tril(jnp.ones((c, c), bool))

    def step(carry, inp):
        C, n, m = carry
        qc, kc, vc, ic, fc = inp
        b = jnp.cumsum(fc, axis=-1)
        log_d = jnp.where(mask, b[..., :, None] - b[..., None, :] + ic[..., None, :], -jnp.inf)
        log_inter = b + m[..., None]
        m_t = jnp.maximum(log_inter, jnp.max(log_d, axis=-1))
        a_inter = jnp.exp(log_inter - m_t)
        s = jnp.einsum('bhtd,bhsd->bhts', qc, kc) * jnp.exp(log_d - m_t[..., None])
        num = a_inter[..., None] * jnp.einsum('bhtd,bhde->bhte', qc, C) + jnp.einsum('bhts,bhse->bhte', s, vc)
        den = a_inter * jnp.einsum('bhtd,bhd->bht', qc, n) + jnp.sum(s, axis=-1)
        h = num / jnp.maximum(jnp.abs(den), jnp.exp(-m_t))[..., None]
        b_end = b[..., -1]
        log_w = b_end[..., None] - b + ic
        m_new = jnp.maximum(b_end + m, jnp.max(log_w, axis=-1))
        a_state = jnp.exp(b_end + m - m_new)
        kw = kc * jnp.exp(log_w - m_new[..., None])[..., None]
        C = a_state[..., None, None] * C + jnp.einsum('bhsd,bhse->bhde', kw, vc)
        n = a_state[..., None] * n + jnp.sum(kw, axis=2)
        return (C, n, m_new), h

    init = (jnp.zeros((B, H, Dh, Dh), F32), jnp.zeros((B, H, Dh), F32), jnp.zeros((B, H), F32))
    _, h = lax.scan(step, init, (to_chunks(q, c), to_chunks(k, c), to_chunks(v, c),
                                 to_chunks(i_pre, c), to_chunks(log_f, c)))
    return from_chunks(h)


def short_conv3(x, w, b):
    xp = jnp.pad(x, ((0, 0), (1, 1), (0, 0)))
    return xp[:, :-2] * w[0] + xp[:, 1:-1] * w[1] + xp[:, 2:] * w[2] + b


def hyena_filter(L, w1, b1, w2, b2, freq, w3):
    pos = jnp.arange(L, dtype=F32)
    t = (pos / (L - 1))[:, None]
    ang = (2.0 * math.pi * pos / L)[:, None] * jnp.linspace(1e-4, HY_BANDS - 1, HY_BANDS, dtype=F32)[None]
    z = jnp.concatenate([t, jnp.cos(ang), -jnp.sin(ang)], axis=-1)
    freq = freq.astype(F32)
    h = jnp.sin(freq * (z @ w1.astype(F32) + b1.astype(F32)))
    h = jnp.sin(freq * (h @ w2.astype(F32) + b2.astype(F32)))
    h = h @ w3.astype(F32)
    max_decay = math.log(HY_DECAY_TARGET) / HY_SHORT_DECAY_PCT
    min_decay = math.log(HY_DECAY_TARGET) / HY_LONG_DECAY_PCT
    deltas = jnp.abs(jnp.linspace(min_decay, max_decay, HY_WIDTH, dtype=F32))
    window = jnp.exp(-t * deltas)
    h_fwd = h[:, :HY_WIDTH] * window
    h_bwd = h[1:, HY_WIDTH:] * window[1:]
    l1 = jnp.sum(jnp.abs(h_fwd), axis=0) + jnp.sum(jnp.abs(h_bwd), axis=0)
    filt = jnp.concatenate([h_fwd, jnp.zeros((1, HY_WIDTH), F32), h_bwd[::-1]], axis=0)
    return filt / l1


def long_conv(u, filt):
    L = u.shape[1]
    U = jnp.fft.rfft(u, n=2 * L, axis=1)
    Kf = jnp.fft.rfft(filt, axis=0)
    return jnp.fft.irfft(U * Kf[None], n=2 * L, axis=1)[:, :L]


def token_mixers(xn, w_in, lb, hg_norm_g, ml_i_bias, ml_f_bias, ml_norm_g, hy_conv_w, hy_conv_b,
                 filt_w1, filt_b1, filt_w2, filt_b2, filt_freq, filt_w3, hy_bias,
                 w_branch_a, w_branch_b, w_branch_c, w_out):
    B, L, _ = xn.shape
    dt = xn.dtype
    proj = xn @ w_in
    sizes = [HG_HEADS * HG_KDIM] * 3 + [HG_WIDTH] * 2 + [ML_WIDTH] * 4 + [ML_HEADS] * 4 + [3 * HY_WIDTH] + [D_MODEL] * N_BRANCHES
    (hq, hff, hfb, hi, hg, mq, mk, mv, mo, mif, mib, mff, mfb, hy, ga, gb, gc) = jnp.split(
        proj, np.cumsum(sizes)[:-1].tolist(), axis=-1)

    def heads(a, nh):
        return a.astype(F32).reshape(B, L, nh, -1)

    q_a = heads(hq, HG_HEADS)
    v_a = heads(hi, HG_HEADS)
    lbf = lb.astype(F32).reshape(HG_HEADS, HG_KDIM)
    log_lb = jnp.log(lbf)
    log_1mlb = jnp.log1p(-lbf)

    def forget(zf):
        z = heads(zf, HG_HEADS)
        log_f = jnp.logaddexp(log_lb, log_1mlb + jax.nn.log_sigmoid(z))
        key = (1.0 - lbf) * jax.nn.sigmoid(-z)
        return key, log_f

    k_af, lf_af = forget(hff)
    k_ab, lf_ab = forget(hfb)
    o_a = hgrn2_scan(q_a, k_af, v_a, lf_af) + flip_seq(
        hgrn2_scan(flip_seq(q_a), flip_seq(k_ab), flip_seq(v_a), flip_seq(lf_ab)))
    y_a = (head_rmsnorm(o_a, hg_norm_g) * jax.nn.silu(hg.astype(F32))).astype(dt)

    q_b = heads(mq, ML_HEADS)
    k_b = heads(mk, ML_HEADS) / math.sqrt(ML_HDIM)
    v_b = heads(mv, ML_HEADS)
    ml_i_bias = ml_i_bias.astype(F32)
    ml_f_bias = ml_f_bias.astype(F32)
    i_fw = mif.astype(F32) + ml_i_bias[0]
    i_bw = mib.astype(F32) + ml_i_bias[1]
    lf_fw = jax.nn.log_sigmoid(mff.astype(F32) + ml_f_bias[0])
    lf_bw = jax.nn.log_sigmoid(mfb.astype(F32) + ml_f_bias[1])
    h_b = mlstm_scan(q_b, k_b, v_b, i_fw, lf_fw) + flip_seq(
        mlstm_scan(flip_seq(q_b), flip_seq(k_b), flip_seq(v_b), flip_seq(i_bw), flip_seq(lf_bw)))
    y_b = (jax.nn.sigmoid(mo.astype(F32)) * head_rmsnorm(h_b, ml_norm_g)).astype(dt)

    hyc = short_conv3(hy, hy_conv_w, hy_conv_b).astype(F32)
    x0, x1, v_c = jnp.split(hyc, 3, axis=-1)
    filt = hyena_filter(L, filt_w1, filt_b1, filt_w2, filt_b2, filt_freq, filt_w3)
    u = x1 * v_c
    y_c = (x0 * (long_conv(u, filt) + u * hy_bias.astype(F32))).astype(dt)

    merged = (jax.nn.sigmoid(ga) * (y_a @ w_branch_a)
              + jax.nn.sigmoid(gb) * (y_b @ w_branch_b)
              + jax.nn.sigmoid(gc) * (y_c @ w_branch_c))
    return merged @ w_out


def routed_ffn(xn, rg_w, rg_b, re_w, re_b, w_gate, w_up, w_down):
    B, L, Dm = xn.shape
    T = B * L
    xt = xn.reshape(T, Dm)
    g_logits = (xt @ rg_w + rg_b).astype(F32)
    g_prob = jax.nn.softmax(g_logits, axis=-1)
    g_sel = jnp.argmax(g_logits, axis=-1).astype(jnp.int32)
    e_logits = (xt @ re_w + re_b).astype(F32).reshape(T, N_GROUPS, EXPERTS_PER_GROUP)
    e_in_group = jnp.take_along_axis(e_logits, g_sel[:, None, None], axis=1)[:, 0]
    top_v, top_i = lax.top_k(e_in_group, TOP_K)
    weights = jax.nn.softmax(top_v, axis=-1) * jnp.take_along_axis(g_prob, g_sel[:, None], axis=1)
    expert_id = (g_sel[:, None] * EXPERTS_PER_GROUP + top_i).reshape(-1).astype(jnp.int32)
    token_id = jnp.repeat(jnp.arange(T, dtype=jnp.int32), TOP_K)
    flat_w = weights.reshape(-1)
    M = T * TOP_K
    order = jnp.argsort(expert_id)
    sorted_e = expert_id[order]
    counts = jnp.bincount(expert_id, length=N_EXPERTS).astype(jnp.int32)
    starts = jnp.cumsum(counts) - counts
    padded = (counts + MOE_BLOCK - 1) // MOE_BLOCK * MOE_BLOCK
    p_ends = jnp.cumsum(padded)
    p_starts = p_ends - padded
    dest = p_starts[sorted_e] + jnp.arange(M, dtype=jnp.int32) - starts[sorted_e]
    n_blocks = -(-M // MOE_BLOCK) + N_EXPERTS
    rows = n_blocks * MOE_BLOCK
    row_tok = jnp.full((rows,), T, jnp.int32).at[dest].set(token_id[order])
    row_w = jnp.zeros((rows,), F32).at[dest].set(flat_w[order])
    block_e = jnp.minimum(jnp.searchsorted(p_ends, jnp.arange(n_blocks, dtype=jnp.int32) * MOE_BLOCK, side='right'),
                          N_EXPERTS - 1).astype(jnp.int32)
    x_rows = jnp.concatenate([xt, jnp.zeros((1, Dm), xt.dtype)], axis=0)[row_tok].reshape(n_blocks, MOE_BLOCK, Dm)

    def expert_block(args):
        xb, e = args
        h = jax.nn.silu(xb @ w_gate[e]) * (xb @ w_up[e])
        return h @ w_down[e]

    out = lax.map(expert_block, (x_rows, block_e)).reshape(rows, Dm)
    y = jnp.zeros((T + 1, Dm), xn.dtype).at[row_tok].add(out * row_w[:, None].astype(out.dtype))
    return y[:T].reshape(B, L, Dm)


def trunk(x, norm_mix_g, w_in, hg_lb, hg_norm_g, ml_i_bias, ml_f_bias, ml_norm_g, hy_conv_w, hy_conv_b,
          filt_w1, filt_b1, filt_w2, filt_b2, filt_freq, filt_w3, hy_bias, w_branch_a, w_branch_b, w_branch_c,
          w_out, norm_ffn_g, router_group_w, router_group_b, router_expert_w, router_expert_b,
          exp_w_gate, exp_w_up, exp_w_down, final_norm_g):
    lbs = jnp.cumsum(jax.nn.softmax(hg_lb.astype(F32), axis=0), axis=0)
    lbs = lbs - lbs[0:1]
    for l in range(DEPTH):
        xn = rmsnorm(x, norm_mix_g[l])
        x = x + token_mixers(xn, w_in[l], lbs[l], hg_norm_g[l], ml_i_bias[l], ml_f_bias[l], ml_norm_g[l],
                             hy_conv_w[l], hy_conv_b[l], filt_w1[l], filt_b1[l], filt_w2[l], filt_b2[l],
                             filt_freq[l], filt_w3[l], hy_bias[l], w_branch_a[l], w_branch_b[l], w_branch_c[l],
                             w_out[l])
        xn = rmsnorm(x, norm_ffn_g[l])
        x = x + routed_ffn(xn, router_group_w[l], router_group_b[l], router_expert_w[l], router_expert_b[l],
                           exp_w_gate[l], exp_w_up[l], exp_w_down[l])
    return rmsnorm(x, final_norm_g)


def setup_inputs(seed: int = 0) -> dict:
    key = jax.random.key(seed)
    ks = iter(jax.random.split(key, 40))

    def nrm(shape, scale):
        return scale * jax.random.normal(next(ks), shape, F32)

    Dm = D_MODEL
    Hf = HY_FILTER_HIDDEN
    return {
        'x_prompt': nrm((BATCH, SEQ, Dm), 1.0),
        'x_sample': nrm((DEC_BATCH, DEC_SEQ, Dm), 1.0),
        'norm_mix_g': 1.0 + nrm((DEPTH, Dm), 0.01),
        'w_in': nrm((DEPTH, Dm, N_IN), Dm ** -0.5),
        'hg_lb': nrm((DEPTH, HG_HEADS * HG_KDIM), 0.1),
        'hg_norm_g': 1.0 + nrm((DEPTH, HG_WIDTH), 0.01),
        'ml_i_bias': nrm((DEPTH, 2, ML_HEADS), 0.1),
        'ml_f_bias': jnp.linspace(3.0, 6.0, ML_HEADS, dtype=F32) + nrm((DEPTH, 2, ML_HEADS), 0.01),
        'ml_norm_g': 1.0 + nrm((DEPTH, ML_WIDTH), 0.01),
        'hy_conv_w': nrm((DEPTH, 3, 3 * HY_WIDTH), 3 ** -0.5),
        'hy_conv_b': nrm((DEPTH, 3 * HY_WIDTH), 0.01),
        'filt_w1': nrm((DEPTH, HY_EMB, Hf), HY_EMB ** -0.5),
        'filt_b1': nrm((DEPTH, Hf), 0.1),
        'filt_w2': nrm((DEPTH, Hf, Hf), Hf ** -0.5),
        'filt_b2': nrm((DEPTH, Hf), 0.1),
        'filt_freq': 1.0 + nrm((DEPTH, Hf), 0.01),
        'filt_w3': nrm((DEPTH, Hf, 2 * HY_WIDTH), Hf ** -0.5),
        'hy_bias': nrm((DEPTH, HY_WIDTH), 0.1),
        'w_branch_a': nrm((DEPTH, HG_WIDTH, Dm), HG_WIDTH ** -0.5),
        'w_branch_b': nrm((DEPTH, ML_WIDTH, Dm), ML_WIDTH ** -0.5),
        'w_branch_c': nrm((DEPTH, HY_WIDTH, Dm), HY_WIDTH ** -0.5),
        'w_out': nrm((DEPTH, Dm, Dm), Dm ** -0.5),
        'norm_ffn_g': 1.0 + nrm((DEPTH, Dm), 0.01),
        'router_group_w': nrm((DEPTH, Dm, N_GROUPS), Dm ** -0.5),
        'router_group_b': nrm((DEPTH, N_GROUPS), 0.01),
        'router_expert_w': nrm((DEPTH, Dm, N_EXPERTS), Dm ** -0.5),
        'router_expert_b': nrm((DEPTH, N_EXPERTS), 0.01),
        'exp_w_gate': nrm((DEPTH, N_EXPERTS, Dm, EXPERT_HIDDEN), Dm ** -0.5),
        'exp_w_up': nrm((DEPTH, N_EXPERTS, Dm, EXPERT_HIDDEN), Dm ** -0.5),
        'exp_w_down': nrm((DEPTH, N_EXPERTS, EXPERT_HIDDEN, Dm), EXPERT_HIDDEN ** -0.5),
        'final_norm_g': 1.0 + nrm((Dm,), 0.01),
    }


def reference(x_prompt, x_sample, norm_mix_g, w_in, hg_lb, hg_norm_g, ml_i_bias, ml_f_bias, ml_norm_g,
              hy_conv_w, hy_conv_b, filt_w1, filt_b1, filt_w2, filt_b2, filt_freq, filt_w3, hy_bias,
              w_branch_a, w_branch_b, w_branch_c, w_out, norm_ffn_g, router_group_w, router_group_b,
              router_expert_w, router_expert_b, exp_w_gate, exp_w_up, exp_w_down, final_norm_g):
    y_prompt = trunk(x_prompt, norm_mix_g, w_in, hg_lb, hg_norm_g, ml_i_bias, ml_f_bias, ml_norm_g, hy_conv_w,
                     hy_conv_b, filt_w1, filt_b1, filt_w2, filt_b2, filt_freq, filt_w3, hy_bias, w_branch_a,
                     w_branch_b, w_branch_c, w_out, norm_ffn_g, router_group_w, router_group_b, router_expert_w,
                     router_expert_b, exp_w_gate, exp_w_up, exp_w_down, final_norm_g)
    y_sample = trunk(x_sample, norm_mix_g, w_in, hg_lb, hg_norm_g, ml_i_bias, ml_f_bias, ml_norm_g, hy_conv_w,
                     hy_conv_b, filt_w1, filt_b1, filt_w2, filt_b2, filt_freq, filt_w3, hy_bias, w_branch_a,
                     w_branch_b, w_branch_c, w_out, norm_ffn_g, router_group_w, router_group_b, router_expert_w,
                     router_expert_b, exp_w_gate, exp_w_up, exp_w_down, final_norm_g)
    return (y_prompt, y_sample)
```

```python
import functools
import math

import numpy as np
import jax
import jax.numpy as jnp
from jax import lax
from jax.experimental import pallas as pl
from jax.experimental.pallas import tpu as pltpu

F32 = jnp.float32
BF16 = jnp.bfloat16
HIGHEST = lax.Precision.HIGHEST

D_MODEL = 1024
EPS = 1e-6
DEPTH = 2
HG_HEADS, HG_KDIM, HG_WIDTH, HG_CHUNK = 8, 64, 512, 32
ML_HEADS, ML_HDIM, ML_WIDTH, ML_CHUNK = 4, 128, 512, 128
HY_WIDTH, HY_BANDS, HY_FILTER_HIDDEN = 512, 16, 64
HY_EMB = 1 + 2 * HY_BANDS
HY_SHORT_DECAY_PCT, HY_LONG_DECAY_PCT, HY_DECAY_TARGET = 0.3, 1.5, 1e-2
N_GROUPS, EXPERTS_PER_GROUP, TOP_K = 4, 8, 2
N_EXPERTS = N_GROUPS * EXPERTS_PER_GROUP
EXPERT_HIDDEN = D_MODEL // 2
MOE_BLOCK = 128

LANES = 128
BRANCH_W = 512
SLAB_HQ, SLAB_HFF, SLAB_HFB, SLAB_HI, SLAB_HG = 0, 1, 2, 3, 4
SLAB_MQ, SLAB_MK, SLAB_MV, SLAB_MO = 5, 6, 7, 8
SLAB_HY = 9
PROJ_MAIN = 12 * BRANCH_W
PROJ_W = PROJ_MAIN + LANES
GATE_BLOCK = PROJ_MAIN // LANES
VMEM_LIMIT = 48 * 1024 * 1024


def _cparams(sem):
    return pltpu.CompilerParams(dimension_semantics=sem, vmem_limit_bytes=VMEM_LIMIT)


def _rms(x, g):
    return x * lax.rsqrt(jnp.mean(x * x, axis=-1, keepdims=True) + EPS) * g


def _log_sigmoid(z):
    return jnp.minimum(z, 0.0) - jnp.log1p(jnp.exp(-jnp.abs(z)))


def _inproj_kernel(x_ref, g_ref, w_ref, o_ref, xn_ref):
    @pl.when(pl.program_id(1) == 0)
    def _():
        xn_ref[...] = _rms(x_ref[...], g_ref[...]).astype(BF16)

    o_ref[...] = jnp.dot(xn_ref[...], w_ref[...], preferred_element_type=F32)


def _inproj_combine_kernel(x_ref, e_ref, r_ref, g_ref, w_ref, o_ref, xo_ref, xn_ref):
    @pl.when(pl.program_id(1) == 0)
    def _():
        r = r_ref[...]
        x = (x_ref[...] + e_ref[:, :D_MODEL] * r[:, 2:3] + e_ref[:, D_MODEL:] * r[:, 3:4])
        xo_ref[...] = x
        xn_ref[...] = _rms(x, g_ref[...]).astype(BF16)

    o_ref[...] = jnp.dot(xn_ref[...], w_ref[...], preferred_element_type=F32)


def _in_projection(x, g, w, moe=None):
    T = x.shape[0]
    tm = min(1024 if moe is None else 512, T)
    tn = PROJ_W // 7
    grid = (T // tm, PROJ_W // tn)
    xspec = pl.BlockSpec((tm, D_MODEL), lambda i, j: (i, 0))
    gspec = pl.BlockSpec((1, D_MODEL), lambda i, j: (0, 0))
    wspec = pl.BlockSpec((D_MODEL, tn), lambda i, j: (0, j))
    ospec = pl.BlockSpec((tm, tn), lambda i, j: (i, j))
    scratch = [pltpu.VMEM((tm, D_MODEL), BF16)]
    cp = _cparams(("parallel", "arbitrary"))
    if moe is None:
        return pl.pallas_call(
            _inproj_kernel, grid=grid, in_specs=[xspec, gspec, wspec], out_specs=ospec,
            out_shape=jax.ShapeDtypeStruct((T, PROJ_W), F32), scratch_shapes=scratch,
            compiler_params=cp, name="in_projection")(x, g, w)
    e2, route = moe
    espec = pl.BlockSpec((tm, 2 * D_MODEL), lambda i, j: (i, 0))
    rspec = pl.BlockSpec((tm, LANES), lambda i, j: (i, 0))
    proj, xo = pl.pallas_call(
        _inproj_combine_kernel, grid=grid, in_specs=[xspec, espec, rspec, gspec, wspec],
        out_specs=[ospec, xspec],
        out_shape=[jax.ShapeDtypeStruct((T, PROJ_W), F32), jax.ShapeDtypeStruct((T, D_MODEL), F32)],
        scratch_shapes=scratch, compiler_params=cp, name="in_projection_combine")(x, e2, route, g, w)
    return proj, xo


HG_TB = 256
HG_PAIRS = HG_WIDTH // LANES


def _hgrn2_kernel(*refs, reverse, finalize):
    if finalize:
        (q_ref, z_ref, v_ref, lb_ref, tri_ref, bd_ref, of_ref, gate_ref, ng_ref,
         o_ref, st_ref, kpad, vpad, bpad, abuf, scbuf) = refs
    else:
        (q_ref, z_ref, v_ref, lb_ref, tri_ref, bd_ref,
         o_ref, st_ref, kpad, vpad, bpad, abuf, scbuf) = refs
    C = HG_CHUNK
    n_chunks = HG_TB // C

    @pl.when(pl.program_id(1) == 0)
    def _():
        st_ref[...] = jnp.zeros_like(st_ref)

    kpad[...] = jnp.zeros_like(kpad)
    vpad[...] = jnp.zeros_like(vpad)
    bpad[...] = jnp.zeros_like(bpad)

    log_lb = lb_ref[0:1, :]
    log_1mlb = lb_ref[1:2, :]
    one_m_lb = lb_ref[2:3, :]
    tri = tri_ref[...]
    bd = bd_ref[...]
    bd_bf = bd.astype(BF16)

    def chunk(i, carry):
        ci = (n_chunks - 1 - i) if reverse else i
        off = pl.multiple_of(ci * C, C)
        q = q_ref[pl.ds(off, C), :]
        z = z_ref[pl.ds(off, C), :]
        v = v_ref[pl.ds(off, C), :]
        ls = _log_sigmoid(z)
        hi = log_1mlb + ls
        mx = jnp.maximum(log_lb, hi)
        lf = mx + jnp.log1p(jnp.exp(-jnp.abs(log_lb - hi)))
        kk = one_m_lb * jnp.exp(ls - z)
        b = jnp.dot(tri, lf, precision=HIGHEST, preferred_element_type=F32)
        btot = b[0:1, :] if reverse else b[C - 1:C, :]
        qb = (q * jnp.exp(b)).astype(BF16)
        kb = (kk * jnp.exp(btot - b)).astype(BF16)
        dec = jnp.exp(btot)
        vb = v.astype(BF16)

        kpad[C:2 * C, :] = kk
        vpad[C:2 * C, :] = v
        bpad[C:2 * C, :] = b
        for d in range(C):
            st = C + d if reverse else C - d
            if d == 0:
                a = q * kk
            else:
                a = q * kpad[st:st + C, :] * jnp.exp(b - bpad[st:st + C, :])
            abuf[d * C:(d + 1) * C, :] = a.astype(BF16)
        for p in range(HG_PAIRS):
            sl = slice(p * LANES, (p + 1) * LANES)
            scbuf[:, sl] = jnp.dot(abuf[:, sl], bd_bf, preferred_element_type=F32)

        parts = []
        for p in range(HG_PAIRS):
            sl = slice(p * LANES, (p + 1) * LANES)
            s_t = st_ref[p]
            parts.append(lax.dot_general(qb[:, sl], s_t.astype(BF16), (((1,), (1,)), ((), ())),
                                         preferred_element_type=F32))
            upd = lax.dot_general(vb[:, sl], kb[:, sl], (((0,), (0,)), ((), ())),
                                  preferred_element_type=F32)
            st_ref[p] = s_t * dec[:, sl] + upd * bd
        acc = jnp.concatenate(parts, axis=1)
        for d in range(C):
            st = C + d if reverse else C - d
            acc = acc + scbuf[d * C:(d + 1) * C, :] * vpad[st:st + C, :]

        if finalize:
            o = acc + of_ref[pl.ds(off, C), :]
            ms = jnp.concatenate(
                [jnp.dot(o[:, p * LANES:(p + 1) * LANES] ** 2, bd, precision=HIGHEST,
                         preferred_element_type=F32) for p in range(HG_PAIRS)], axis=1) * (1.0 / HG_KDIM)
            g = gate_ref[pl.ds(off, C), :]
            y = o * lax.rsqrt(ms + EPS) * ng_ref[...] * (g * jax.nn.sigmoid(g))
            o_ref[pl.ds(off, C), :] = y.astype(o_ref.dtype)
        else:
            o_ref[pl.ds(off, C), :] = acc
        return carry

    lax.fori_loop(0, n_chunks, chunk, 0)


def _hgrn2(proj, lbrow, norm_g, row0, B, L):
    C = HG_CHUNK
    nT = L // HG_TB
    blk0 = row0 // HG_TB
    tri_f = jnp.asarray(np.tril(np.ones((C, C), np.float32)))
    tri_b = jnp.asarray(np.triu(np.ones((C, C), np.float32)))
    head = np.arange(LANES) // HG_KDIM
    bd = jnp.asarray((head[:, None] == head[None, :]).astype(np.float32))

    def in_spec(slab, reverse):
        if reverse:
            return pl.BlockSpec((HG_TB, BRANCH_W), lambda b, t: (blk0 + b * nT + nT - 1 - t, slab))
        return pl.BlockSpec((HG_TB, BRANCH_W), lambda b, t: (blk0 + b * nT + t, slab))

    def out_spec(reverse):
        if reverse:
            return pl.BlockSpec((HG_TB, BRANCH_W), lambda b, t: (b * nT + nT - 1 - t, 0))
        return pl.BlockSpec((HG_TB, BRANCH_W), lambda b, t: (b * nT + t, 0))

    const = lambda shape: pl.BlockSpec(shape, lambda b, t: (0,) * len(shape))
    scratch = [pltpu.VMEM((HG_PAIRS, LANES, LANES), F32),
               pltpu.VMEM((3 * C, BRANCH_W), F32), pltpu.VMEM((3 * C, BRANCH_W), F32),
               pltpu.VMEM((3 * C, BRANCH_W), F32),
               pltpu.VMEM((C * C, BRANCH_W), BF16), pltpu.VMEM((C * C, BRANCH_W), F32)]
    cp = _cparams(("parallel", "arbitrary"))
    o_f = pl.pallas_call(
        functools.partial(_hgrn2_kernel, reverse=False, finalize=False), grid=(B, nT),
        in_specs=[in_spec(SLAB_HQ, False), in_spec(SLAB_HFF, False), in_spec(SLAB_HI, False),
                  const((8, BRANCH_W)), const((C, C)), const((LANES, LANES))],
        out_specs=out_spec(False), out_shape=jax.ShapeDtypeStruct((B * L, BRANCH_W), F32),
        scratch_shapes=scratch, compiler_params=cp, name="hgrn2_fwd")(proj, proj, proj, lbrow, tri_f, bd)
    return pl.pallas_call(
        functools.partial(_hgrn2_kernel, reverse=True, finalize=True), grid=(B, nT),
        in_specs=[in_spec(SLAB_HQ, True), in_spec(SLAB_HFB, True), in_spec(SLAB_HI, True),
                  const((8, BRANCH_W)), const((C, C)), const((LANES, LANES)),
                  out_spec(True), in_spec(SLAB_HG, True), const((1, BRANCH_W))],
        out_specs=out_spec(True), out_shape=jax.ShapeDtypeStruct((B * L, BRANCH_W), BF16),
        scratch_shapes=scratch, compiler_params=cp, name="hgrn2_bwd")(
            proj, proj, proj, lbrow, tri_b, bd, o_f, proj, norm_g)


def _mlstm_kernel(*refs, reverse, finalize):
    if finalize:
        (q_ref, k_ref, v_ref, gt_ref, gb_ref, tri_ref, hf_ref, og_ref, ng_ref,
         o_ref, c_ref, n_ref, m_ref) = refs
    else:
        (q_ref, k_ref, v_ref, gt_ref, gb_ref, tri_ref, o_ref, c_ref, n_ref, m_ref) = refs
    C = ML_CHUNK

    @pl.when(pl.program_id(1) == 0)
    def _():
        c_ref[...] = jnp.zeros_like(c_ref)
        n_ref[...] = jnp.zeros_like(n_ref)
        m_ref[...] = jnp.zeros_like(m_ref)

    tri = tri_ref[...]
    gates = gt_ref[...] + gb_ref[...]
    lane = lax.broadcasted_iota(jnp.int32, (C, LANES), 1)
    g_col = jnp.where(lane >= 2 * ML_HEADS, _log_sigmoid(gates), gates)
    g_row = g_col.T
    b_col = jnp.dot(tri, g_col, precision=HIGHEST, preferred_element_type=F32)
    b_row = lax.dot_general(g_row, tri, (((1,), (1,)), ((), ())), precision=HIGHEST,
                            preferred_element_type=F32)
    ti = lax.broadcasted_iota(jnp.int32, (C, C), 0)
    si = lax.broadcasted_iota(jnp.int32, (C, C), 1)
    causal = (si >= ti) if reverse else (si <= ti)
    dsel = ML_HEADS if reverse else 0
    edge = 0 if reverse else C - 1
    scale = 1.0 / math.sqrt(ML_HDIM)

    outs = []
    for h in range(ML_HEADS):
        sl = slice(h * ML_HDIM, (h + 1) * ML_HDIM)
        li, lf = dsel + h, 2 * ML_HEADS + dsel + h
        q = q_ref[:, sl]
        kc = k_ref[:, sl] * scale
        v = v_ref[:, sl]
        qb, kb, vb = q.astype(BF16), kc.astype(BF16), v.astype(BF16)
        bc = b_col[:, lf:lf + 1]
        br = b_row[lf:lf + 1, :]
        ic = g_col[:, li:li + 1]
        ir = g_row[li:li + 1, :]
        btot = b_col[edge:edge + 1, lf:lf + 1]
        m_prev = m_ref[0:1, h:h + 1]
        n_prev = n_ref[h:h + 1, :]
        c_prev = c_ref[h]

        log_d = jnp.where(causal, bc - br + ir, -jnp.inf)
        log_inter = bc + m_prev
        m_t = jnp.maximum(log_inter, jnp.max(log_d, axis=1, keepdims=True))
        a_inter = jnp.exp(log_inter - m_t)
        s = lax.dot_general(qb, kb, (((1,), (1,)), ((), ())), preferred_element_type=F32)
        s = s * jnp.exp(log_d - m_t)
        num = a_inter * jnp.dot(qb, c_prev.astype(BF16), preferred_element_type=F32) \
            + jnp.dot(s.astype(BF16), vb, preferred_element_type=F32)
        den = a_inter * jnp.sum(q * n_prev, axis=1, keepdims=True) + jnp.sum(s, axis=1, keepdims=True)
        outs.append(num / jnp.maximum(jnp.abs(den), jnp.exp(-m_t)))

        log_w = btot - bc + ic
        m_new = jnp.maximum(btot + m_prev, jnp.max(log_w, axis=0, keepdims=True))
        a_state = jnp.exp(btot + m_prev - m_new)
        kw = kc * jnp.exp(log_w - m_new)
        c_ref[h] = a_state * c_prev + lax.dot_general(kw.astype(BF16), vb, (((0,), (0,)), ((), ())),
                                                      preferred_element_type=F32)
        n_ref[h:h + 1, :] = a_state * n_prev + jnp.sum(kw, axis=0, keepdims=True)
        m_ref[0:1, h:h + 1] = m_new

    hcur = jnp.concatenate(outs, axis=1)
    if finalize:
        hsum = hcur + hf_ref[...]
        ys = []
        for h in range(ML_HEADS):
            sl = slice(h * ML_HDIM, (h + 1) * ML_HDIM)
            hh = hsum[:, sl]
            ys.append(hh * lax.rsqrt(jnp.mean(hh * hh, axis=1, keepdims=True) + EPS))
        y = jnp.concatenate(ys, axis=1) * ng_ref[...] * jax.nn.sigmoid(og_ref[...])
        o_ref[...] = y.astype(o_ref.dtype)
    else:
        o_ref[...] = hcur


def _mlstm(proj, gate_bias, norm_g, row0, B, L):
    C = ML_CHUNK
    nT = L // C
    blk0 = row0 // C
    tri_f = jnp.asarray(np.tril(np.ones((C, C), np.float32)))
    tri_b = jnp.asarray(np.triu(np.ones((C, C), np.float32)))

    def rows(reverse):
        if reverse:
            return lambda b, t: blk0 + b * nT + nT - 1 - t
        return lambda b, t: blk0 + b * nT + t

    def in_spec(slab, reverse):
        r = rows(reverse)
        return pl.BlockSpec((C, BRANCH_W), lambda b, t: (r(b, t), slab))

    def gate_spec(reverse):
        r = rows(reverse)
        return pl.BlockSpec((C, LANES), lambda b, t: (r(b, t), GATE_BLOCK))

    def out_spec(reverse):
        if reverse:
            return pl.BlockSpec((C, BRANCH_W), lambda b, t: (b * nT + nT - 1 - t, 0))
        return pl.BlockSpec((C, BRANCH_W), lambda b, t: (b * nT + t, 0))

    const = lambda shape: pl.BlockSpec(shape, lambda b, t: (0,) * len(shape))
    scratch = [pltpu.VMEM((ML_HEADS, ML_HDIM, ML_HDIM), F32), pltpu.VMEM((8, ML_HDIM), F32),
               pltpu.VMEM((8, LANES), F32)]
    cp = _cparams(("parallel", "arbitrary"))
    h_f = pl.pallas_call(
        functools.partial(_mlstm_kernel, reverse=False, finalize=False), grid=(B, nT),
        in_specs=[in_spec(SLAB_MQ, False), in_spec(SLAB_MK, False), in_spec(SLAB_MV, False),
                  gate_spec(False), const((1, LANES)), const((C, C))],
        out_specs=out_spec(False), out_shape=jax.ShapeDtypeStruct((B * L, BRANCH_W), F32),
        scratch_shapes=scratch, compiler_params=cp, name="mlstm_fwd")(
            proj, proj, proj, proj, gate_bias, tri_f)
    return pl.pallas_call(
        functools.partial(_mlstm_kernel, reverse=True, finalize=True), grid=(B, nT),
        in_specs=[in_spec(SLAB_MQ, True), in_spec(SLAB_MK, True), in_spec(SLAB_MV, True),
                  gate_spec(True), const((1, LANES)), const((C, C)),
                  out_spec(True), in_spec(SLAB_MO, True), const((1, BRANCH_W))],
        out_specs=out_spec(True), out_shape=jax.ShapeDtypeStruct((B * L, BRANCH_W), BF16),
        scratch_shapes=scratch, compiler_params=cp, name="mlstm_bwd")(
            proj, proj, proj, proj, gate_bias, tri_b, h_f, proj, norm_g)


HY_TB = 256


def _shortconv_kernel(c_ref, p_ref, n_ref, w_ref, b_ref, x0_ref, u_ref, *, nT):
    t = pl.program_id(1)
    cur = c_ref[...]
    prev_row = jnp.where(t > 0, p_ref[7:8, :], 0.0)
    next_row = jnp.where(t < nT - 1, n_ref[0:1, :], 0.0)
    row = lax.broadcasted_iota(jnp.int32, cur.shape, 0)
    up = jnp.where(row == 0, prev_row, pltpu.roll(cur, 1, axis=0))
    dn = jnp.where(row == HY_TB - 1, next_row, pltpu.roll(cur, HY_TB - 1, axis=0))
    y = up * w_ref[0:1, :] + cur * w_ref[1:2, :] + dn * w_ref[2:3, :] + b_ref[...]
    x0_ref[...] = y[:, :HY_WIDTH]
    u_ref[...] = y[:, HY_WIDTH:2 * HY_WIDTH] * y[:, 2 * HY_WIDTH:]


def _short_conv(proj, w, b, row0, B, L):
    nT = L // HY_TB
    blk0 = row0 // HY_TB
    sub = HY_TB // 8
    W3 = 3 * HY_WIDTH
    slab = SLAB_HY * BRANCH_W // W3
    cur = pl.BlockSpec((HY_TB, W3), lambda bb, t: (blk0 + bb * nT + t, slab))
    prv = pl.BlockSpec((8, W3), lambda bb, t: (jnp.maximum((blk0 + bb * nT + t) * sub - 1, 0), slab))
    nxt = pl.BlockSpec((8, W3), lambda bb, t: (jnp.minimum((blk0 + bb * nT + t + 1) * sub,
                                                           (blk0 + B * nT) * sub - 1), slab))
    const = lambda shape: pl.BlockSpec(shape, lambda bb, t: (0,) * len(shape))
    out = pl.BlockSpec((HY_TB, HY_WIDTH), lambda bb, t: (bb * nT + t, 0))
    return pl.pallas_call(
        functools.partial(_shortconv_kernel, nT=nT), grid=(B, nT),
        in_specs=[cur, prv, nxt, const((8, W3)), const((1, W3))], out_specs=[out, out],
        out_shape=[jax.ShapeDtypeStruct((B * L, HY_WIDTH), F32)] * 2,
        compiler_params=_cparams(("parallel", "parallel")), name="hyena_short_conv")(proj, proj, proj, w, b)


HYF_TB = 256


def _filter_kernel(band_ref, w1_ref, b1_ref, w2_ref, b2_ref, fr_ref, w3_ref, dl_ref, h_ref, l1_ref, *, L):
    i = pl.program_id(0)
    pos = (lax.broadcasted_iota(jnp.int32, (HYF_TB, LANES), 0) + i * HYF_TB).astype(F32)
    lane = lax.broadcasted_iota(jnp.int32, (HYF_TB, LANES), 1)
    t = pos / (L - 1)
    ang = (2.0 * math.pi * pos / L) * band_ref[...]
    z = jnp.where(lane == 0, t,
                  jnp.where(lane <= HY_BANDS, jnp.cos(ang),
                            jnp.where(lane <= 2 * HY_BANDS, -jnp.sin(ang), 0.0)))
    fr = fr_ref[...]
    h = jnp.sin(fr * (jnp.dot(z, w1_ref[...], precision=HIGHEST, preferred_element_type=F32) + b1_ref[...]))
    h = jnp.sin(fr * (jnp.dot(h, w2_ref[...], precision=HIGHEST, preferred_element_type=F32) + b2_ref[...]))
    h = jnp.dot(h, w3_ref[...], precision=HIGHEST, preferred_element_type=F32)
    tt = (lax.broadcasted_iota(jnp.int32, (HYF_TB, HY_WIDTH), 0) + i * HYF_TB).astype(F32) / (L - 1)
    window = jnp.exp(-tt * dl_ref[...])
    rowi = lax.broadcasted_iota(jnp.int32, (HYF_TB, HY_WIDTH), 0) + i * HYF_TB
    hf = h[:, :HY_WIDTH] * window
    hb = jnp.where(rowi == 0, 0.0, h[:, HY_WIDTH:] * window)
    h_ref[0] = hf
    h_ref[1] = hb

    @pl.when(i == 0)
    def _():
        l1_ref[...] = jnp.zeros_like(l1_ref)

    l1_ref[...] += jnp.sum(jnp.abs(hf) + jnp.abs(hb), axis=0, keepdims=True)


def _hyena_filter(L, w1, b1, w2, b2, freq, w3):
    band = np.zeros((1, LANES), np.float32)
    bands = np.linspace(1e-4, HY_BANDS - 1, HY_BANDS, dtype=np.float32)
    band[0, 1:1 + HY_BANDS] = bands
    band[0, 1 + HY_BANDS:1 + 2 * HY_BANDS] = bands
    max_decay = math.log(HY_DECAY_TARGET) / HY_SHORT_DECAY_PCT
    min_decay = math.log(HY_DECAY_TARGET) / HY_LONG_DECAY_PCT
    deltas = np.abs(np.linspace(min_decay, max_decay, HY_WIDTH, dtype=np.float32))[None, :]
    w1p = jnp.zeros((LANES, HY_FILTER_HIDDEN), F32).at[:HY_EMB].set(w1.astype(F32))
    const = lambda shape: pl.BlockSpec(shape, lambda i: (0,) * len(shape))
    H = HY_FILTER_HIDDEN
    return pl.pallas_call(
        functools.partial(_filter_kernel, L=L), grid=(L // HYF_TB,),
        in_specs=[const((1, LANES)), const((LANES, H)), const((1, H)), const((H, H)), const((1, H)),
                  const((1, H)), const((H, 2 * HY_WIDTH)), const((1, HY_WIDTH))],
        out_specs=[pl.BlockSpec((2, HYF_TB, HY_WIDTH), lambda i: (0, i, 0)), const((1, HY_WIDTH))],
        out_shape=[jax.ShapeDtypeStruct((2, L, HY_WIDTH), F32), jax.ShapeDtypeStruct((1, HY_WIDTH), F32)],
        compiler_params=_cparams(("arbitrary",)), name="hyena_filter")(
            jnp.asarray(band), w1p, b1.astype(F32)[None], w2.astype(F32), b2.astype(F32)[None],
            freq.astype(F32)[None], w3.astype(F32), jnp.asarray(deltas))


def _fft_factors(n):
    lg = int(round(math.log2(n)))
    n1 = 1 << (lg // 2)
    return n1, n // n1


def _dft(n):
    k = np.arange(n)
    a = -2.0 * np.pi * ((k[:, None] * k[None, :]) % n) / n
    return np.cos(a), np.sin(a)


FFT_G = 4


def _fft1_kernel(u_ref, f_ref, yr_ref, yi_ref, *, n1):
    y = jnp.dot(f_ref[...], u_ref[0], precision=HIGHEST, preferred_element_type=F32)
    yr_ref[0] = y[:n1]
    yi_ref[0] = y[n1:]


def _fft_stage1(u, n1, n2):
    B, L, C = u.shape
    fr, fi = _dft(n1)
    fst = jnp.asarray(np.concatenate([fr[:, :n1 // 2], fi[:, :n1 // 2]], 0).astype(np.float32))
    uv = u.reshape(B, n1 // 2, n2 * C)
    G = min(FFT_G, n2)
    blk = pl.BlockSpec((1, n1 // 2, G * C), lambda b, j: (b, 0, j))
    oblk = pl.BlockSpec((1, n1, G * C), lambda b, j: (b, 0, j))
    shp = jax.ShapeDtypeStruct((B, n1, n2 * C), F32)
    return pl.pallas_call(
        functools.partial(_fft1_kernel, n1=n1), grid=(B, n2 // G),
        in_specs=[blk, pl.BlockSpec((2 * n1, n1 // 2), lambda b, j: (0, 0))],
        out_specs=[oblk, oblk], out_shape=[shp, shp],
        compiler_params=_cparams(("parallel", "parallel")), name="fft_stage1")(uv, fst)


def _cmul(ar, ai, br, bi):
    return ar * br - ai * bi, ar * bi + ai * br


def _fft2_fwd(yr, yi, tr, ti, f2, n2, C):
    ar, ai = _cmul(yr, yi, tr, ti)
    p = jnp.dot(f2, jnp.concatenate([ar, ai], axis=1), precision=HIGHEST, preferred_element_type=F32)
    return p[:n2, :C] - p[n2:, C:], p[:n2, C:] + p[n2:, :C]


def _fft2_filter_kernel(yr_ref, yi_ref, tr_ref, ti_ref, f_ref, xr_ref, xi_ref, *, n2):
    C = yr_ref.shape[2]
    tr = jnp.tile(tr_ref[0], (1, C // LANES))
    ti = jnp.tile(ti_ref[0], (1, C // LANES))
    xr, xi = _fft2_fwd(yr_ref[0], yi_ref[0], tr, ti, f_ref[...], n2, C)
    xr_ref[0] = xr
    xi_ref[0] = xi


def _fft2_conv_kernel(yr_ref, yi_ref, tr_ref, ti_ref, f_ref, gr_ref, gi_ref, hr_ref, hi_ref, l1_ref,
                      zr_ref, zi_ref, *, n2, n):
    C = yr_ref.shape[2]
    tr = jnp.tile(tr_ref[0], (1, C // LANES))
    ti = jnp.tile(ti_ref[0], (1, C // LANES))
    f2 = f_ref[...]
    xr, xi = _fft2_fwd(yr_ref[0], yi_ref[0], tr, ti, f2, n2, C)
    inv = 1.0 / l1_ref[...]
    kr = (gr_ref[0] + hr_ref[0]) * inv
    ki = (gi_ref[0] - hi_ref[0]) * inv
    vr, vi = _cmul(xr, xi, kr, ki)
    q = jnp.dot(f2, jnp.concatenate([vr, vi], axis=1), precision=HIGHEST, preferred_element_type=F32)
    wr = q[:n2, :C] + q[n2:, C:]
    wi = q[:n2, C:] - q[n2:, :C]
    zr, zi = _cmul(wr, wi, tr, -ti)
    zr_ref[0] = zr * (1.0 / n)
    zi_ref[0] = zi * (1.0 / n)


def _fft_tables(n1, n2):
    n = n1 * n2
    k1 = jnp.arange(n1, dtype=jnp.int32)[:, None]
    j2 = jnp.arange(n2, dtype=jnp.int32)[None, :]
    a = (-2.0 * math.pi / n) * (k1 * j2).astype(F32)
    tr = jnp.broadcast_to(jnp.cos(a)[:, :, None], (n1, n2, LANES))
    ti = jnp.broadcast_to(jnp.sin(a)[:, :, None], (n1, n2, LANES))
    fr, fi = _dft(n2)
    f2 = jnp.asarray(np.concatenate([fr, fi], 0).astype(np.float32))
    return tr, ti, f2


def _fft_stage2_filter(yr, yi, tables, n1, n2):
    B = yr.shape[0]
    C = yr.shape[2] // n2
    tr, ti, f2 = tables
    yr = yr.reshape(B, n1 * n2, C)
    yi = yi.reshape(B, n1 * n2, C)
    blk = pl.BlockSpec((1, n2, C), lambda b, k: (b, k, 0))
    tblk = pl.BlockSpec((1, n2, LANES), lambda b, k: (k, 0, 0))
    shp = jax.ShapeDtypeStruct((B, n1 * n2, C), F32)
    return pl.pallas_call(
        functools.partial(_fft2_filter_kernel, n2=n2), grid=(B, n1),
        in_specs=[blk, blk, tblk, tblk, pl.BlockSpec((2 * n2, n2), lambda b, k: (0, 0))],
        out_specs=[blk, blk], out_shape=[shp, shp],
        compiler_params=_cparams(("parallel", "parallel")), name="fft_stage2_filter")(yr, yi, tr, ti, f2)


def _fft_stage2_conv(yr, yi, tables, spec_r, spec_i, l1, n1, n2):
    B = yr.shape[0]
    C = yr.shape[2] // n2
    tr, ti, f2 = tables
    yr = yr.reshape(B, n1 * n2, C)
    yi = yi.reshape(B, n1 * n2, C)
    blk = pl.BlockSpec((1, n2, C), lambda b, k: (b, k, 0))
    tblk = pl.BlockSpec((1, n2, LANES), lambda b, k: (k, 0, 0))
    gblk = pl.BlockSpec((1, n2, C), lambda b, k: (0, k, 0))
    hblk = pl.BlockSpec((1, n2, C), lambda b, k: (1, k, 0))
    shp = jax.ShapeDtypeStruct((B, n1 * n2, C), F32)
    zr, zi = pl.pallas_call(
        functools.partial(_fft2_conv_kernel, n2=n2, n=n1 * n2), grid=(B, n1),
        in_specs=[blk, blk, tblk, tblk, pl.BlockSpec((2 * n2, n2), lambda b, k: (0, 0)),
                  gblk, gblk, hblk, hblk, pl.BlockSpec((1, C), lambda b, k: (0, 0))],
        out_specs=[blk, blk], out_shape=[shp, shp],
        compiler_params=_cparams(("parallel", "parallel")), name="fft_stage2_conv")(
            yr, yi, tr, ti, f2, spec_r, spec_i, spec_r, spec_i, l1)
    return zr.reshape(B, n1, n2 * C), zi.reshape(B, n1, n2 * C)


def _fft3_kernel(zr_ref, zi_ref, fr_ref, fi_ref, x0_ref, u_ref, bias_ref, o_ref):
    conv = jnp.dot(fr_ref[...], zr_ref[0], precision=HIGHEST, preferred_element_type=F32) \
        + jnp.dot(fi_ref[...], zi_ref[0], precision=HIGHEST, preferred_element_type=F32)
    u = u_ref[0]
    o_ref[0] = (x0_ref[0] * (conv + u * bias_ref[...])).astype(o_ref.dtype)


def _fft_stage3(zr, zi, x0, u, bias, n1, n2):
    B, L, C = u.shape
    fr, fi = _dft(n1)
    frh = jnp.asarray(fr[:n1 // 2].astype(np.float32))
    fih = jnp.asarray(fi[:n1 // 2].astype(np.float32))
    G = min(FFT_G, n2)
    fblk = pl.BlockSpec((n1 // 2, n1), lambda b, j: (0, 0))
    zblk = pl.BlockSpec((1, n1, G * C), lambda b, j: (b, 0, j))
    ublk = pl.BlockSpec((1, n1 // 2, G * C), lambda b, j: (b, 0, j))
    out = pl.pallas_call(
        _fft3_kernel, grid=(B, n2 // G),
        in_specs=[zblk, zblk, fblk, fblk, ublk, ublk,
                  pl.BlockSpec((1, G * C), lambda b, j: (0, 0))],
        out_specs=ublk, out_shape=jax.ShapeDtypeStruct((B, n1 // 2, n2 * C), BF16),
        compiler_params=_cparams(("parallel", "parallel")), name="fft_stage3")(
            zr, zi, frh, fih, x0.reshape(B, n1 // 2, n2 * C), u.reshape(B, n1 // 2, n2 * C),
            jnp.tile(bias, (1, G)))
    return out.reshape(B * L, C)


def _hyena(proj, conv_w, conv_b, filt, hy_bias, row0, B, L):
    n1, n2 = _fft_factors(2 * L)
    x0, u = _short_conv(proj, conv_w, conv_b, row0, B, L)
    x0 = x0.reshape(B, L, HY_WIDTH)
    u = u.reshape(B, L, HY_WIDTH)
    tables = _fft_tables(n1, n2)
    hfb, l1 = _hyena_filter(L, *filt)
    fr, fi = _fft_stage1(hfb, n1, n2)
    sr, si = _fft_stage2_filter(fr, fi, tables, n1, n2)
    yr, yi = _fft_stage1(u, n1, n2)
    zr, zi = _fft_stage2_conv(yr, yi, tables, sr, si, l1, n1, n2)
    return _fft_stage3(zr, zi, x0, u, hy_bias, n1, n2)


def _merge_kernel(x_ref, g_ref, wg_ref, ya_ref, yb_ref, yc_ref, wa_ref, wb_ref, wc_ref, wo_ref, o_ref):
    x = x_ref[...]
    xn = _rms(x, g_ref[...]).astype(BF16)
    merged = None
    for j, (y_ref, w_ref) in enumerate(((ya_ref, wa_ref), (yb_ref, wb_ref), (yc_ref, wc_ref))):
        gate = jnp.dot(xn, wg_ref[:, j * D_MODEL:(j + 1) * D_MODEL], preferred_element_type=F32)
        br = jnp.dot(y_ref[...], w_ref[...], preferred_element_type=F32)
        term = jax.nn.sigmoid(gate) * br
        merged = term if merged is None else merged + term
    o_ref[...] = x + jnp.dot(merged.astype(BF16), wo_ref[...], preferred_element_type=F32)


def _merge(x, g, wg, ya, yb, yc, wa, wb, wc, wo):
    T = x.shape[0]
    tm = min(512, T)
    xspec = pl.BlockSpec((tm, D_MODEL), lambda i: (i, 0))
    yspec = pl.BlockSpec((tm, BRANCH_W), lambda i: (i, 0))
    const = lambda shape: pl.BlockSpec(shape, lambda i: (0,) * len(shape))
    return pl.pallas_call(
        _merge_kernel, grid=(T // tm,),
        in_specs=[xspec, const((1, D_MODEL)), const((D_MODEL, 3 * D_MODEL)), yspec, yspec, yspec,
                  const((BRANCH_W, D_MODEL)), const((BRANCH_W, D_MODEL)), const((BRANCH_W, D_MODEL)),
                  const((D_MODEL, D_MODEL))],
        out_specs=xspec, out_shape=jax.ShapeDtypeStruct((T, D_MODEL), F32),
        compiler_params=_cparams(("parallel",)), name="merge_out_projection")(
            x, g, wg, ya, yb, yc, wa, wb, wc, wo)


def _router_kernel(x_ref, g_ref, w_ref, b_ref, xn_ref, r_ref):
    xn = _rms(x_ref[...], g_ref[...])
    xn_ref[...] = xn
    lg = jnp.dot(xn, w_ref[...], precision=HIGHEST, preferred_element_type=F32) + b_ref[...]
    lane = lax.broadcasted_iota(jnp.int32, lg.shape, 1).astype(F32)
    neg = -jnp.inf
    is_g = lane < N_GROUPS
    gl = jnp.where(is_g, lg, neg)
    gmax = jnp.max(gl, axis=1, keepdims=True)
    gsel = jnp.min(jnp.where(gl == gmax, lane, float(LANES)), axis=1, keepdims=True)
    gprob = 1.0 / jnp.sum(jnp.where(is_g, jnp.exp(lg - gmax), 0.0), axis=1, keepdims=True)
    lo = N_GROUPS + gsel * EXPERTS_PER_GROUP
    el = jnp.where((lane >= lo) & (lane < lo + EXPERTS_PER_GROUP), lg, neg)
    v1 = jnp.max(el, axis=1, keepdims=True)
    i1 = jnp.min(jnp.where(el == v1, lane, float(LANES)), axis=1, keepdims=True)
    el2 = jnp.where(lane == i1, neg, el)
    v2 = jnp.max(el2, axis=1, keepdims=True)
    i2 = jnp.min(jnp.where(el2 == v2, lane, float(LANES)), axis=1, keepdims=True)
    e = jnp.exp(v2 - v1)
    w1 = gprob / (1.0 + e)
    w2 = w1 * e
    r_ref[...] = jnp.where(lane == 0, i1 - N_GROUPS,
                           jnp.where(lane == 1, i2 - N_GROUPS,
                                     jnp.where(lane == 2, w1, jnp.where(lane == 3, w2, 0.0))))


def _router(x, g, w, b):
    T = x.shape[0]
    tm = min(512, T)
    xspec = pl.BlockSpec((tm, D_MODEL), lambda i: (i, 0))
    const = lambda shape: pl.BlockSpec(shape, lambda i: (0,) * len(shape))
    return pl.pallas_call(
        _router_kernel, grid=(T // tm,),
        in_specs=[xspec, const((1, D_MODEL)), const((D_MODEL, LANES)), const((1, LANES))],
        out_specs=[xspec, pl.BlockSpec((tm, LANES), lambda i: (i, 0))],
        out_shape=[jax.ShapeDtypeStruct((T, D_MODEL), F32), jax.ShapeDtypeStruct((T, LANES), F32)],
        compiler_params=_cparams(("parallel",)), name="router")(x, g, w, b)


def _moe_kernel(be_ref, rows_hbm, xn_hbm, wg_ref, wu_ref, wd_ref, out_hbm, idx, xbuf, obuf, sem):
    i = pl.program_id(0)
    cp = pltpu.make_async_copy(rows_hbm.at[i], idx, sem.at[0])
    cp.start()
    cp.wait()

    def row_copy_in(r):
        a = jnp.maximum(idx[r], 0)
        return pltpu.make_async_copy(xn_hbm.at[pl.ds(a >> 1, 1)], xbuf.at[pl.ds(r, 1)], sem.at[1])

    def row_copy_out(r):
        a = jnp.maximum(idx[r], 0)
        return pltpu.make_async_copy(obuf.at[pl.ds(r, 1)], out_hbm.at[pl.ds(a, 1)], sem.at[2])

    def start_in(r, c):
        row_copy_in(r).start()
        return c

    def wait_in(r, c):
        row_copy_in(r).wait()
        return c

    lax.fori_loop(0, MOE_BLOCK, start_in, 0)
    lax.fori_loop(0, MOE_BLOCK, wait_in, 0)

    xb = xbuf[...].astype(BF16)
    h = jax.nn.silu(jnp.dot(xb, wg_ref[0], preferred_element_type=F32)) \
        * jnp.dot(xb, wu_ref[0], preferred_element_type=F32)
    obuf[...] = jnp.dot(h.astype(BF16), wd_ref[0], preferred_element_type=F32)

    def start_out(r, c):
        @pl.when(idx[r] >= 0)
        def _():
            row_copy_out(r).start()
        return c

    def wait_out(r, c):
        @pl.when(idx[r] >= 0)
        def _():
            row_copy_out(r).wait()
        return c

    lax.fori_loop(0, MOE_BLOCK, start_out, 0)
    lax.fori_loop(0, MOE_BLOCK, wait_out, 0)


def _moe_experts(xn, rows, block_e, wg, wu, wd):
    T = xn.shape[0]
    n_blocks = rows.shape[0]
    grid_spec = pltpu.PrefetchScalarGridSpec(
        num_scalar_prefetch=1, grid=(n_blocks,),
        in_specs=[pl.BlockSpec(memory_space=pl.ANY), pl.BlockSpec(memory_space=pl.ANY),
                  pl.BlockSpec((1, D_MODEL, EXPERT_HIDDEN), lambda i, be: (be[i], 0, 0)),
                  pl.BlockSpec((1, D_MODEL, EXPERT_HIDDEN), lambda i, be: (be[i], 0, 0)),
                  pl.BlockSpec((1, EXPERT_HIDDEN, D_MODEL), lambda i, be: (be[i], 0, 0))],
        out_specs=pl.BlockSpec(memory_space=pl.ANY),
        scratch_shapes=[pltpu.SMEM((MOE_BLOCK,), jnp.int32), pltpu.VMEM((MOE_BLOCK, D_MODEL), F32),
                        pltpu.VMEM((MOE_BLOCK, D_MODEL), F32), pltpu.SemaphoreType.DMA((3,))])
    return pl.pallas_call(
        _moe_kernel, grid_spec=grid_spec, out_shape=jax.ShapeDtypeStruct((TOP_K * T, D_MODEL), F32),
        compiler_params=_cparams(("arbitrary",)), name="moe_experts")(block_e, rows, xn, wg, wu, wd)


def _dispatch(route, T):
    M = T * TOP_K
    expert_id = route[:, :TOP_K].astype(jnp.int32).reshape(-1)
    order = jnp.argsort(expert_id).astype(jnp.int32)
    sorted_e = expert_id[order]
    counts = jnp.bincount(expert_id, length=N_EXPERTS).astype(jnp.int32)
    starts = jnp.cumsum(counts) - counts
    padded = (counts + MOE_BLOCK - 1) // MOE_BLOCK * MOE_BLOCK
    p_ends = jnp.cumsum(padded)
    p_starts = p_ends - padded
    dest = p_starts[sorted_e] + jnp.arange(M, dtype=jnp.int32) - starts[sorted_e]
    n_blocks = M // MOE_BLOCK + N_EXPERTS
    rows = jnp.full((n_blocks * MOE_BLOCK,), -1, jnp.int32).at[dest].set(order)
    block_e = jnp.minimum(jnp.searchsorted(p_ends, jnp.arange(n_blocks, dtype=jnp.int32) * MOE_BLOCK,
                                           side='right'), N_EXPERTS - 1).astype(jnp.int32)
    return rows.reshape(n_blocks, MOE_BLOCK), block_e


def _final_kernel(x_ref, e_ref, r_ref, g_ref, o_ref):
    r = r_ref[...]
    x = x_ref[...] + e_ref[:, :D_MODEL] * r[:, 2:3] + e_ref[:, D_MODEL:] * r[:, 3:4]
    o_ref[...] = _rms(x, g_ref[...])


def _final(x, e2, route, g):
    T = x.shape[0]
    tm = min(512, T)
    xspec = pl.BlockSpec((tm, D_MODEL), lambda i: (i, 0))
    return pl.pallas_call(
        _final_kernel, grid=(T // tm,),
        in_specs=[xspec, pl.BlockSpec((tm, 2 * D_MODEL), lambda i: (i, 0)),
                  pl.BlockSpec((tm, LANES), lambda i: (i, 0)), pl.BlockSpec((1, D_MODEL), lambda i: (0, 0))],
        out_specs=xspec, out_shape=jax.ShapeDtypeStruct((T, D_MODEL), F32),
        compiler_params=_cparams(("parallel",)), name="combine_final_norm")(x, e2, route, g)


def _pack_layer(l, p):
    w = p['w_in'][l]
    o_gate = 3 * HG_WIDTH + 2 * HG_WIDTH + 4 * ML_WIDTH
    o_hy = o_gate + 4 * ML_HEADS
    o_g = o_hy + 3 * HY_WIDTH
    gates = jnp.pad(w[:, o_gate:o_hy], ((0, 0), (0, LANES - 4 * ML_HEADS)))
    w_proj = jnp.concatenate([w[:, :o_gate], w[:, o_hy:o_g], gates], axis=1).astype(BF16)
    w_gate = w[:, o_g:].astype(BF16)
    gate_bias = jnp.pad(jnp.concatenate([p['ml_i_bias'][l].reshape(-1), p['ml_f_bias'][l].reshape(-1)]),
                        (0, LANES - 4 * ML_HEADS)).astype(F32)[None]
    w_router = jnp.pad(jnp.concatenate([p['router_group_w'][l], p['router_expert_w'][l]], axis=1),
                       ((0, 0), (0, LANES - N_GROUPS - N_EXPERTS))).astype(F32)
    b_router = jnp.pad(jnp.concatenate([p['router_group_b'][l], p['router_expert_b'][l]]),
                       (0, LANES - N_GROUPS - N_EXPERTS)).astype(F32)[None]
    return dict(
        norm_mix_g=p['norm_mix_g'][l].astype(F32)[None], w_proj=w_proj, w_gate=w_gate,
        hg_norm_g=p['hg_norm_g'][l].astype(F32)[None], gate_bias=gate_bias,
        ml_norm_g=p['ml_norm_g'][l].astype(F32)[None],
        conv_w=jnp.pad(p['hy_conv_w'][l].astype(F32), ((0, 5), (0, 0))), conv_b=p['hy_conv_b'][l].astype(F32)[None],
        filt=(p['filt_w1'][l], p['filt_b1'][l], p['filt_w2'][l], p['filt_b2'][l], p['filt_freq'][l],
              p['filt_w3'][l]),
        hy_bias=p['hy_bias'][l].astype(F32)[None],
        wa=p['w_branch_a'][l].astype(BF16), wb=p['w_branch_b'][l].astype(BF16),
        wc=p['w_branch_c'][l].astype(BF16), wo=p['w_out'][l].astype(BF16),
        norm_ffn_g=p['norm_ffn_g'][l].astype(F32)[None], w_router=w_router, b_router=b_router,
        wg=p['exp_w_gate'][l].astype(BF16), wu=p['exp_w_up'][l].astype(BF16),
        wd=p['exp_w_down'][l].astype(BF16))


def _trunk(x, groups, p):
    T = x.shape[0]
    lbs = jnp.cumsum(jax.nn.softmax(p['hg_lb'].astype(F32), axis=0), axis=0)
    lbs = lbs - lbs[0:1]
    moe = None
    for l in range(DEPTH):
        lp = _pack_layer(l, p)
        lb = lbs[l][None]
        lbrow = jnp.concatenate([jnp.log(lb), jnp.log1p(-lb), 1.0 - lb, jnp.zeros((5, HG_WIDTH), F32)], axis=0)
        if moe is None:
            proj = _in_projection(x, lp['norm_mix_g'], lp['w_proj'])
        else:
            proj, x = _in_projection(x, lp['norm_mix_g'], lp['w_proj'], moe)
        ya, yb, yc = [], [], []
        for row0, B, L in groups:
            ya.append(_hgrn2(proj, lbrow, lp['hg_norm_g'], row0, B, L))
            yb.append(_mlstm(proj, lp['gate_bias'], lp['ml_norm_g'], row0, B, L))
            yc.append(_hyena(proj, lp['conv_w'], lp['conv_b'], lp['filt'], lp['hy_bias'], row0, B, L))
        ya, yb, yc = (jnp.concatenate(v, axis=0) for v in (ya, yb, yc))
        x = _merge(x, lp['norm_mix_g'], lp['w_gate'], ya, yb, yc, lp['wa'], lp['wb'], lp['wc'], lp['wo'])
        xn, route = _router(x, lp['norm_ffn_g'], lp['w_router'], lp['b_router'])
        rows, block_e = _dispatch(route, T)
        e2 = _moe_experts(xn, rows, block_e, lp['wg'], lp['wu'], lp['wd'])
        moe = (e2.reshape(T, TOP_K * D_MODEL), route)
    return _final(x, moe[0], moe[1], p['final_norm_g'].astype(F32)[None])


def kernel(x_prompt, x_sample, norm_mix_g, w_in, hg_lb, hg_norm_g, ml_i_bias, ml_f_bias, ml_norm_g, hy_conv_w, hy_conv_b, filt_w1, filt_b1, filt_w2, filt_b2, filt_freq, filt_w3, hy_bias, w_branch_a, w_branch_b, w_branch_c, w_out, norm_ffn_g, router_group_w, router_group_b, router_expert_w, router_expert_b, exp_w_gate, exp_w_up, exp_w_down, final_norm_g):
    p = dict(norm_mix_g=norm_mix_g, w_in=w_in, hg_lb=hg_lb, hg_norm_g=hg_norm_g, ml_i_bias=ml_i_bias,
             ml_f_bias=ml_f_bias, ml_norm_g=ml_norm_g, hy_conv_w=hy_conv_w, hy_conv_b=hy_conv_b,
             filt_w1=filt_w1, filt_b1=filt_b1, filt_w2=filt_w2, filt_b2=filt_b2, filt_freq=filt_freq,
             filt_w3=filt_w3, hy_bias=hy_bias, w_branch_a=w_branch_a, w_branch_b=w_branch_b,
             w_branch_c=w_branch_c, w_out=w_out, norm_ffn_g=norm_ffn_g, router_group_w=router_group_w,
             router_group_b=router_group_b, router_expert_w=router_expert_w,
             router_expert_b=router_expert_b, exp_w_gate=exp_w_gate, exp_w_up=exp_w_up,
             exp_w_down=exp_w_down, final_norm_g=final_norm_g)
    Bp, Lp, _ = x_prompt.shape
    Bs, Ls, _ = x_sample.shape
    Tp, Ts = Bp * Lp, Bs * Ls
    x = jnp.concatenate([x_prompt.reshape(Tp, D_MODEL), x_sample.reshape(Ts, D_MODEL)], axis=0).astype(F32)
    y = _trunk(x, ((0, Bp, Lp), (Tp, Bs, Ls)), p)
    return (y[:Tp].reshape(Bp, Lp, D_MODEL), y[Tp:].reshape(Bs, Ls, D_MODEL))
```

```python
import functools
import math

import numpy as np
import jax
import jax.numpy as jnp
from jax import lax
from jax.experimental import pallas as pl
from jax.experimental.pallas import tpu as pltpu

F32 = jnp.float32
BF16 = jnp.bfloat16
HIGHEST = lax.Precision.HIGHEST

D_MODEL = 1024
EPS = 1e-6
DEPTH = 2
HG_HEADS, HG_KDIM, HG_WIDTH, HG_CHUNK = 8, 64, 512, 32
ML_HEADS, ML_HDIM, ML_WIDTH, ML_CHUNK = 4, 128, 512, 128
HY_WIDTH, HY_BANDS, HY_FILTER_HIDDEN = 512, 16, 64
HY_EMB = 1 + 2 * HY_BANDS
HY_SHORT_DECAY_PCT, HY_LONG_DECAY_PCT, HY_DECAY_TARGET = 0.3, 1.5, 1e-2
N_GROUPS, EXPERTS_PER_GROUP, TOP_K = 4, 8, 2
N_EXPERTS = N_GROUPS * EXPERTS_PER_GROUP
EXPERT_HIDDEN = D_MODEL // 2
MOE_BLOCK = 128

LANES = 128
BRANCH_W = 512
SLAB_HQ, SLAB_HFF, SLAB_HFB, SLAB_HI, SLAB_HG = 0, 1, 2, 3, 4
SLAB_MQ, SLAB_MK, SLAB_MV, SLAB_MO = 5, 6, 7, 8
SLAB_HY = 9
PROJ_MAIN = 12 * BRANCH_W
PROJ_W = PROJ_MAIN + LANES
GATE_BLOCK = PROJ_MAIN // LANES
VMEM_LIMIT = 48 * 1024 * 1024


def _cparams(sem):
    return pltpu.CompilerParams(dimension_semantics=sem, vmem_limit_bytes=VMEM_LIMIT)


def _rms(x, g):
    return x * lax.rsqrt(jnp.mean(x * x, axis=-1, keepdims=True) + EPS) * g


def _log_sigmoid(z):
    return jnp.minimum(z, 0.0) - jnp.log1p(jnp.exp(-jnp.abs(z)))


def _inproj_kernel(x_ref, g_ref, w_ref, o_ref, xn_ref):
    @pl.when(pl.program_id(1) == 0)
    def _():
        xn_ref[...] = _rms(x_ref[...], g_ref[...]).astype(BF16)

    o_ref[...] = jnp.dot(xn_ref[...], w_ref[...], preferred_element_type=F32)


def _inproj_combine_kernel(x_ref, e0_ref, e1_ref, r_ref, g_ref, w_ref, o_ref, xo_ref, xn_ref):
    @pl.when(pl.program_id(1) == 0)
    def _():
        r = r_ref[...]
        x = x_ref[...] + e0_ref[...] * r[:, 2:3] + e1_ref[...] * r[:, 3:4]
        xo_ref[...] = x
        xn_ref[...] = _rms(x, g_ref[...]).astype(BF16)

    o_ref[...] = jnp.dot(xn_ref[...], w_ref[...], preferred_element_type=F32)


def _in_projection(x, g, w, moe=None):
    T = x.shape[0]
    tm = min(1024 if moe is None else 512, T)
    tn = PROJ_W // 7
    grid = (T // tm, PROJ_W // tn)
    xspec = pl.BlockSpec((tm, D_MODEL), lambda i, j: (i, 0))
    gspec = pl.BlockSpec((1, D_MODEL), lambda i, j: (0, 0))
    wspec = pl.BlockSpec((D_MODEL, tn), lambda i, j: (0, j))
    ospec = pl.BlockSpec((tm, tn), lambda i, j: (i, j))
    scratch = [pltpu.VMEM((tm, D_MODEL), BF16)]
    cp = _cparams(("parallel", "arbitrary"))
    if moe is None:
        return pl.pallas_call(
            _inproj_kernel, grid=grid, in_specs=[xspec, gspec, wspec], out_specs=ospec,
            out_shape=jax.ShapeDtypeStruct((T, PROJ_W), F32), scratch_shapes=scratch,
            compiler_params=cp, name="in_projection")(x, g, w)
    e2, route = moe
    nT = T // tm
    e0spec = pl.BlockSpec((tm, D_MODEL), lambda i, j: (i, 0))
    e1spec = pl.BlockSpec((tm, D_MODEL), lambda i, j: (nT + i, 0))
    rspec = pl.BlockSpec((tm, LANES), lambda i, j: (i, 0))
    proj, xo = pl.pallas_call(
        _inproj_combine_kernel, grid=grid, in_specs=[xspec, e0spec, e1spec, rspec, gspec, wspec],
        out_specs=[ospec, xspec],
        out_shape=[jax.ShapeDtypeStruct((T, PROJ_W), F32), jax.ShapeDtypeStruct((T, D_MODEL), F32)],
        scratch_shapes=scratch, compiler_params=cp, name="in_projection_combine")(x, e2, e2, route, g, w)
    return proj, xo


HG_TB = 256
HG_PAIRS = HG_WIDTH // LANES


HG_SUB = 8
HG_NSUB = HG_CHUNK // HG_SUB
HG_XROWS = (HG_NSUB - 1) * HG_HEADS * HG_SUB
HG_XCOLS = HG_SUB * HG_NSUB * (HG_NSUB - 1) // 2


def _hgrn2_tiles(reverse):
    if reverse:
        return [(i, (i + 1) * HG_SUB, HG_CHUNK - (i + 1) * HG_SUB) for i in range(HG_NSUB - 1)]
    return [(i, 0, i * HG_SUB) for i in range(1, HG_NSUB)]


def _hgrn2_kernel(*refs, reverse, finalize):
    if finalize:
        (q_ref, z_ref, v_ref, lb_ref, tri_ref, bd_ref, xm_ref, of_ref, gate_ref, ng_ref,
         o_ref, st_ref, abuf, scbuf) = refs
    else:
        (q_ref, z_ref, v_ref, lb_ref, tri_ref, bd_ref, xm_ref, o_ref, st_ref, abuf, scbuf) = refs
    C, c, nb, W = HG_CHUNK, HG_SUB, HG_NSUB, HG_WIDTH
    n_chunks = HG_TB // C

    @pl.when(pl.program_id(1) == 0)
    def _():
        st_ref[...] = jnp.zeros_like(st_ref)

    log_lb = lb_ref[0:1, :]
    log_1mlb = lb_ref[1:2, :]
    one_m_lb = lb_ref[2:3, :]
    tri3 = tri_ref[...]
    bd = bd_ref[...]
    bd_bf = bd.astype(BF16)
    xmask = xm_ref[...]
    sub_row = lax.broadcasted_iota(jnp.int32, (nb, c, W), 1)
    lane_head = lax.broadcasted_iota(jnp.int32, (c, W), 1) // HG_KDIM
    tiles = _hgrn2_tiles(reverse)

    def chunk(i, carry):
        ci = (n_chunks - 1 - i) if reverse else i
        off = pl.multiple_of(ci * C, C)
        q = q_ref[pl.ds(off, C), :]
        z = z_ref[pl.ds(off, C), :]
        v = v_ref[pl.ds(off, C), :]
        ls = jnp.minimum(z, 0.0) - jnp.log(1.0 + jnp.exp(-jnp.abs(z)))
        hi = log_1mlb + ls
        mx = jnp.maximum(log_lb, hi)
        lf = mx + jnp.log(1.0 + jnp.exp(-jnp.abs(log_lb - hi)))
        kk = one_m_lb * jnp.exp(ls - z)
        l1 = lf.astype(BF16)
        r1 = lf - l1.astype(F32)
        l2 = r1.astype(BF16)
        l3 = (r1 - l2.astype(F32)).astype(BF16)
        b = jnp.dot(tri3, jnp.concatenate([l1, l2, l3], axis=0), preferred_element_type=F32)
        btot = b[0:1, :] if reverse else b[C - 1:C, :]
        qb = (q * jnp.exp(b)).astype(BF16)
        kb = (kk * jnp.exp(btot - b)).astype(BF16)
        dec = jnp.exp(btot)
        vb = v.astype(BF16)

        q3, k3, v3, b3 = (a.reshape(nb, c, W) for a in (q, kk, v, b))
        abuf[0:C, :] = (q * kk).astype(BF16)
        for d in range(1, c):
            sh = c - d if reverse else d
            ok = (sub_row < c - d) if reverse else (sub_row >= d)
            a = jnp.where(ok, q3 * pltpu.roll(k3, sh, axis=1) * jnp.exp(b3 - pltpu.roll(b3, sh, axis=1)), 0.0)
            abuf[d * C:(d + 1) * C, :] = a.reshape(C, W).astype(BF16)
        for p in range(HG_PAIRS):
            sl = slice(p * LANES, (p + 1) * LANES)
            scbuf[:, sl] = jnp.dot(abuf[:, sl], bd_bf, preferred_element_type=F32)
        acc3 = scbuf[0:C, :].reshape(nb, c, W) * v3
        for d in range(1, c):
            sh = c - d if reverse else d
            acc3 = acc3 + scbuf[d * C:(d + 1) * C, :].reshape(nb, c, W) * pltpu.roll(v3, sh, axis=1)
        acc = acc3.reshape(C, W)

        qx, kx, vx = [], [], []
        for (ti, s0, sn) in tiles:
            edge = s0 if reverse else s0 + sn - 1
            r = b[edge:edge + 1, :]
            rows = slice(ti * c, (ti + 1) * c)
            qh = q[rows] * jnp.exp(b[rows] - r)
            qx += [jnp.where(lane_head == h, qh, 0.0) for h in range(HG_HEADS)]
            kx.append(kk[s0:s0 + sn] * jnp.exp(r - b[s0:s0 + sn]))
            vx.append(v[s0:s0 + sn])
        qx = jnp.concatenate(qx, axis=0).astype(BF16)
        kx = jnp.concatenate(kx, axis=0).astype(BF16)
        vx = jnp.concatenate(vx, axis=0).astype(BF16)
        sc = lax.dot_general(qx, kx, (((1,), (1,)), ((), ())), preferred_element_type=F32) * xmask
        px = jnp.dot(sc.astype(BF16), vx, preferred_element_type=F32)
        offd = {}
        for n, (ti, s0, sn) in enumerate(tiles):
            base = n * HG_HEADS * c
            t_acc = jnp.where(lane_head == 0, px[base:base + c], 0.0)
            for h in range(1, HG_HEADS):
                t_acc = t_acc + jnp.where(lane_head == h, px[base + h * c:base + (h + 1) * c], 0.0)
            offd[ti] = t_acc
        acc = acc + jnp.concatenate([offd.get(ti, jnp.zeros((c, W), F32)) for ti in range(nb)], axis=0)

        parts = []
        for p in range(HG_PAIRS):
            sl = slice(p * LANES, (p + 1) * LANES)
            s_t = st_ref[p]
            parts.append(lax.dot_general(qb[:, sl], s_t.astype(BF16), (((1,), (1,)), ((), ())),
                                         preferred_element_type=F32))
            upd = lax.dot_general(vb[:, sl], kb[:, sl], (((0,), (0,)), ((), ())),
                                  preferred_element_type=F32)
            st_ref[p] = s_t * dec[:, sl] + upd * bd
        acc = acc + jnp.concatenate(parts, axis=1)

        if finalize:
            o = acc + of_ref[pl.ds(off, C), :]
            o2 = (o * o).astype(BF16)
            ms = jnp.concatenate(
                [jnp.dot(o2[:, p * LANES:(p + 1) * LANES], bd_bf, preferred_element_type=F32)
                 for p in range(HG_PAIRS)], axis=1) * (1.0 / HG_KDIM)
            g = gate_ref[pl.ds(off, C), :]
            y = o * lax.rsqrt(ms + EPS) * ng_ref[...] * (g * jax.nn.sigmoid(g))
            o_ref[pl.ds(off, C), :] = y.astype(o_ref.dtype)
        else:
            o_ref[pl.ds(off, C), :] = acc
        return carry

    lax.fori_loop(0, n_chunks, chunk, 0)


def _hgrn2(proj, lbrow, norm_g, row0, B, L):
    C = HG_CHUNK
    nT = L // HG_TB
    blk0 = row0 // HG_TB
    tri_f = jnp.asarray(np.tile(np.tril(np.ones((C, C), np.float32)), (1, 3))).astype(BF16)
    tri_b = jnp.asarray(np.tile(np.triu(np.ones((C, C), np.float32)), (1, 3))).astype(BF16)
    head = np.arange(LANES) // HG_KDIM
    bd = jnp.asarray((head[:, None] == head[None, :]).astype(np.float32))

    def tile_mask(reverse):
        m = np.zeros((HG_XROWS, HG_XCOLS), np.float32)
        col = 0
        for n, (_, _, sn) in enumerate(_hgrn2_tiles(reverse)):
            m[n * HG_HEADS * HG_SUB:(n + 1) * HG_HEADS * HG_SUB, col:col + sn] = 1.0
            col += sn
        return jnp.asarray(m)

    def in_spec(slab, reverse):
        if reverse:
            return pl.BlockSpec((HG_TB, BRANCH_W), lambda b, t: (blk0 + b * nT + nT - 1 - t, slab))
        return pl.BlockSpec((HG_TB, BRANCH_W), lambda b, t: (blk0 + b * nT + t, slab))

    def out_spec(reverse):
        if reverse:
            return pl.BlockSpec((HG_TB, BRANCH_W), lambda b, t: (b * nT + nT - 1 - t, 0))
        return pl.BlockSpec((HG_TB, BRANCH_W), lambda b, t: (b * nT + t, 0))

    const = lambda shape: pl.BlockSpec(shape, lambda b, t: (0,) * len(shape))
    scratch = [pltpu.VMEM((HG_PAIRS, LANES, LANES), F32),
               pltpu.VMEM((HG_SUB * C, BRANCH_W), BF16), pltpu.VMEM((HG_SUB * C, BRANCH_W), F32)]
    consts = [const((8, BRANCH_W)), const((C, 3 * C)), const((LANES, LANES)), const((HG_XROWS, HG_XCOLS))]
    cp = _cparams(("parallel", "arbitrary"))
    o_f = pl.pallas_call(
        functools.partial(_hgrn2_kernel, reverse=False, finalize=False), grid=(B, nT),
        in_specs=[in_spec(SLAB_HQ, False), in_spec(SLAB_HFF, False), in_spec(SLAB_HI, False)] + consts,
        out_specs=out_spec(False), out_shape=jax.ShapeDtypeStruct((B * L, BRANCH_W), F32),
        scratch_shapes=scratch, compiler_params=cp, name="hgrn2_fwd")(
            proj, proj, proj, lbrow, tri_f, bd, tile_mask(False))
    return pl.pallas_call(
        functools.partial(_hgrn2_kernel, reverse=True, finalize=True), grid=(B, nT),
        in_specs=[in_spec(SLAB_HQ, True), in_spec(SLAB_HFB, True), in_spec(SLAB_HI, True)] + consts
        + [out_spec(True), in_spec(SLAB_HG, True), const((1, BRANCH_W))],
        out_specs=out_spec(True), out_shape=jax.ShapeDtypeStruct((B * L, BRANCH_W), BF16),
        scratch_shapes=scratch, compiler_params=cp, name="hgrn2_bwd")(
            proj, proj, proj, lbrow, tri_b, bd, tile_mask(True), o_f, proj, norm_g)


def _mlstm_kernel(*refs, reverse, finalize):
    if finalize:
        (q_ref, k_ref, v_ref, gt_ref, gb_ref, tri_ref, hf_ref, og_ref, ng_ref,
         o_ref, c_ref, n_ref, m_ref) = refs
    else:
        (q_ref, k_ref, v_ref, gt_ref, gb_ref, tri_ref, o_ref, c_ref, n_ref, m_ref) = refs
    C = ML_CHUNK

    @pl.when(pl.program_id(1) == 0)
    def _():
        c_ref[...] = jnp.zeros_like(c_ref)
        n_ref[...] = jnp.zeros_like(n_ref)
        m_ref[...] = jnp.zeros_like(m_ref)

    tri = tri_ref[...]
    gates = gt_ref[...] + gb_ref[...]
    lane = lax.broadcasted_iota(jnp.int32, (C, LANES), 1)
    g_col = jnp.where(lane >= 2 * ML_HEADS, _log_sigmoid(gates), gates)
    g_row = g_col.T
    b_col = jnp.dot(tri, g_col, precision=HIGHEST, preferred_element_type=F32)
    b_row = lax.dot_general(g_row, tri, (((1,), (1,)), ((), ())), precision=HIGHEST,
                            preferred_element_type=F32)
    ti = lax.broadcasted_iota(jnp.int32, (C, C), 0)
    si = lax.broadcasted_iota(jnp.int32, (C, C), 1)
    causal = (si >= ti) if reverse else (si <= ti)
    dsel = ML_HEADS if reverse else 0
    edge = 0 if reverse else C - 1
    scale = 1.0 / math.sqrt(ML_HDIM)

    outs = []
    for h in range(ML_HEADS):
        sl = slice(h * ML_HDIM, (h + 1) * ML_HDIM)
        li, lf = dsel + h, 2 * ML_HEADS + dsel + h
        q = q_ref[:, sl]
        kc = k_ref[:, sl] * scale
        v = v_ref[:, sl]
        qb, kb, vb = q.astype(BF16), kc.astype(BF16), v.astype(BF16)
        bc = b_col[:, lf:lf + 1]
        br = b_row[lf:lf + 1, :]
        ic = g_col[:, li:li + 1]
        ir = g_row[li:li + 1, :]
        btot = b_col[edge:edge + 1, lf:lf + 1]
        m_prev = m_ref[0:1, h:h + 1]
        n_prev = n_ref[h:h + 1, :]
        c_prev = c_ref[h]

        log_d = jnp.where(causal, bc - br + ir, -jnp.inf)
        log_inter = bc + m_prev
        m_t = jnp.maximum(log_inter, jnp.max(log_d, axis=1, keepdims=True))
        a_inter = jnp.exp(log_inter - m_t)
        s = lax.dot_general(qb, kb, (((1,), (1,)), ((), ())), preferred_element_type=F32)
        s = s * jnp.exp(log_d - m_t)
        num = a_inter * jnp.dot(qb, c_prev.astype(BF16), preferred_element_type=F32) \
            + jnp.dot(s.astype(BF16), vb, preferred_element_type=F32)
        den = a_inter * jnp.sum(q * n_prev, axis=1, keepdims=True) + jnp.sum(s, axis=1, keepdims=True)
        outs.append(num / jnp.maximum(jnp.abs(den), jnp.exp(-m_t)))

        log_w = btot - bc + ic
        m_new = jnp.maximum(btot + m_prev, jnp.max(log_w, axis=0, keepdims=True))
        a_state = jnp.exp(btot + m_prev - m_new)
        kw = kc * jnp.exp(log_w - m_new)
        c_ref[h] = a_state * c_prev + lax.dot_general(kw.astype(BF16), vb, (((0,), (0,)), ((), ())),
                                                      preferred_element_type=F32)
        n_ref[h:h + 1, :] = a_state * n_prev + jnp.sum(kw, axis=0, keepdims=True)
        m_ref[0:1, h:h + 1] = m_new

    hcur = jnp.concatenate(outs, axis=1)
    if finalize:
        hsum = hcur + hf_ref[...]
        ys = []
        for h in range(ML_HEADS):
            sl = slice(h * ML_HDIM, (h + 1) * ML_HDIM)
            hh = hsum[:, sl]
            ys.append(hh * lax.rsqrt(jnp.mean(hh * hh, axis=1, keepdims=True) + EPS))
        y = jnp.concatenate(ys, axis=1) * ng_ref[...] * jax.nn.sigmoid(og_ref[...])
        o_ref[...] = y.astype(o_ref.dtype)
    else:
        o_ref[...] = hcur


def _mlstm(proj, gate_bias, norm_g, row0, B, L):
    C = ML_CHUNK
    nT = L // C
    blk0 = row0 // C
    tri_f = jnp.asarray(np.tril(np.ones((C, C), np.float32)))
    tri_b = jnp.asarray(np.triu(np.ones((C, C), np.float32)))

    def rows(reverse):
        if reverse:
            return lambda b, t: blk0 + b * nT + nT - 1 - t
        return lambda b, t: blk0 + b * nT + t

    def in_spec(slab, reverse):
        r = rows(reverse)
        return pl.BlockSpec((C, BRANCH_W), lambda b, t: (r(b, t), slab))

    def gate_spec(reverse):
        r = rows(reverse)
        return pl.BlockSpec((C, LANES), lambda b, t: (r(b, t), GATE_BLOCK))

    def out_spec(reverse):
        if reverse:
            return pl.BlockSpec((C, BRANCH_W), lambda b, t: (b * nT + nT - 1 - t, 0))
        return pl.BlockSpec((C, BRANCH_W), lambda b, t: (b * nT + t, 0))

    const = lambda shape: pl.BlockSpec(shape, lambda b, t: (0,) * len(shape))
    scratch = [pltpu.VMEM((ML_HEADS, ML_HDIM, ML_HDIM), F32), pltpu.VMEM((8, ML_HDIM), F32),
               pltpu.VMEM((8, LANES), F32)]
    cp = _cparams(("parallel", "arbitrary"))
    h_f = pl.pallas_call(
        functools.partial(_mlstm_kernel, reverse=False, finalize=False), grid=(B, nT),
        in_specs=[in_spec(SLAB_MQ, False), in_spec(SLAB_MK, False), in_spec(SLAB_MV, False),
                  gate_spec(False), const((1, LANES)), const((C, C))],
        out_specs=out_spec(False), out_shape=jax.ShapeDtypeStruct((B * L, BRANCH_W), F32),
        scratch_shapes=scratch, compiler_params=cp, name="mlstm_fwd")(
            proj, proj, proj, proj, gate_bias, tri_f)
    return pl.pallas_call(
        functools.partial(_mlstm_kernel, reverse=True, finalize=True), grid=(B, nT),
        in_specs=[in_spec(SLAB_MQ, True), in_spec(SLAB_MK, True), in_spec(SLAB_MV, True),
                  gate_spec(True), const((1, LANES)), const((C, C)),
                  out_spec(True), in_spec(SLAB_MO, True), const((1, BRANCH_W))],
        out_specs=out_spec(True), out_shape=jax.ShapeDtypeStruct((B * L, BRANCH_W), BF16),
        scratch_shapes=scratch, compiler_params=cp, name="mlstm_bwd")(
            proj, proj, proj, proj, gate_bias, tri_b, h_f, proj, norm_g)


HY_TB = 256


def _shortconv_kernel(c_ref, p_ref, n_ref, w_ref, b_ref, x0_ref, u_ref, *, nT):
    t = pl.program_id(1)
    cur = c_ref[...]
    prev_row = jnp.where(t > 0, p_ref[7:8, :], 0.0)
    next_row = jnp.where(t < nT - 1, n_ref[0:1, :], 0.0)
    row = lax.broadcasted_iota(jnp.int32, cur.shape, 0)
    up = jnp.where(row == 0, prev_row, pltpu.roll(cur, 1, axis=0))
    dn = jnp.where(row == HY_TB - 1, next_row, pltpu.roll(cur, HY_TB - 1, axis=0))
    y = up * w_ref[0:1, :] + cur * w_ref[1:2, :] + dn * w_ref[2:3, :] + b_ref[...]
    x0_ref[...] = y[:, :HY_WIDTH]
    u_ref[...] = y[:, HY_WIDTH:2 * HY_WIDTH] * y[:, 2 * HY_WIDTH:]


def _short_conv(proj, w, b, row0, B, L):
    nT = L // HY_TB
    blk0 = row0 // HY_TB
    sub = HY_TB // 8
    W3 = 3 * HY_WIDTH
    slab = SLAB_HY * BRANCH_W // W3
    cur = pl.BlockSpec((HY_TB, W3), lambda bb, t: (blk0 + bb * nT + t, slab))
    prv = pl.BlockSpec((8, W3), lambda bb, t: (jnp.maximum((blk0 + bb * nT + t) * sub - 1, 0), slab))
    nxt = pl.BlockSpec((8, W3), lambda bb, t: (jnp.minimum((blk0 + bb * nT + t + 1) * sub,
                                                           (blk0 + B * nT) * sub - 1), slab))
    const = lambda shape: pl.BlockSpec(shape, lambda bb, t: (0,) * len(shape))
    out = pl.BlockSpec((HY_TB, HY_WIDTH), lambda bb, t: (bb * nT + t, 0))
    return pl.pallas_call(
        functools.partial(_shortconv_kernel, nT=nT), grid=(B, nT),
        in_specs=[cur, prv, nxt, const((8, W3)), const((1, W3))], out_specs=[out, out],
        out_shape=[jax.ShapeDtypeStruct((B * L, HY_WIDTH), F32)] * 2,
        compiler_params=_cparams(("parallel", "parallel")), name="hyena_short_conv")(proj, proj, proj, w, b)


HYF_TB = 256


def _filter_kernel(band_ref, w1_ref, b1_ref, w2_ref, b2_ref, fr_ref, w3_ref, dl_ref, h_ref, l1_ref, *, L):
    i = pl.program_id(0)
    pos = (lax.broadcasted_iota(jnp.int32, (HYF_TB, LANES), 0) + i * HYF_TB).astype(F32)
    lane = lax.broadcasted_iota(jnp.int32, (HYF_TB, LANES), 1)
    t = pos / (L - 1)
    ang = (2.0 * math.pi * pos / L) * band_ref[...]
    z = jnp.where(lane == 0, t,
                  jnp.where(lane <= HY_BANDS, jnp.cos(ang),
                            jnp.where(lane <= 2 * HY_BANDS, -jnp.sin(ang), 0.0)))
    fr = fr_ref[...]
    h = jnp.sin(fr * (jnp.dot(z, w1_ref[...], precision=HIGHEST, preferred_element_type=F32) + b1_ref[...]))
    h = jnp.sin(fr * (jnp.dot(h, w2_ref[...], precision=HIGHEST, preferred_element_type=F32) + b2_ref[...]))
    h = jnp.dot(h, w3_ref[...], precision=HIGHEST, preferred_element_type=F32)
    tt = (lax.broadcasted_iota(jnp.int32, (HYF_TB, HY_WIDTH), 0) + i * HYF_TB).astype(F32) / (L - 1)
    window = jnp.exp(-tt * dl_ref[...])
    rowi = lax.broadcasted_iota(jnp.int32, (HYF_TB, HY_WIDTH), 0) + i * HYF_TB
    hf = h[:, :HY_WIDTH] * window
    hb = jnp.where(rowi == 0, 0.0, h[:, HY_WIDTH:] * window)
    h_ref[0] = hf
    h_ref[1] = hb

    @pl.when(i == 0)
    def _():
        l1_ref[...] = jnp.zeros_like(l1_ref)

    l1_ref[...] += jnp.sum(jnp.abs(hf) + jnp.abs(hb), axis=0, keepdims=True)


def _hyena_filter(L, w1, b1, w2, b2, freq, w3):
    band = np.zeros((1, LANES), np.float32)
    bands = np.linspace(1e-4, HY_BANDS - 1, HY_BANDS, dtype=np.float32)
    band[0, 1:1 + HY_BANDS] = bands
    band[0, 1 + HY_BANDS:1 + 2 * HY_BANDS] = bands
    max_decay = math.log(HY_DECAY_TARGET) / HY_SHORT_DECAY_PCT
    min_decay = math.log(HY_DECAY_TARGET) / HY_LONG_DECAY_PCT
    deltas = np.abs(np.linspace(min_decay, max_decay, HY_WIDTH, dtype=np.float32))[None, :]
    w1p = jnp.zeros((LANES, HY_FILTER_HIDDEN), F32).at[:HY_EMB].set(w1.astype(F32))
    const = lambda shape: pl.BlockSpec(shape, lambda i: (0,) * len(shape))
    H = HY_FILTER_HIDDEN
    return pl.pallas_call(
        functools.partial(_filter_kernel, L=L), grid=(L // HYF_TB,),
        in_specs=[const((1, LANES)), const((LANES, H)), const((1, H)), const((H, H)), const((1, H)),
                  const((1, H)), const((H, 2 * HY_WIDTH)), const((1, HY_WIDTH))],
        out_specs=[pl.BlockSpec((2, HYF_TB, HY_WIDTH), lambda i: (0, i, 0)), const((1, HY_WIDTH))],
        out_shape=[jax.ShapeDtypeStruct((2, L, HY_WIDTH), F32), jax.ShapeDtypeStruct((1, HY_WIDTH), F32)],
        compiler_params=_cparams(("arbitrary",)), name="hyena_filter")(
            jnp.asarray(band), w1p, b1.astype(F32)[None], w2.astype(F32), b2.astype(F32)[None],
            freq.astype(F32)[None], w3.astype(F32), jnp.asarray(deltas))


def _fft_factors(n):
    lg = int(round(math.log2(n)))
    n1 = 1 << (lg // 2)
    return n1, n // n1


def _dft(n):
    k = np.arange(n)
    a = -2.0 * np.pi * ((k[:, None] * k[None, :]) % n) / n
    return np.cos(a), np.sin(a)


FFT_G = 8


def _split3(f):
    f = jnp.asarray(np.asarray(f, np.float32))
    hi = f.astype(BF16)
    lo = (f - hi.astype(F32)).astype(BF16)
    return jnp.concatenate([hi, lo, hi], axis=1)


def _dot3(f3, x):
    hi = x.astype(BF16)
    lo = (x - hi.astype(F32)).astype(BF16)
    return jnp.dot(f3, jnp.concatenate([hi, hi, lo], axis=0), preferred_element_type=F32)


def _fft1_kernel(u_ref, f_ref, yr_ref, yi_ref, *, n1):
    f3 = f_ref[...]
    for g in range(FFT_G):
        y = _dot3(f3, u_ref[0, :, g, :])
        yr_ref[0, :, g, :] = y[:n1]
        yi_ref[0, :, g, :] = y[n1:]


def _fft_stage1(u, n1, n2):
    B, L, C = u.shape
    fr, fi = _dft(n1)
    f3 = _split3(np.concatenate([fr[:, :n1 // 2], fi[:, :n1 // 2]], 0))
    G = FFT_G
    blk = pl.BlockSpec((1, n1 // 2, G, C), lambda b, j: (b, 0, j, 0))
    oblk = pl.BlockSpec((1, n1, G, C), lambda b, j: (b, 0, j, 0))
    shp = jax.ShapeDtypeStruct((B, n1, n2, C), F32)
    yr, yi = pl.pallas_call(
        functools.partial(_fft1_kernel, n1=n1), grid=(B, n2 // G),
        in_specs=[blk, pl.BlockSpec(f3.shape, lambda b, j: (0, 0))],
        out_specs=[oblk, oblk], out_shape=[shp, shp],
        compiler_params=_cparams(("parallel", "parallel")), name="fft_stage1")(u.reshape(B, n1 // 2, n2, C), f3)
    return yr.reshape(B, n1 * n2, C), yi.reshape(B, n1 * n2, C)


def _cmul(ar, ai, br, bi):
    return ar * br - ai * bi, ar * bi + ai * br


def _fft2_fwd(yr, yi, tr, ti, f2, n2, C):
    ar, ai = _cmul(yr, yi, tr, ti)
    p = _dot3(f2, jnp.concatenate([ar, ai], axis=1))
    return p[:n2, :C] - p[n2:, C:], p[:n2, C:] + p[n2:, :C]


def _fft2_filter_kernel(yr_ref, yi_ref, tr_ref, ti_ref, f_ref, xr_ref, xi_ref, *, n2):
    C = yr_ref.shape[2]
    tr = jnp.tile(tr_ref[0], (1, C // LANES))
    ti = jnp.tile(ti_ref[0], (1, C // LANES))
    xr, xi = _fft2_fwd(yr_ref[0], yi_ref[0], tr, ti, f_ref[...], n2, C)
    xr_ref[0] = xr
    xi_ref[0] = xi


def _fft2_conv_kernel(yr_ref, yi_ref, tr_ref, ti_ref, f_ref, gr_ref, gi_ref, hr_ref, hi_ref, l1_ref,
                      zr_ref, zi_ref, *, n2, n):
    C = yr_ref.shape[2]
    tr = jnp.tile(tr_ref[0], (1, C // LANES))
    ti = jnp.tile(ti_ref[0], (1, C // LANES))
    f2 = f_ref[...]
    xr, xi = _fft2_fwd(yr_ref[0], yi_ref[0], tr, ti, f2, n2, C)
    inv = 1.0 / l1_ref[...]
    kr = (gr_ref[0] + hr_ref[0]) * inv
    ki = (gi_ref[0] - hi_ref[0]) * inv
    vr, vi = _cmul(xr, xi, kr, ki)
    q = _dot3(f2, jnp.concatenate([vr, vi], axis=1))
    wr = q[:n2, :C] + q[n2:, C:]
    wi = q[:n2, C:] - q[n2:, :C]
    zr, zi = _cmul(wr, wi, tr, -ti)
    zr_ref[0] = zr * (1.0 / n)
    zi_ref[0] = zi * (1.0 / n)


def _fft_tables(n1, n2):
    n = n1 * n2
    k1 = jnp.arange(n1, dtype=jnp.int32)[:, None]
    j2 = jnp.arange(n2, dtype=jnp.int32)[None, :]
    a = (-2.0 * math.pi / n) * (k1 * j2).astype(F32)
    tr = jnp.broadcast_to(jnp.cos(a)[:, :, None], (n1, n2, LANES))
    ti = jnp.broadcast_to(jnp.sin(a)[:, :, None], (n1, n2, LANES))
    fr, fi = _dft(n2)
    return tr, ti, _split3(np.concatenate([fr, fi], 0))


def _fft_stage2_filter(yr, yi, tables, n1, n2):
    B, _, C = yr.shape
    tr, ti, f2 = tables
    blk = pl.BlockSpec((1, n2, C), lambda b, k: (b, k, 0))
    tblk = pl.BlockSpec((1, n2, LANES), lambda b, k: (k, 0, 0))
    shp = jax.ShapeDtypeStruct((B, n1 * n2, C), F32)
    return pl.pallas_call(
        functools.partial(_fft2_filter_kernel, n2=n2), grid=(B, n1),
        in_specs=[blk, blk, tblk, tblk, pl.BlockSpec(f2.shape, lambda b, k: (0, 0))],
        out_specs=[blk, blk], out_shape=[shp, shp],
        compiler_params=_cparams(("parallel", "parallel")), name="fft_stage2_filter")(yr, yi, tr, ti, f2)


def _fft_stage2_conv(yr, yi, tables, spec_r, spec_i, l1, n1, n2):
    B, _, C = yr.shape
    tr, ti, f2 = tables
    blk = pl.BlockSpec((1, n2, C), lambda b, k: (b, k, 0))
    tblk = pl.BlockSpec((1, n2, LANES), lambda b, k: (k, 0, 0))
    gblk = pl.BlockSpec((1, n2, C), lambda b, k: (0, k, 0))
    hblk = pl.BlockSpec((1, n2, C), lambda b, k: (1, k, 0))
    shp = jax.ShapeDtypeStruct((B, n1 * n2, C), F32)
    return pl.pallas_call(
        functools.partial(_fft2_conv_kernel, n2=n2, n=n1 * n2), grid=(B, n1),
        in_specs=[blk, blk, tblk, tblk, pl.BlockSpec(f2.shape, lambda b, k: (0, 0)),
                  gblk, gblk, hblk, hblk, pl.BlockSpec((1, C), lambda b, k: (0, 0))],
        out_specs=[blk, blk], out_shape=[shp, shp],
        compiler_params=_cparams(("parallel", "parallel")), name="fft_stage2_conv")(
            yr, yi, tr, ti, f2, spec_r, spec_i, spec_r, spec_i, l1)


def _fft3_kernel(zr_ref, zi_ref, f_ref, x0_ref, u_ref, bias_ref, o_ref):
    f3 = f_ref[...]
    bias = bias_ref[...]
    for g in range(FFT_G):
        conv = _dot3(f3, jnp.concatenate([zr_ref[0, :, g, :], zi_ref[0, :, g, :]], axis=0))
        u = u_ref[0, :, g, :]
        o_ref[0, :, g, :] = x0_ref[0, :, g, :] * (conv + u * bias)


def _fft_stage3(zr, zi, x0, u, bias, n1, n2):
    B, L, C = u.shape
    fr, fi = _dft(n1)
    f3 = _split3(np.concatenate([fr[:n1 // 2], fi[:n1 // 2]], 1))
    G = FFT_G
    zblk = pl.BlockSpec((1, n1, G, C), lambda b, j: (b, 0, j, 0))
    ublk = pl.BlockSpec((1, n1 // 2, G, C), lambda b, j: (b, 0, j, 0))
    v4 = lambda a: a.reshape(B, -1, n2, C)
    out = pl.pallas_call(
        _fft3_kernel, grid=(B, n2 // G),
        in_specs=[zblk, zblk, pl.BlockSpec(f3.shape, lambda b, j: (0, 0)), ublk, ublk,
                  pl.BlockSpec((1, C), lambda b, j: (0, 0))],
        out_specs=ublk, out_shape=jax.ShapeDtypeStruct((B, n1 // 2, n2, C), F32),
        compiler_params=_cparams(("parallel", "parallel")), name="fft_stage3")(
            v4(zr), v4(zi), f3, v4(x0), v4(u), bias)
    return out.reshape(B * L, C)


def _hyena(proj, conv_w, conv_b, filt, hy_bias, row0, B, L):
    n1, n2 = _fft_factors(2 * L)
    x0, u = _short_conv(proj, conv_w, conv_b, row0, B, L)
    x0 = x0.reshape(B, L, HY_WIDTH)
    u = u.reshape(B, L, HY_WIDTH)
    tables = _fft_tables(n1, n2)
    hfb, l1 = _hyena_filter(L, *filt)
    fr, fi = _fft_stage1(hfb, n1, n2)
    sr, si = _fft_stage2_filter(fr, fi, tables, n1, n2)
    yr, yi = _fft_stage1(u, n1, n2)
    zr, zi = _fft_stage2_conv(yr, yi, tables, sr, si, l1, n1, n2)
    return _fft_stage3(zr, zi, x0, u, hy_bias, n1, n2)


def _merge_kernel(x_ref, g_ref, wg_ref, ya_ref, yb_ref, yc_ref, wa_ref, wb_ref, wc_ref, wo_ref, o_ref):
    x = x_ref[...]
    xn = _rms(x, g_ref[...]).astype(BF16)
    merged = None
    for j, (y_ref, w_ref) in enumerate(((ya_ref, wa_ref), (yb_ref, wb_ref), (yc_ref, wc_ref))):
        gate = jnp.dot(xn, wg_ref[:, j * D_MODEL:(j + 1) * D_MODEL], preferred_element_type=F32)
        br = jnp.dot(y_ref[...].astype(BF16), w_ref[...], preferred_element_type=F32)
        term = jax.nn.sigmoid(gate) * br
        merged = term if merged is None else merged + term
    o_ref[...] = x + jnp.dot(merged.astype(BF16), wo_ref[...], preferred_element_type=F32)


def _merge(x, g, wg, ya, yb, yc, wa, wb, wc, wo):
    T = x.shape[0]
    tm = min(512, T)
    xspec = pl.BlockSpec((tm, D_MODEL), lambda i: (i, 0))
    yspec = pl.BlockSpec((tm, BRANCH_W), lambda i: (i, 0))
    const = lambda shape: pl.BlockSpec(shape, lambda i: (0,) * len(shape))
    return pl.pallas_call(
        _merge_kernel, grid=(T // tm,),
        in_specs=[xspec, const((1, D_MODEL)), const((D_MODEL, 3 * D_MODEL)), yspec, yspec, yspec,
                  const((BRANCH_W, D_MODEL)), const((BRANCH_W, D_MODEL)), const((BRANCH_W, D_MODEL)),
                  const((D_MODEL, D_MODEL))],
        out_specs=xspec, out_shape=jax.ShapeDtypeStruct((T, D_MODEL), F32),
        compiler_params=_cparams(("parallel",)), name="merge_out_projection")(
            x, g, wg, ya, yb, yc, wa, wb, wc, wo)


def _router_kernel(x_ref, g_ref, w_ref, b_ref, xn_ref, r_ref):
    xn = _rms(x_ref[...], g_ref[...])
    xn_ref[...] = xn
    lg = jnp.dot(xn, w_ref[...], precision=HIGHEST, preferred_element_type=F32) + b_ref[...]
    lane = lax.broadcasted_iota(jnp.int32, lg.shape, 1).astype(F32)
    neg = -jnp.inf
    is_g = lane < N_GROUPS
    gl = jnp.where(is_g, lg, neg)
    gmax = jnp.max(gl, axis=1, keepdims=True)
    gsel = jnp.min(jnp.where(gl == gmax, lane, float(LANES)), axis=1, keepdims=True)
    gprob = 1.0 / jnp.sum(jnp.where(is_g, jnp.exp(lg - gmax), 0.0), axis=1, keepdims=True)
    lo = N_GROUPS + gsel * EXPERTS_PER_GROUP
    el = jnp.where((lane >= lo) & (lane < lo + EXPERTS_PER_GROUP), lg, neg)
    v1 = jnp.max(el, axis=1, keepdims=True)
    i1 = jnp.min(jnp.where(el == v1, lane, float(LANES)), axis=1, keepdims=True)
    el2 = jnp.where(lane == i1, neg, el)
    v2 = jnp.max(el2, axis=1, keepdims=True)
    i2 = jnp.min(jnp.where(el2 == v2, lane, float(LANES)), axis=1, keepdims=True)
    e = jnp.exp(v2 - v1)
    w1 = gprob / (1.0 + e)
    w2 = w1 * e
    r_ref[...] = jnp.where(lane == 0, i1 - N_GROUPS,
                           jnp.where(lane == 1, i2 - N_GROUPS,
                                     jnp.where(lane == 2, w1, jnp.where(lane == 3, w2, 0.0))))


def _router(x, g, w, b):
    T = x.shape[0]
    tm = min(512, T)
    xspec = pl.BlockSpec((tm, D_MODEL), lambda i: (i, 0))
    const = lambda shape: pl.BlockSpec(shape, lambda i: (0,) * len(shape))
    return pl.pallas_call(
        _router_kernel, grid=(T // tm,),
        in_specs=[xspec, const((1, D_MODEL)), const((D_MODEL, LANES)), const((1, LANES))],
        out_specs=[xspec, pl.BlockSpec((tm, LANES), lambda i: (i, 0))],
        out_shape=[jax.ShapeDtypeStruct((T, D_MODEL), F32), jax.ShapeDtypeStruct((T, LANES), F32)],
        compiler_params=_cparams(("parallel",)), name="router")(x, g, w, b)


MOE_ROWS = 512


def _moe_kernel(be_ref, idx_hbm, xn_hbm, wg_ref, wu_ref, wd_ref, out_hbm, idx, xbuf, obuf, isem, gsem, ssem):
    R = MOE_ROWS
    i = pl.program_id(0)
    n = pl.num_programs(0)
    s = i % 2

    def idx_copy(blk, slot):
        return pltpu.make_async_copy(idx_hbm.at[blk], idx.at[slot], isem.at[slot])

    def issue_gathers(slot):
        for r in range(R):
            pltpu.make_async_copy(xn_hbm.at[pl.ds(idx[slot, r], 1)], xbuf.at[slot, pl.ds(r, 1)],
                                  gsem.at[slot]).start()

    def wait_gathers(slot):
        pltpu.make_async_copy(xn_hbm.at[pl.ds(0, R)], xbuf.at[slot], gsem.at[slot]).wait()

    def issue_scatters(slot):
        for r in range(R):
            pltpu.make_async_copy(obuf.at[slot, pl.ds(r, 1)], out_hbm.at[pl.ds(idx[slot, R + r], 1)],
                                  ssem.at[slot]).start()

    def wait_scatters(slot):
        pltpu.make_async_copy(obuf.at[slot], out_hbm.at[pl.ds(0, R)], ssem.at[slot]).wait()

    @pl.when(i == 0)
    def _():
        c = idx_copy(0, 0)
        c.start()
        c.wait()
        issue_gathers(0)

        @pl.when(n > 1)
        def _():
            idx_copy(1, 1).start()

    wait_gathers(s)

    @pl.when(i + 1 < n)
    def _():
        idx_copy(i + 1, 1 - s).wait()
        issue_gathers(1 - s)

    @pl.when(i >= 2)
    def _():
        wait_scatters(s)

    xb = xbuf[s].astype(BF16)
    h = jax.nn.silu(jnp.dot(xb, wg_ref[0], preferred_element_type=F32)) \
        * jnp.dot(xb, wu_ref[0], preferred_element_type=F32)
    obuf[s] = jnp.dot(h.astype(BF16), wd_ref[0], preferred_element_type=F32)
    issue_scatters(s)

    @pl.when(i + 2 < n)
    def _():
        idx_copy(i + 2, s).start()

    @pl.when(i == n - 1)
    def _():
        wait_scatters(s)

        @pl.when(n > 1)
        def _():
            wait_scatters(1 - s)


def _moe_experts(xn, idx, block_e, wg, wu, wd):
    T = xn.shape[0]
    R = MOE_ROWS
    n_blocks = idx.shape[0]
    grid_spec = pltpu.PrefetchScalarGridSpec(
        num_scalar_prefetch=1, grid=(n_blocks,),
        in_specs=[pl.BlockSpec(memory_space=pl.ANY), pl.BlockSpec(memory_space=pl.ANY),
                  pl.BlockSpec((1, D_MODEL, EXPERT_HIDDEN), lambda i, be: (be[i], 0, 0)),
                  pl.BlockSpec((1, D_MODEL, EXPERT_HIDDEN), lambda i, be: (be[i], 0, 0)),
                  pl.BlockSpec((1, EXPERT_HIDDEN, D_MODEL), lambda i, be: (be[i], 0, 0))],
        out_specs=pl.BlockSpec(memory_space=pl.ANY),
        scratch_shapes=[pltpu.SMEM((2, 2 * R), jnp.int32), pltpu.VMEM((2, R, D_MODEL), F32),
                        pltpu.VMEM((2, R, D_MODEL), F32), pltpu.SemaphoreType.DMA((2,)),
                        pltpu.SemaphoreType.DMA((2,)), pltpu.SemaphoreType.DMA((2,))])
    return pl.pallas_call(
        _moe_kernel, grid_spec=grid_spec, out_shape=jax.ShapeDtypeStruct((TOP_K * T + 2 * R, D_MODEL), F32),
        compiler_params=_cparams(("arbitrary",)), name="moe_experts")(block_e, idx, xn, wg, wu, wd)


def _dispatch(route, T):
    R = MOE_ROWS
    M = T * TOP_K
    expert_id = route[:, :TOP_K].astype(jnp.int32).reshape(-1)
    order = jnp.argsort(expert_id).astype(jnp.int32)
    sorted_e = expert_id[order]
    counts = jnp.bincount(expert_id, length=N_EXPERTS).astype(jnp.int32)
    starts = jnp.cumsum(counts) - counts
    padded = (counts + R - 1) // R * R
    p_ends = jnp.cumsum(padded)
    p_starts = p_ends - padded
    dest = p_starts[sorted_e] + jnp.arange(M, dtype=jnp.int32) - starts[sorted_e]
    n_blocks = -(-M // R) + N_EXPERTS
    token = order >> 1
    src = jnp.zeros((n_blocks * R,), jnp.int32).at[dest].set(token)
    scrap = M + jnp.arange(n_blocks * R, dtype=jnp.int32) % (2 * R)
    dst = scrap.at[dest].set((order & 1) * T + token)
    block_e = jnp.minimum(jnp.searchsorted(p_ends, jnp.arange(n_blocks, dtype=jnp.int32) * R, side='right'),
                          N_EXPERTS - 1).astype(jnp.int32)
    return jnp.concatenate([src.reshape(n_blocks, R), dst.reshape(n_blocks, R)], axis=1), block_e


def _final_kernel(x_ref, e0_ref, e1_ref, r_ref, g_ref, o_ref):
    r = r_ref[...]
    x = x_ref[...] + e0_ref[...] * r[:, 2:3] + e1_ref[...] * r[:, 3:4]
    o_ref[...] = _rms(x, g_ref[...])


def _final(x, e2, route, g):
    T = x.shape[0]
    tm = min(512, T)
    nT = T // tm
    xspec = pl.BlockSpec((tm, D_MODEL), lambda i: (i, 0))
    return pl.pallas_call(
        _final_kernel, grid=(nT,),
        in_specs=[xspec, xspec, pl.BlockSpec((tm, D_MODEL), lambda i: (nT + i, 0)),
                  pl.BlockSpec((tm, LANES), lambda i: (i, 0)), pl.BlockSpec((1, D_MODEL), lambda i: (0, 0))],
        out_specs=xspec, out_shape=jax.ShapeDtypeStruct((T, D_MODEL), F32),
        compiler_params=_cparams(("parallel",)), name="combine_final_norm")(x, e2, e2, route, g)


def _pack_layer(l, p):
    w = p['w_in'][l]
    o_gate = 3 * HG_WIDTH + 2 * HG_WIDTH + 4 * ML_WIDTH
    o_hy = o_gate + 4 * ML_HEADS
    o_g = o_hy + 3 * HY_WIDTH
    gates = jnp.pad(w[:, o_gate:o_hy], ((0, 0), (0, LANES - 4 * ML_HEADS)))
    w_proj = jnp.concatenate([w[:, :o_gate], w[:, o_hy:o_g], gates], axis=1).astype(BF16)
    w_gate = w[:, o_g:].astype(BF16)
    gate_bias = jnp.pad(jnp.concatenate([p['ml_i_bias'][l].reshape(-1), p['ml_f_bias'][l].reshape(-1)]),
                        (0, LANES - 4 * ML_HEADS)).astype(F32)[None]
    w_router = jnp.pad(jnp.concatenate([p['router_group_w'][l], p['router_expert_w'][l]], axis=1),
                       ((0, 0), (0, LANES - N_GROUPS - N_EXPERTS))).astype(F32)
    b_router = jnp.pad(jnp.concatenate([p['router_group_b'][l], p['router_expert_b'][l]]),
                       (0, LANES - N_GROUPS - N_EXPERTS)).astype(F32)[None]
    return dict(
        norm_mix_g=p['norm_mix_g'][l].astype(F32)[None], w_proj=w_proj, w_gate=w_gate,
        hg_norm_g=p['hg_norm_g'][l].astype(F32)[None], gate_bias=gate_bias,
        ml_norm_g=p['ml_norm_g'][l].astype(F32)[None],
        conv_w=jnp.pad(p['hy_conv_w'][l].astype(F32), ((0, 5), (0, 0))), conv_b=p['hy_conv_b'][l].astype(F32)[None],
        filt=(p['filt_w1'][l], p['filt_b1'][l], p['filt_w2'][l], p['filt_b2'][l], p['filt_freq'][l],
              p['filt_w3'][l]),
        hy_bias=p['hy_bias'][l].astype(F32)[None],
        wa=p['w_branch_a'][l].astype(BF16), wb=p['w_branch_b'][l].astype(BF16),
        wc=p['w_branch_c'][l].astype(BF16), wo=p['w_out'][l].astype(BF16),
        norm_ffn_g=p['norm_ffn_g'][l].astype(F32)[None], w_router=w_router, b_router=b_router,
        wg=p['exp_w_gate'][l].astype(BF16), wu=p['exp_w_up'][l].astype(BF16),
        wd=p['exp_w_down'][l].astype(BF16))


def _trunk(x, groups, p):
    T = x.shape[0]
    lbs = jnp.cumsum(jax.nn.softmax(p['hg_lb'].astype(F32), axis=0), axis=0)
    lbs = lbs - lbs[0:1]
    moe = None
    for l in range(DEPTH):
        lp = _pack_layer(l, p)
        lb = lbs[l][None]
        lbrow = jnp.concatenate([jnp.log(lb), jnp.log1p(-lb), 1.0 - lb, jnp.zeros((5, HG_WIDTH), F32)], axis=0)
        if moe is None:
            proj = _in_projection(x, lp['norm_mix_g'], lp['w_proj'])
        else:
            proj, x = _in_projection(x, lp['norm_mix_g'], lp['w_proj'], moe)
        ya, yb, yc = [], [], []
        for row0, B, L in groups:
            ya.append(_hgrn2(proj, lbrow, lp['hg_norm_g'], row0, B, L))
            yb.append(_mlstm(proj, lp['gate_bias'], lp['ml_norm_g'], row0, B, L))
            yc.append(_hyena(proj, lp['conv_w'], lp['conv_b'], lp['filt'], lp['hy_bias'], row0, B, L))
        ya, yb, yc = (jnp.concatenate(v, axis=0) for v in (ya, yb, yc))
        x = _merge(x, lp['norm_mix_g'], lp['w_gate'], ya, yb, yc, lp['wa'], lp['wb'], lp['wc'], lp['wo'])
        xn, route = _router(x, lp['norm_ffn_g'], lp['w_router'], lp['b_router'])
        rows, block_e = _dispatch(route, T)
        e2 = _moe_experts(xn, rows, block_e, lp['wg'], lp['wu'], lp['wd'])
        moe = (e2, route)
    return _final(x, moe[0], moe[1], p['final_norm_g'].astype(F32)[None])


def kernel(x_prompt, x_sample, norm_mix_g, w_in, hg_lb, hg_norm_g, ml_i_bias, ml_f_bias, ml_norm_g, hy_conv_w, hy_conv_b, filt_w1, filt_b1, filt_w2, filt_b2, filt_freq, filt_w3, hy_bias, w_branch_a, w_branch_b, w_branch_c, w_out, norm_ffn_g, router_group_w, router_group_b, router_expert_w, router_expert_b, exp_w_gate, exp_w_up, exp_w_down, final_norm_g):
    p = dict(norm_mix_g=norm_mix_g, w_in=w_in, hg_lb=hg_lb, hg_norm_g=hg_norm_g, ml_i_bias=ml_i_bias,
             ml_f_bias=ml_f_bias, ml_norm_g=ml_norm_g, hy_conv_w=hy_conv_w, hy_conv_b=hy_conv_b,
             filt_w1=filt_w1, filt_b1=filt_b1, filt_w2=filt_w2, filt_b2=filt_b2, filt_freq=filt_freq,
             filt_w3=filt_w3, hy_bias=hy_bias, w_branch_a=w_branch_a, w_branch_b=w_branch_b,
             w_branch_c=w_branch_c, w_out=w_out, norm_ffn_g=norm_ffn_g, router_group_w=router_group_w,
             router_group_b=router_group_b, router_expert_w=router_expert_w,
             router_expert_b=router_expert_b, exp_w_gate=exp_w_gate, exp_w_up=exp_w_up,
             exp_w_down=exp_w_down, final_norm_g=final_norm_g)
    Bp, Lp, _ = x_prompt.shape
    Bs, Ls, _ = x_sample.shape
    Tp, Ts = Bp * Lp, Bs * Ls
    x = jnp.concatenate([x_prompt.reshape(Tp, D_MODEL), x_sample.reshape(Ts, D_MODEL)], axis=0).astype(F32)
    y = _trunk(x, ((0, Bp, Lp), (Tp, Bs, Ls)), p)
    return (y[:Tp].reshape(Bp, Lp, D_MODEL), y[Tp:].reshape(Bs, Ls, D_MODEL))
```

```python
import functools
import math

import numpy as np
import jax
import jax.numpy as jnp
from jax import lax
from jax.experimental import pallas as pl
from jax.experimental.pallas import tpu as pltpu

F32 = jnp.float32
BF16 = jnp.bfloat16
HIGHEST = lax.Precision.HIGHEST

D_MODEL = 1024
EPS = 1e-6
DEPTH = 2
HG_HEADS, HG_KDIM, HG_WIDTH, HG_CHUNK = 8, 64, 512, 32
ML_HEADS, ML_HDIM, ML_WIDTH, ML_CHUNK = 4, 128, 512, 128
HY_WIDTH, HY_BANDS, HY_FILTER_HIDDEN = 512, 16, 64
HY_EMB = 1 + 2 * HY_BANDS
HY_SHORT_DECAY_PCT, HY_LONG_DECAY_PCT, HY_DECAY_TARGET = 0.3, 1.5, 1e-2
N_GROUPS, EXPERTS_PER_GROUP, TOP_K = 4, 8, 2
N_EXPERTS = N_GROUPS * EXPERTS_PER_GROUP
EXPERT_HIDDEN = D_MODEL // 2
MOE_BLOCK = 128

LANES = 128
BRANCH_W = 512
SLAB_HQ, SLAB_HFF, SLAB_HFB, SLAB_HI, SLAB_HG = 0, 1, 2, 3, 4
SLAB_MQ, SLAB_MK, SLAB_MV, SLAB_MO = 5, 6, 7, 8
SLAB_HY = 9
PROJ_MAIN = 12 * BRANCH_W
PROJ_W = PROJ_MAIN + LANES
GATE_BLOCK = PROJ_MAIN // LANES
VMEM_LIMIT = 48 * 1024 * 1024
VMEM_LIMIT_WIDE = 56 * 1024 * 1024


def _cparams(sem, vmem=VMEM_LIMIT):
    return pltpu.CompilerParams(dimension_semantics=sem, vmem_limit_bytes=vmem)


def _rms(x, g):
    return x * lax.rsqrt(jnp.mean(x * x, axis=-1, keepdims=True) + EPS) * g


def _log_sigmoid(z):
    return jnp.minimum(z, 0.0) - jnp.log1p(jnp.exp(-jnp.abs(z)))


def _inproj_kernel(x_ref, g_ref, w_ref, o_ref, xn_ref):
    @pl.when(pl.program_id(1) == 0)
    def _():
        xn_ref[...] = _rms(x_ref[...], g_ref[...]).astype(BF16)

    o_ref[...] = jnp.dot(xn_ref[...], w_ref[...], preferred_element_type=F32)


def _inproj_combine_kernel(x_ref, e0_ref, e1_ref, r_ref, g_ref, w_ref, o_ref, xo_ref, xn_ref):
    @pl.when(pl.program_id(1) == 0)
    def _():
        r = r_ref[...]
        x = x_ref[...] + e0_ref[...] * r[:, 2:3] + e1_ref[...] * r[:, 3:4]
        xo_ref[...] = x
        xn_ref[...] = _rms(x, g_ref[...]).astype(BF16)

    o_ref[...] = jnp.dot(xn_ref[...], w_ref[...], preferred_element_type=F32)


def _in_projection(x, g, w, moe=None):
    T = x.shape[0]
    tm = min(1024, T)
    tn = PROJ_W // 7
    grid = (T // tm, PROJ_W // tn)
    xspec = pl.BlockSpec((tm, D_MODEL), lambda i, j: (i, 0))
    gspec = pl.BlockSpec((1, D_MODEL), lambda i, j: (0, 0))
    wspec = pl.BlockSpec((D_MODEL, tn), lambda i, j: (0, j))
    ospec = pl.BlockSpec((tm, tn), lambda i, j: (i, j))
    scratch = [pltpu.VMEM((tm, D_MODEL), BF16)]
    cp = _cparams(("parallel", "arbitrary"))
    if moe is None:
        return pl.pallas_call(
            _inproj_kernel, grid=grid, in_specs=[xspec, gspec, wspec], out_specs=ospec,
            out_shape=jax.ShapeDtypeStruct((T, PROJ_W), F32), scratch_shapes=scratch,
            compiler_params=cp, name="in_projection")(x, g, w)
    e2, route = moe
    nT = T // tm
    e0spec = pl.BlockSpec((tm, D_MODEL), lambda i, j: (i, 0))
    e1spec = pl.BlockSpec((tm, D_MODEL), lambda i, j: (nT + i, 0))
    rspec = pl.BlockSpec((tm, LANES), lambda i, j: (i, 0))
    proj, xo = pl.pallas_call(
        _inproj_combine_kernel, grid=grid, in_specs=[xspec, e0spec, e1spec, rspec, gspec, wspec],
        out_specs=[ospec, xspec],
        out_shape=[jax.ShapeDtypeStruct((T, PROJ_W), F32), jax.ShapeDtypeStruct((T, D_MODEL), F32)],
        scratch_shapes=scratch, compiler_params=_cparams(("parallel", "arbitrary"), VMEM_LIMIT_WIDE),
        name="in_projection_combine")(x, e2, e2, route, g, w)
    return proj, xo


HG_TB = 256
HG_PAIRS = HG_WIDTH // LANES


HG_SUB = 8
HG_NSUB = HG_CHUNK // HG_SUB
HG_XROWS = (HG_NSUB - 1) * HG_HEADS * HG_SUB
HG_XCOLS = HG_SUB * HG_NSUB * (HG_NSUB - 1) // 2


def _hgrn2_tiles(reverse):
    if reverse:
        return [(i, (i + 1) * HG_SUB, HG_CHUNK - (i + 1) * HG_SUB) for i in range(HG_NSUB - 1)]
    return [(i, 0, i * HG_SUB) for i in range(1, HG_NSUB)]


def _hgrn2_kernel(*refs, reverse, finalize):
    if finalize:
        (q_ref, z_ref, v_ref, lb_ref, tri_ref, bd_ref, xm_ref, of_ref, gate_ref, ng_ref,
         o_ref, st_ref) = refs
    else:
        (q_ref, z_ref, v_ref, lb_ref, tri_ref, bd_ref, xm_ref, o_ref, st_ref) = refs
    C, c, nb, W = HG_CHUNK, HG_SUB, HG_NSUB, HG_WIDTH
    n_chunks = HG_TB // C

    @pl.when(pl.program_id(1) == 0)
    def _():
        st_ref[...] = jnp.zeros_like(st_ref)

    log_lb = lb_ref[0:1, :]
    log_1mlb = lb_ref[1:2, :]
    one_m_lb = lb_ref[2:3, :]
    tri3 = tri_ref[...]
    bd = bd_ref[...]
    bd_bf = bd.astype(BF16)
    xmask = xm_ref[...]
    sub_row = lax.broadcasted_iota(jnp.int32, (nb, c, W), 1)
    lane_head = lax.broadcasted_iota(jnp.int32, (c, W), 1) // HG_KDIM
    tiles = _hgrn2_tiles(reverse)

    def chunk(i):
        ci = (n_chunks - 1 - i) if reverse else i
        off = pl.multiple_of(ci * C, C)
        q = q_ref[pl.ds(off, C), :]
        z = z_ref[pl.ds(off, C), :]
        v = v_ref[pl.ds(off, C), :]
        ls = jnp.minimum(z, 0.0) - jnp.log(1.0 + jnp.exp(-jnp.abs(z)))
        hi = log_1mlb + ls
        mx = jnp.maximum(log_lb, hi)
        lf = mx + jnp.log(1.0 + jnp.exp(-jnp.abs(log_lb - hi)))
        kk = one_m_lb * jnp.exp(ls - z)
        l1 = lf.astype(BF16)
        r1 = lf - l1.astype(F32)
        l2 = r1.astype(BF16)
        l3 = (r1 - l2.astype(F32)).astype(BF16)
        b = jnp.dot(tri3, jnp.concatenate([l1, l2, l3], axis=0), preferred_element_type=F32)
        btot = b[0:1, :] if reverse else b[C - 1:C, :]
        qb = (q * jnp.exp(b)).astype(BF16)
        kb = (kk * jnp.exp(btot - b)).astype(BF16)
        dec = jnp.exp(btot)
        vb = v.astype(BF16)

        q3, k3, v3, b3 = (a.reshape(nb, c, W) for a in (q, kk, v, b))
        a_rows = [(q * kk).astype(BF16)]
        for d in range(1, c):
            sh = c - d if reverse else d
            ok = (sub_row < c - d) if reverse else (sub_row >= d)
            a = jnp.where(ok, q3 * pltpu.roll(k3, sh, axis=1) * jnp.exp(b3 - pltpu.roll(b3, sh, axis=1)), 0.0)
            a_rows.append(a.reshape(C, W).astype(BF16))
        a_all = jnp.concatenate(a_rows, axis=0)
        sums = jnp.concatenate(
            [jnp.dot(a_all[:, p * LANES:(p + 1) * LANES], bd_bf, preferred_element_type=F32)
             for p in range(HG_PAIRS)], axis=1)
        acc3 = sums[0:C].reshape(nb, c, W) * v3
        for d in range(1, c):
            sh = c - d if reverse else d
            acc3 = acc3 + sums[d * C:(d + 1) * C].reshape(nb, c, W) * pltpu.roll(v3, sh, axis=1)
        acc = acc3.reshape(C, W)

        qx, kx, vx = [], [], []
        for (ti, s0, sn) in tiles:
            edge = s0 if reverse else s0 + sn - 1
            r = b[edge:edge + 1, :]
            rows = slice(ti * c, (ti + 1) * c)
            qh = q[rows] * jnp.exp(b[rows] - r)
            qx += [jnp.where(lane_head == h, qh, 0.0) for h in range(HG_HEADS)]
            kx.append(kk[s0:s0 + sn] * jnp.exp(r - b[s0:s0 + sn]))
            vx.append(v[s0:s0 + sn])
        qx = jnp.concatenate(qx, axis=0).astype(BF16)
        kx = jnp.concatenate(kx, axis=0).astype(BF16)
        vx = jnp.concatenate(vx, axis=0).astype(BF16)
        sc = lax.dot_general(qx, kx, (((1,), (1,)), ((), ())), preferred_element_type=F32) * xmask
        px = jnp.dot(sc.astype(BF16), vx, preferred_element_type=F32)
        offd = {}
        for n, (ti, s0, sn) in enumerate(tiles):
            base = n * HG_HEADS * c
            t_acc = jnp.where(lane_head == 0, px[base:base + c], 0.0)
            for h in range(1, HG_HEADS):
                t_acc = t_acc + jnp.where(lane_head == h, px[base + h * c:base + (h + 1) * c], 0.0)
            offd[ti] = t_acc
        acc = acc + jnp.concatenate([offd.get(ti, jnp.zeros((c, W), F32)) for ti in range(nb)], axis=0)

        parts = []
        for p in range(HG_PAIRS):
            sl = slice(p * LANES, (p + 1) * LANES)
            s_t = st_ref[p]
            parts.append(lax.dot_general(qb[:, sl], s_t.astype(BF16), (((1,), (1,)), ((), ())),
                                         preferred_element_type=F32))
            upd = lax.dot_general(vb[:, sl], kb[:, sl], (((0,), (0,)), ((), ())),
                                  preferred_element_type=F32)
            st_ref[p] = s_t * dec[:, sl] + upd * bd
        acc = acc + jnp.concatenate(parts, axis=1)

        if finalize:
            o = acc + of_ref[pl.ds(off, C), :]
            o2 = (o * o).astype(BF16)
            ms = jnp.concatenate(
                [jnp.dot(o2[:, p * LANES:(p + 1) * LANES], bd_bf, preferred_element_type=F32)
                 for p in range(HG_PAIRS)], axis=1) * (1.0 / HG_KDIM)
            g = gate_ref[pl.ds(off, C), :]
            y = o * lax.rsqrt(ms + EPS) * ng_ref[...] * (g * jax.nn.sigmoid(g))
            o_ref[pl.ds(off, C), :] = y.astype(o_ref.dtype)
        else:
            o_ref[pl.ds(off, C), :] = acc

    def chunk_pair(j, carry):
        chunk(2 * j)
        chunk(2 * j + 1)
        return carry

    lax.fori_loop(0, n_chunks // 2, chunk_pair, 0)


def _hgrn2(proj, lbrow, norm_g, row0, B, L):
    C = HG_CHUNK
    nT = L // HG_TB
    blk0 = row0 // HG_TB
    tri_f = jnp.asarray(np.tile(np.tril(np.ones((C, C), np.float32)), (1, 3))).astype(BF16)
    tri_b = jnp.asarray(np.tile(np.triu(np.ones((C, C), np.float32)), (1, 3))).astype(BF16)
    head = np.arange(LANES) // HG_KDIM
    bd = jnp.asarray((head[:, None] == head[None, :]).astype(np.float32))

    def tile_mask(reverse):
        m = np.zeros((HG_XROWS, HG_XCOLS), np.float32)
        col = 0
        for n, (_, _, sn) in enumerate(_hgrn2_tiles(reverse)):
            m[n * HG_HEADS * HG_SUB:(n + 1) * HG_HEADS * HG_SUB, col:col + sn] = 1.0
            col += sn
        return jnp.asarray(m)

    def in_spec(slab, reverse):
        if reverse:
            return pl.BlockSpec((HG_TB, BRANCH_W), lambda b, t: (blk0 + b * nT + nT - 1 - t, slab))
        return pl.BlockSpec((HG_TB, BRANCH_W), lambda b, t: (blk0 + b * nT + t, slab))

    def out_spec(reverse):
        if reverse:
            return pl.BlockSpec((HG_TB, BRANCH_W), lambda b, t: (b * nT + nT - 1 - t, 0))
        return pl.BlockSpec((HG_TB, BRANCH_W), lambda b, t: (b * nT + t, 0))

    const = lambda shape: pl.BlockSpec(shape, lambda b, t: (0,) * len(shape))
    scratch = [pltpu.VMEM((HG_PAIRS, LANES, LANES), F32)]
    consts = [const((8, BRANCH_W)), const((C, 3 * C)), const((LANES, LANES)), const((HG_XROWS, HG_XCOLS))]
    cp = _cparams(("parallel", "arbitrary"))
    o_f = pl.pallas_call(
        functools.partial(_hgrn2_kernel, reverse=False, finalize=False), grid=(B, nT),
        in_specs=[in_spec(SLAB_HQ, False), in_spec(SLAB_HFF, False), in_spec(SLAB_HI, False)] + consts,
        out_specs=out_spec(False), out_shape=jax.ShapeDtypeStruct((B * L, BRANCH_W), F32),
        scratch_shapes=scratch, compiler_params=cp, name="hgrn2_fwd")(
            proj, proj, proj, lbrow, tri_f, bd, tile_mask(False))
    return pl.pallas_call(
        functools.partial(_hgrn2_kernel, reverse=True, finalize=True), grid=(B, nT),
        in_specs=[in_spec(SLAB_HQ, True), in_spec(SLAB_HFB, True), in_spec(SLAB_HI, True)] + consts
        + [out_spec(True), in_spec(SLAB_HG, True), const((1, BRANCH_W))],
        out_specs=out_spec(True), out_shape=jax.ShapeDtypeStruct((B * L, BRANCH_W), BF16),
        scratch_shapes=scratch, compiler_params=cp, name="hgrn2_bwd")(
            proj, proj, proj, lbrow, tri_b, bd, tile_mask(True), o_f, proj, norm_g)


def _mlstm_kernel(*refs, reverse, finalize):
    if finalize:
        (q_ref, k_ref, v_ref, gt_ref, gb_ref, tri_ref, hf_ref, og_ref, ng_ref,
         o_ref, c_ref, n_ref, m_ref) = refs
    else:
        (q_ref, k_ref, v_ref, gt_ref, gb_ref, tri_ref, o_ref, c_ref, n_ref, m_ref) = refs
    C = ML_CHUNK

    @pl.when(pl.program_id(1) == 0)
    def _():
        c_ref[...] = jnp.zeros_like(c_ref)
        n_ref[...] = jnp.zeros_like(n_ref)
        m_ref[...] = jnp.zeros_like(m_ref)

    tri = tri_ref[...]
    lane = lax.broadcasted_iota(jnp.int32, (C, LANES), 1)
    ti = lax.broadcasted_iota(jnp.int32, (C, C), 0)
    si = lax.broadcasted_iota(jnp.int32, (C, C), 1)
    causal = (si >= ti) if reverse else (si <= ti)
    dsel = ML_HEADS if reverse else 0
    edge = 0 if reverse else C - 1
    scale = 1.0 / math.sqrt(ML_HDIM)
    order = range(ML_TB // C - 1, -1, -1) if reverse else range(ML_TB // C)
    for ci in order:
        _mlstm_chunk(refs, slice(ci * C, (ci + 1) * C), tri, lane, causal, dsel, edge, scale, finalize)


def _mlstm_chunk(refs, rws, tri, lane, causal, dsel, edge, scale, finalize):
    if finalize:
        (q_ref, k_ref, v_ref, gt_ref, gb_ref, tri_ref, hf_ref, og_ref, ng_ref,
         o_ref, c_ref, n_ref, m_ref) = refs
    else:
        (q_ref, k_ref, v_ref, gt_ref, gb_ref, tri_ref, o_ref, c_ref, n_ref, m_ref) = refs
    gates = gt_ref[rws, :] + gb_ref[...]
    g_col = jnp.where(lane >= 2 * ML_HEADS, _log_sigmoid(gates), gates)
    g_row = g_col.T
    b_col = jnp.dot(tri, g_col, precision=HIGHEST, preferred_element_type=F32)
    b_row = lax.dot_general(g_row, tri, (((1,), (1,)), ((), ())), precision=HIGHEST,
                            preferred_element_type=F32)

    outs = []
    for h in range(ML_HEADS):
        sl = slice(h * ML_HDIM, (h + 1) * ML_HDIM)
        li, lf = dsel + h, 2 * ML_HEADS + dsel + h
        q = q_ref[rws, sl]
        kc = k_ref[rws, sl] * scale
        v = v_ref[rws, sl]
        qb, kb, vb = q.astype(BF16), kc.astype(BF16), v.astype(BF16)
        bc = b_col[:, lf:lf + 1]
        br = b_row[lf:lf + 1, :]
        ic = g_col[:, li:li + 1]
        ir = g_row[li:li + 1, :]
        btot = b_col[edge:edge + 1, lf:lf + 1]
        m_prev = m_ref[0:1, h:h + 1]
        n_prev = n_ref[h:h + 1, :]
        c_prev = c_ref[h]

        log_d = jnp.where(causal, bc - br + ir, -jnp.inf)
        log_inter = bc + m_prev
        m_t = jnp.maximum(log_inter, jnp.max(log_d, axis=1, keepdims=True))
        a_inter = jnp.exp(log_inter - m_t)
        s = lax.dot_general(qb, kb, (((1,), (1,)), ((), ())), preferred_element_type=F32)
        s = s * jnp.exp(log_d - m_t)
        num = a_inter * jnp.dot(qb, c_prev.astype(BF16), preferred_element_type=F32) \
            + jnp.dot(s.astype(BF16), vb, preferred_element_type=F32)
        den = a_inter * jnp.sum(q * n_prev, axis=1, keepdims=True) + jnp.sum(s, axis=1, keepdims=True)
        outs.append(num / jnp.maximum(jnp.abs(den), jnp.exp(-m_t)))

        log_w = btot - bc + ic
        m_new = jnp.maximum(btot + m_prev, jnp.max(log_w, axis=0, keepdims=True))
        a_state = jnp.exp(btot + m_prev - m_new)
        kw = kc * jnp.exp(log_w - m_new)
        c_ref[h] = a_state * c_prev + lax.dot_general(kw.astype(BF16), vb, (((0,), (0,)), ((), ())),
                                                      preferred_element_type=F32)
        n_ref[h:h + 1, :] = a_state * n_prev + jnp.sum(kw, axis=0, keepdims=True)
        m_ref[0:1, h:h + 1] = m_new

    hcur = jnp.concatenate(outs, axis=1)
    if finalize:
        hsum = hcur + hf_ref[rws, :]
        ys = []
        for h in range(ML_HEADS):
            sl = slice(h * ML_HDIM, (h + 1) * ML_HDIM)
            hh = hsum[:, sl]
            ys.append(hh * lax.rsqrt(jnp.mean(hh * hh, axis=1, keepdims=True) + EPS))
        y = jnp.concatenate(ys, axis=1) * ng_ref[...] * jax.nn.sigmoid(og_ref[rws, :])
        o_ref[rws, :] = y.astype(o_ref.dtype)
    else:
        o_ref[rws, :] = hcur


ML_TB = 2 * ML_CHUNK


def _mlstm(proj, gate_bias, norm_g, row0, B, L):
    C = ML_TB
    nT = L // C
    blk0 = row0 // C
    tri_f = jnp.asarray(np.tril(np.ones((ML_CHUNK, ML_CHUNK), np.float32)))
    tri_b = jnp.asarray(np.triu(np.ones((ML_CHUNK, ML_CHUNK), np.float32)))

    def rows(reverse):
        if reverse:
            return lambda b, t: blk0 + b * nT + nT - 1 - t
        return lambda b, t: blk0 + b * nT + t

    def in_spec(slab, reverse):
        r = rows(reverse)
        return pl.BlockSpec((C, BRANCH_W), lambda b, t: (r(b, t), slab))

    def gate_spec(reverse):
        r = rows(reverse)
        return pl.BlockSpec((C, LANES), lambda b, t: (r(b, t), GATE_BLOCK))

    def out_spec(reverse):
        if reverse:
            return pl.BlockSpec((C, BRANCH_W), lambda b, t: (b * nT + nT - 1 - t, 0))
        return pl.BlockSpec((C, BRANCH_W), lambda b, t: (b * nT + t, 0))

    const = lambda shape: pl.BlockSpec(shape, lambda b, t: (0,) * len(shape))
    scratch = [pltpu.VMEM((ML_HEADS, ML_HDIM, ML_HDIM), F32), pltpu.VMEM((8, ML_HDIM), F32),
               pltpu.VMEM((8, LANES), F32)]
    cp = _cparams(("parallel", "arbitrary"))
    h_f = pl.pallas_call(
        functools.partial(_mlstm_kernel, reverse=False, finalize=False), grid=(B, nT),
        in_specs=[in_spec(SLAB_MQ, False), in_spec(SLAB_MK, False), in_spec(SLAB_MV, False),
                  gate_spec(False), const((1, LANES)), const((ML_CHUNK, ML_CHUNK))],
        out_specs=out_spec(False), out_shape=jax.ShapeDtypeStruct((B * L, BRANCH_W), F32),
        scratch_shapes=scratch, compiler_params=cp, name="mlstm_fwd")(
            proj, proj, proj, proj, gate_bias, tri_f)
    return pl.pallas_call(
        functools.partial(_mlstm_kernel, reverse=True, finalize=True), grid=(B, nT),
        in_specs=[in_spec(SLAB_MQ, True), in_spec(SLAB_MK, True), in_spec(SLAB_MV, True),
                  gate_spec(True), const((1, LANES)), const((ML_CHUNK, ML_CHUNK)),
                  out_spec(True), in_spec(SLAB_MO, True), const((1, BRANCH_W))],
        out_specs=out_spec(True), out_shape=jax.ShapeDtypeStruct((B * L, BRANCH_W), BF16),
        scratch_shapes=scratch, compiler_params=cp, name="mlstm_bwd")(
            proj, proj, proj, proj, gate_bias, tri_b, h_f, proj, norm_g)


HY_TB = 256


def _shortconv_kernel(c_ref, p_ref, n_ref, w_ref, b_ref, x0_ref, u_ref, *, nT):
    t = pl.program_id(1)
    cur = c_ref[...]
    prev_row = jnp.where(t > 0, p_ref[7:8, :], 0.0)
    next_row = jnp.where(t < nT - 1, n_ref[0:1, :], 0.0)
    row = lax.broadcasted_iota(jnp.int32, cur.shape, 0)
    up = jnp.where(row == 0, prev_row, pltpu.roll(cur, 1, axis=0))
    dn = jnp.where(row == HY_TB - 1, next_row, pltpu.roll(cur, HY_TB - 1, axis=0))
    y = up * w_ref[0:1, :] + cur * w_ref[1:2, :] + dn * w_ref[2:3, :] + b_ref[...]
    x0_ref[...] = y[:, :HY_WIDTH]
    u_ref[...] = y[:, HY_WIDTH:2 * HY_WIDTH] * y[:, 2 * HY_WIDTH:]


def _short_conv(proj, w, b, row0, B, L):
    nT = L // HY_TB
    blk0 = row0 // HY_TB
    sub = HY_TB // 8
    W3 = 3 * HY_WIDTH
    slab = SLAB_HY * BRANCH_W // W3
    cur = pl.BlockSpec((HY_TB, W3), lambda bb, t: (blk0 + bb * nT + t, slab))
    prv = pl.BlockSpec((8, W3), lambda bb, t: (jnp.maximum((blk0 + bb * nT + t) * sub - 1, 0), slab))
    nxt = pl.BlockSpec((8, W3), lambda bb, t: (jnp.minimum((blk0 + bb * nT + t + 1) * sub,
                                                           (blk0 + B * nT) * sub - 1), slab))
    const = lambda shape: pl.BlockSpec(shape, lambda bb, t: (0,) * len(shape))
    out = pl.BlockSpec((HY_TB, HY_WIDTH), lambda bb, t: (bb * nT + t, 0))
    return pl.pallas_call(
        functools.partial(_shortconv_kernel, nT=nT), grid=(B, nT),
        in_specs=[cur, prv, nxt, const((8, W3)), const((1, W3))], out_specs=[out, out],
        out_shape=[jax.ShapeDtypeStruct((B * L, HY_WIDTH), F32)] * 2,
        compiler_params=_cparams(("parallel", "parallel")), name="hyena_short_conv")(proj, proj, proj, w, b)


HYF_TB = 256


def _filter_kernel(band_ref, w1_ref, b1_ref, w2_ref, b2_ref, fr_ref, w3_ref, dl_ref, h_ref, l1_ref, *, L):
    i = pl.program_id(0)
    pos = (lax.broadcasted_iota(jnp.int32, (HYF_TB, LANES), 0) + i * HYF_TB).astype(F32)
    lane = lax.broadcasted_iota(jnp.int32, (HYF_TB, LANES), 1)
    t = pos / (L - 1)
    ang = (2.0 * math.pi * pos / L) * band_ref[...]
    z = jnp.where(lane == 0, t,
                  jnp.where(lane <= HY_BANDS, jnp.cos(ang),
                            jnp.where(lane <= 2 * HY_BANDS, -jnp.sin(ang), 0.0)))
    fr = fr_ref[...]
    h = jnp.sin(fr * (jnp.dot(z, w1_ref[...], precision=HIGHEST, preferred_element_type=F32) + b1_ref[...]))
    h = jnp.sin(fr * (jnp.dot(h, w2_ref[...], precision=HIGHEST, preferred_element_type=F32) + b2_ref[...]))
    h = jnp.dot(h, w3_ref[...], precision=HIGHEST, preferred_element_type=F32)
    tt = (lax.broadcasted_iota(jnp.int32, (HYF_TB, HY_WIDTH), 0) + i * HYF_TB).astype(F32) / (L - 1)
    window = jnp.exp(-tt * dl_ref[...])
    rowi = lax.broadcasted_iota(jnp.int32, (HYF_TB, HY_WIDTH), 0) + i * HYF_TB
    hf = h[:, :HY_WIDTH] * window
    hb = jnp.where(rowi == 0, 0.0, h[:, HY_WIDTH:] * window)
    h_ref[0] = hf
    h_ref[1] = hb

    @pl.when(i == 0)
    def _():
        l1_ref[...] = jnp.zeros_like(l1_ref)

    l1_ref[...] += jnp.sum(jnp.abs(hf) + jnp.abs(hb), axis=0, keepdims=True)


def _hyena_filter(L, w1, b1, w2, b2, freq, w3):
    band = np.zeros((1, LANES), np.float32)
    bands = np.linspace(1e-4, HY_BANDS - 1, HY_BANDS, dtype=np.float32)
    band[0, 1:1 + HY_BANDS] = bands
    band[0, 1 + HY_BANDS:1 + 2 * HY_BANDS] = bands
    max_decay = math.log(HY_DECAY_TARGET) / HY_SHORT_DECAY_PCT
    min_decay = math.log(HY_DECAY_TARGET) / HY_LONG_DECAY_PCT
    deltas = np.abs(np.linspace(min_decay, max_decay, HY_WIDTH, dtype=np.float32))[None, :]
    w1p = jnp.zeros((LANES, HY_FILTER_HIDDEN), F32).at[:HY_EMB].set(w1.astype(F32))
    const = lambda shape: pl.BlockSpec(shape, lambda i: (0,) * len(shape))
    H = HY_FILTER_HIDDEN
    return pl.pallas_call(
        functools.partial(_filter_kernel, L=L), grid=(L // HYF_TB,),
        in_specs=[const((1, LANES)), const((LANES, H)), const((1, H)), const((H, H)), const((1, H)),
                  const((1, H)), const((H, 2 * HY_WIDTH)), const((1, HY_WIDTH))],
        out_specs=[pl.BlockSpec((2, HYF_TB, HY_WIDTH), lambda i: (0, i, 0)), const((1, HY_WIDTH))],
        out_shape=[jax.ShapeDtypeStruct((2, L, HY_WIDTH), F32), jax.ShapeDtypeStruct((1, HY_WIDTH), F32)],
        compiler_params=_cparams(("arbitrary",)), name="hyena_filter")(
            jnp.asarray(band), w1p, b1.astype(F32)[None], w2.astype(F32), b2.astype(F32)[None],
            freq.astype(F32)[None], w3.astype(F32), jnp.asarray(deltas))


def _fft_factors(n):
    lg = int(round(math.log2(n)))
    n1 = 1 << (lg // 2)
    return n1, n // n1


def _dft(n):
    k = np.arange(n)
    a = -2.0 * np.pi * ((k[:, None] * k[None, :]) % n) / n
    return np.cos(a), np.sin(a)


FFT_G = 8


def _split3(f):
    f = jnp.asarray(np.asarray(f, np.float32))
    hi = f.astype(BF16)
    lo = (f - hi.astype(F32)).astype(BF16)
    return jnp.concatenate([hi, lo, hi], axis=1)


def _dot3(f3, x):
    hi = x.astype(BF16)
    lo = (x - hi.astype(F32)).astype(BF16)
    return jnp.dot(f3, jnp.concatenate([hi, hi, lo], axis=0), preferred_element_type=F32)


def _fft1_kernel(u_ref, f_ref, yr_ref, yi_ref, *, n1):
    f3 = f_ref[...]
    for g in range(FFT_G):
        y = _dot3(f3, u_ref[0, :, g, :])
        yr_ref[0, :, g, :] = y[:n1]
        yi_ref[0, :, g, :] = y[n1:]


def _fft_stage1(u, n1, n2):
    B, L, C = u.shape
    fr, fi = _dft(n1)
    f3 = _split3(np.concatenate([fr[:, :n1 // 2], fi[:, :n1 // 2]], 0))
    G = FFT_G
    blk = pl.BlockSpec((1, n1 // 2, G, C), lambda b, j: (b, 0, j, 0))
    oblk = pl.BlockSpec((1, n1, G, C), lambda b, j: (b, 0, j, 0))
    shp = jax.ShapeDtypeStruct((B, n1, n2, C), F32)
    yr, yi = pl.pallas_call(
        functools.partial(_fft1_kernel, n1=n1), grid=(B, n2 // G),
        in_specs=[blk, pl.BlockSpec(f3.shape, lambda b, j: (0, 0))],
        out_specs=[oblk, oblk], out_shape=[shp, shp],
        compiler_params=_cparams(("parallel", "parallel")), name="fft_stage1")(u.reshape(B, n1 // 2, n2, C), f3)
    return yr.reshape(B, n1 * n2, C), yi.reshape(B, n1 * n2, C)


def _cmul(ar, ai, br, bi):
    return ar * br - ai * bi, ar * bi + ai * br


def _fft2_fwd(yr, yi, tr, ti, f2, n2, C):
    ar, ai = _cmul(yr, yi, tr, ti)
    p = _dot3(f2, jnp.concatenate([ar, ai], axis=1))
    return p[:n2, :C] - p[n2:, C:], p[:n2, C:] + p[n2:, :C]


FFT_ROWS = 512


def _fft2_filter_kernel(yr_ref, yi_ref, tr_ref, ti_ref, f_ref, l1_ref, kr_ref, ki_ref, *, n2, n):
    C = yr_ref.shape[2]
    f2 = f_ref[...]
    scale = 1.0 / (l1_ref[...] * n)
    for kb in range(FFT_ROWS // n2):
        rws = slice(kb * n2, (kb + 1) * n2)
        tr = jnp.tile(tr_ref[kb], (1, C // LANES))
        ti = jnp.tile(ti_ref[kb], (1, C // LANES))
        gr, gi = _fft2_fwd(yr_ref[0, rws, :], yi_ref[0, rws, :], tr, ti, f2, n2, C)
        hr, hi = _fft2_fwd(yr_ref[1, rws, :], yi_ref[1, rws, :], tr, ti, f2, n2, C)
        kr_ref[rws, :] = (gr + hr) * scale
        ki_ref[rws, :] = (gi - hi) * scale


def _fft2_conv_kernel(yr_ref, yi_ref, tr_ref, ti_ref, f_ref, kr_ref, ki_ref, zr_ref, zi_ref, *, n2):
    C = yr_ref.shape[2]
    f2 = f_ref[...]
    for kb in range(FFT_ROWS // n2):
        rws = slice(kb * n2, (kb + 1) * n2)
        tr = jnp.tile(tr_ref[kb], (1, C // LANES))
        ti = jnp.tile(ti_ref[kb], (1, C // LANES))
        xr, xi = _fft2_fwd(yr_ref[0, rws, :], yi_ref[0, rws, :], tr, ti, f2, n2, C)
        vr, vi = _cmul(xr, xi, kr_ref[rws, :], ki_ref[rws, :])
        q = _dot3(f2, jnp.concatenate([vr, vi], axis=1))
        wr = q[:n2, :C] + q[n2:, C:]
        wi = q[:n2, C:] - q[n2:, :C]
        zr, zi = _cmul(wr, wi, tr, -ti)
        zr_ref[0, rws, :] = zr
        zi_ref[0, rws, :] = zi


def _fft_tables(n1, n2):
    n = n1 * n2
    k1 = jnp.arange(n1, dtype=jnp.int32)[:, None]
    j2 = jnp.arange(n2, dtype=jnp.int32)[None, :]
    a = (-2.0 * math.pi / n) * (k1 * j2).astype(F32)
    tr = jnp.broadcast_to(jnp.cos(a)[:, :, None], (n1, n2, LANES))
    ti = jnp.broadcast_to(jnp.sin(a)[:, :, None], (n1, n2, LANES))
    fr, fi = _dft(n2)
    return tr, ti, _split3(np.concatenate([fr, fi], 0))


def _fft_stage2_filter(yr, yi, tables, l1, n1, n2):
    _, N, C = yr.shape
    tr, ti, f2 = tables
    kb = FFT_ROWS // n2
    blk = pl.BlockSpec((2, FFT_ROWS, C), lambda k: (0, k, 0))
    tblk = pl.BlockSpec((kb, n2, LANES), lambda k: (k, 0, 0))
    oblk = pl.BlockSpec((FFT_ROWS, C), lambda k: (k, 0))
    shp = jax.ShapeDtypeStruct((N, C), F32)
    return pl.pallas_call(
        functools.partial(_fft2_filter_kernel, n2=n2, n=N), grid=(N // FFT_ROWS,),
        in_specs=[blk, blk, tblk, tblk, pl.BlockSpec(f2.shape, lambda k: (0, 0)),
                  pl.BlockSpec((1, C), lambda k: (0, 0))],
        out_specs=[oblk, oblk], out_shape=[shp, shp],
        compiler_params=_cparams(("parallel",)), name="fft_stage2_filter")(yr, yi, tr, ti, f2, l1)


def _fft_stage2_conv(yr, yi, tables, spec_r, spec_i, n1, n2):
    B, N, C = yr.shape
    tr, ti, f2 = tables
    kb = FFT_ROWS // n2
    blk = pl.BlockSpec((1, FFT_ROWS, C), lambda b, k: (b, k, 0))
    tblk = pl.BlockSpec((kb, n2, LANES), lambda b, k: (k, 0, 0))
    sblk = pl.BlockSpec((FFT_ROWS, C), lambda b, k: (k, 0))
    shp = jax.ShapeDtypeStruct((B, N, C), F32)
    return pl.pallas_call(
        functools.partial(_fft2_conv_kernel, n2=n2), grid=(B, N // FFT_ROWS),
        in_specs=[blk, blk, tblk, tblk, pl.BlockSpec(f2.shape, lambda b, k: (0, 0)), sblk, sblk],
        out_specs=[blk, blk], out_shape=[shp, shp],
        compiler_params=_cparams(("parallel", "parallel")), name="fft_stage2_conv")(
            yr, yi, tr, ti, f2, spec_r, spec_i)


def _fft3_kernel(zr_ref, zi_ref, f_ref, x0_ref, u_ref, bias_ref, o_ref):
    f3 = f_ref[...]
    bias = bias_ref[...]
    for g in range(FFT_G):
        conv = _dot3(f3, jnp.concatenate([zr_ref[0, :, g, :], zi_ref[0, :, g, :]], axis=0))
        u = u_ref[0, :, g, :]
        o_ref[0, :, g, :] = x0_ref[0, :, g, :] * (conv + u * bias)


def _fft_stage3(zr, zi, x0, u, bias, n1, n2):
    B, L, C = u.shape
    fr, fi = _dft(n1)
    f3 = _split3(np.concatenate([fr[:n1 // 2], fi[:n1 // 2]], 1))
    G = FFT_G
    zblk = pl.BlockSpec((1, n1, G, C), lambda b, j: (b, 0, j, 0))
    ublk = pl.BlockSpec((1, n1 // 2, G, C), lambda b, j: (b, 0, j, 0))
    v4 = lambda a: a.reshape(B, -1, n2, C)
    out = pl.pallas_call(
        _fft3_kernel, grid=(B, n2 // G),
        in_specs=[zblk, zblk, pl.BlockSpec(f3.shape, lambda b, j: (0, 0)), ublk, ublk,
                  pl.BlockSpec((1, C), lambda b, j: (0, 0))],
        out_specs=ublk, out_shape=jax.ShapeDtypeStruct((B, n1 // 2, n2, C), F32),
        compiler_params=_cparams(("parallel", "parallel")), name="fft_stage3")(
            v4(zr), v4(zi), f3, v4(x0), v4(u), bias)
    return out.reshape(B * L, C)


def _hyena(proj, conv_w, conv_b, filt, hy_bias, row0, B, L):
    n1, n2 = _fft_factors(2 * L)
    x0, u = _short_conv(proj, conv_w, conv_b, row0, B, L)
    x0 = x0.reshape(B, L, HY_WIDTH)
    u = u.reshape(B, L, HY_WIDTH)
    tables = _fft_tables(n1, n2)
    hfb, l1 = _hyena_filter(L, *filt)
    fr, fi = _fft_stage1(hfb, n1, n2)
    sr, si = _fft_stage2_filter(fr, fi, tables, l1, n1, n2)
    yr, yi = _fft_stage1(u, n1, n2)
    zr, zi = _fft_stage2_conv(yr, yi, tables, sr, si, n1, n2)
    return _fft_stage3(zr, zi, x0, u, hy_bias, n1, n2)


def _merge_kernel(x_ref, g_ref, wg_ref, ya_ref, yb_ref, yc_ref, wa_ref, wb_ref, wc_ref, wo_ref, o_ref):
    x = x_ref[...]
    xn = _rms(x, g_ref[...]).astype(BF16)
    merged = None
    for j, (y_ref, w_ref) in enumerate(((ya_ref, wa_ref), (yb_ref, wb_ref), (yc_ref, wc_ref))):
        gate = jnp.dot(xn, wg_ref[:, j * D_MODEL:(j + 1) * D_MODEL], preferred_element_type=F32)
        br = jnp.dot(y_ref[...].astype(BF16), w_ref[...], preferred_element_type=F32)
        term = jax.nn.sigmoid(gate) * br
        merged = term if merged is None else merged + term
    o_ref[...] = x + jnp.dot(merged.astype(BF16), wo_ref[...], preferred_element_type=F32)


def _merge(x, g, wg, ya, yb, yc, wa, wb, wc, wo):
    T = x.shape[0]
    tm = min(512, T)
    xspec = pl.BlockSpec((tm, D_MODEL), lambda i: (i, 0))
    yspec = pl.BlockSpec((tm, BRANCH_W), lambda i: (i, 0))
    const = lambda shape: pl.BlockSpec(shape, lambda i: (0,) * len(shape))
    return pl.pallas_call(
        _merge_kernel, grid=(T // tm,),
        in_specs=[xspec, const((1, D_MODEL)), const((D_MODEL, 3 * D_MODEL)), yspec, yspec, yspec,
                  const((BRANCH_W, D_MODEL)), const((BRANCH_W, D_MODEL)), const((BRANCH_W, D_MODEL)),
                  const((D_MODEL, D_MODEL))],
        out_specs=xspec, out_shape=jax.ShapeDtypeStruct((T, D_MODEL), F32),
        compiler_params=_cparams(("parallel",)), name="merge_out_projection")(
            x, g, wg, ya, yb, yc, wa, wb, wc, wo)


def _router_kernel(x_ref, g_ref, w_ref, b_ref, xn_ref, r_ref):
    xn = _rms(x_ref[...], g_ref[...])
    xn_ref[...] = xn
    lg = jnp.dot(xn, w_ref[...], precision=HIGHEST, preferred_element_type=F32) + b_ref[...]
    lane = lax.broadcasted_iota(jnp.int32, lg.shape, 1).astype(F32)
    neg = -jnp.inf
    is_g = lane < N_GROUPS
    gl = jnp.where(is_g, lg, neg)
    gmax = jnp.max(gl, axis=1, keepdims=True)
    gsel = jnp.min(jnp.where(gl == gmax, lane, float(LANES)), axis=1, keepdims=True)
    gprob = 1.0 / jnp.sum(jnp.where(is_g, jnp.exp(lg - gmax), 0.0), axis=1, keepdims=True)
    lo = N_GROUPS + gsel * EXPERTS_PER_GROUP
    el = jnp.where((lane >= lo) & (lane < lo + EXPERTS_PER_GROUP), lg, neg)
    v1 = jnp.max(el, axis=1, keepdims=True)
    i1 = jnp.min(jnp.where(el == v1, lane, float(LANES)), axis=1, keepdims=True)
    el2 = jnp.where(lane == i1, neg, el)
    v2 = jnp.max(el2, axis=1, keepdims=True)
    i2 = jnp.min(jnp.where(el2 == v2, lane, float(LANES)), axis=1, keepdims=True)
    e = jnp.exp(v2 - v1)
    w1 = gprob / (1.0 + e)
    w2 = w1 * e
    r_ref[...] = jnp.where(lane == 0, i1 - N_GROUPS,
                           jnp.where(lane == 1, i2 - N_GROUPS,
                                     jnp.where(lane == 2, w1, jnp.where(lane == 3, w2, 0.0))))


def _router(x, g, w, b):
    T = x.shape[0]
    tm = min(512, T)
    xspec = pl.BlockSpec((tm, D_MODEL), lambda i: (i, 0))
    const = lambda shape: pl.BlockSpec(shape, lambda i: (0,) * len(shape))
    return pl.pallas_call(
        _router_kernel, grid=(T // tm,),
        in_specs=[xspec, const((1, D_MODEL)), const((D_MODEL, LANES)), const((1, LANES))],
        out_specs=[xspec, pl.BlockSpec((tm, LANES), lambda i: (i, 0))],
        out_shape=[jax.ShapeDtypeStruct((T, D_MODEL), F32), jax.ShapeDtypeStruct((T, LANES), F32)],
        compiler_params=_cparams(("parallel",)), name="router")(x, g, w, b)


MOE_ROWS = 512


def _moe_kernel(be_ref, idx_hbm, xn_hbm, wg_ref, wu_ref, wd_ref, out_hbm, idx, xbuf, obuf, isem, gsem, ssem):
    R = MOE_ROWS
    i = pl.program_id(0)
    n = pl.num_programs(0)
    s = i % 2

    def idx_copy(blk, slot):
        return pltpu.make_async_copy(idx_hbm.at[blk], idx.at[slot], isem.at[slot])

    def issue_gathers(slot):
        for r in range(R):
            pltpu.make_async_copy(xn_hbm.at[pl.ds(idx[slot, r], 1)], xbuf.at[slot, pl.ds(r, 1)],
                                  gsem.at[slot]).start()

    def wait_gathers(slot):
        pltpu.make_async_copy(xn_hbm.at[pl.ds(0, R)], xbuf.at[slot], gsem.at[slot]).wait()

    def issue_scatters(slot):
        for r in range(R):
            pltpu.make_async_copy(obuf.at[slot, pl.ds(r, 1)], out_hbm.at[pl.ds(idx[slot, R + r], 1)],
                                  ssem.at[slot]).start()

    def wait_scatters(slot):
        pltpu.make_async_copy(obuf.at[slot], out_hbm.at[pl.ds(0, R)], ssem.at[slot]).wait()

    @pl.when(i == 0)
    def _():
        c = idx_copy(0, 0)
        c.start()
        c.wait()
        issue_gathers(0)

        @pl.when(n > 1)
        def _():
            idx_copy(1, 1).start()

    wait_gathers(s)

    @pl.when(i + 1 < n)
    def _():
        idx_copy(i + 1, 1 - s).wait()
        issue_gathers(1 - s)

    @pl.when(i >= 2)
    def _():
        wait_scatters(s)

    xb = xbuf[s].astype(BF16)
    h = jax.nn.silu(jnp.dot(xb, wg_ref[0], preferred_element_type=F32)) \
        * jnp.dot(xb, wu_ref[0], preferred_element_type=F32)
    obuf[s] = jnp.dot(h.astype(BF16), wd_ref[0], preferred_element_type=F32)
    issue_scatters(s)

    @pl.when(i + 2 < n)
    def _():
        idx_copy(i + 2, s).start()

    @pl.when(i == n - 1)
    def _():
        wait_scatters(s)

        @pl.when(n > 1)
        def _():
            wait_scatters(1 - s)


def _moe_experts(xn, idx, block_e, wg, wu, wd):
    T = xn.shape[0]
    R = MOE_ROWS
    n_blocks = idx.shape[0]
    grid_spec = pltpu.PrefetchScalarGridSpec(
        num_scalar_prefetch=1, grid=(n_blocks,),
        in_specs=[pl.BlockSpec(memory_space=pl.ANY), pl.BlockSpec(memory_space=pl.ANY),
                  pl.BlockSpec((1, D_MODEL, EXPERT_HIDDEN), lambda i, be: (be[i], 0, 0)),
                  pl.BlockSpec((1, D_MODEL, EXPERT_HIDDEN), lambda i, be: (be[i], 0, 0)),
                  pl.BlockSpec((1, EXPERT_HIDDEN, D_MODEL), lambda i, be: (be[i], 0, 0))],
        out_specs=pl.BlockSpec(memory_space=pl.ANY),
        scratch_shapes=[pltpu.SMEM((2, 2 * R), jnp.int32), pltpu.VMEM((2, R, D_MODEL), F32),
                        pltpu.VMEM((2, R, D_MODEL), F32), pltpu.SemaphoreType.DMA((2,)),
                        pltpu.SemaphoreType.DMA((2,)), pltpu.SemaphoreType.DMA((2,))])
    return pl.pallas_call(
        _moe_kernel, grid_spec=grid_spec, out_shape=jax.ShapeDtypeStruct((TOP_K * T + 2 * R, D_MODEL), F32),
        compiler_params=_cparams(("arbitrary",)), name="moe_experts")(block_e, idx, xn, wg, wu, wd)


def _dispatch(route, T):
    R = MOE_ROWS
    M = T * TOP_K
    expert_id = route[:, :TOP_K].astype(jnp.int32).reshape(-1)
    order = jnp.argsort(expert_id).astype(jnp.int32)
    counts =jnp.bincount(expert_id, length=N_EXPERTS).astype(jnp.int32)
    starts = jnp.cumsum(counts) - counts
    padded = (counts + R - 1) // R * R
    p_ends = jnp.cumsum(padded)
    p_starts = p_ends - padded
    n_blocks = -(-M // R) + N_EXPERTS
    block_e = jnp.minimum(jnp.searchsorted(p_ends, jnp.arange(n_blocks, dtype=jnp.int32) * R, side='right'),
                          N_EXPERTS - 1).astype(jnp.int32)
    row = jnp.arange(n_blocks * R, dtype=jnp.int32).reshape(n_blocks, R)
    k = row - p_starts[block_e][:, None]
    valid = (k < counts[block_e][:, None]) & (row < p_ends[N_EXPERTS - 1])
    a = order[jnp.clip(starts[block_e][:, None] + k, 0, M - 1)]
    token = a >> 1
    src = jnp.where(valid, token, 0)
    scrap = M + row % (2 * R)
    dst = jnp.where(valid, (a & 1) * T + token, scrap)
    return jnp.concatenate([src, dst], axis=1), block_e


def _final_kernel(x_ref, e0_ref, e1_ref, r_ref, g_ref, o_ref):
    r = r_ref[...]
    x = x_ref[...] + e0_ref[...] * r[:, 2:3] + e1_ref[...] * r[:, 3:4]
    o_ref[...] = _rms(x, g_ref[...])


def _final(x, e2, route, g, row0, rows):
    T = x.shape[0]
    tm = min(512, rows)
    nT = T // tm
    b0 = row0 // tm
    xspec = pl.BlockSpec((tm, D_MODEL), lambda i: (b0 + i, 0))
    return pl.pallas_call(
        _final_kernel, grid=(rows // tm,),
        in_specs=[xspec, xspec, pl.BlockSpec((tm, D_MODEL), lambda i: (nT + b0 + i, 0)),
                  pl.BlockSpec((tm, LANES), lambda i: (b0 + i, 0)), pl.BlockSpec((1, D_MODEL), lambda i: (0, 0))],
        out_specs=pl.BlockSpec((tm, D_MODEL), lambda i: (i, 0)),
        out_shape=jax.ShapeDtypeStruct((rows, D_MODEL), F32),
        compiler_params=_cparams(("parallel",)), name="combine_final_norm")(x, e2, e2, route, g)


def _pack_layer(l, p):
    w = p['w_in'][l]
    o_gate = 3 * HG_WIDTH + 2 * HG_WIDTH + 4 * ML_WIDTH
    o_hy = o_gate + 4 * ML_HEADS
    o_g = o_hy + 3 * HY_WIDTH
    gates = jnp.pad(w[:, o_gate:o_hy], ((0, 0), (0, LANES - 4 * ML_HEADS)))
    w_proj = jnp.concatenate([w[:, :o_gate], w[:, o_hy:o_g], gates], axis=1).astype(BF16)
    w_gate = w[:, o_g:].astype(BF16)
    gate_bias = jnp.pad(jnp.concatenate([p['ml_i_bias'][l].reshape(-1), p['ml_f_bias'][l].reshape(-1)]),
                        (0, LANES - 4 * ML_HEADS)).astype(F32)[None]
    w_router = jnp.pad(jnp.concatenate([p['router_group_w'][l], p['router_expert_w'][l]], axis=1),
                       ((0, 0), (0, LANES - N_GROUPS - N_EXPERTS))).astype(F32)
    b_router = jnp.pad(jnp.concatenate([p['router_group_b'][l], p['router_expert_b'][l]]),
                       (0, LANES - N_GROUPS - N_EXPERTS)).astype(F32)[None]
    return dict(
        norm_mix_g=p['norm_mix_g'][l].astype(F32)[None], w_proj=w_proj, w_gate=w_gate,
        hg_norm_g=p['hg_norm_g'][l].astype(F32)[None], gate_bias=gate_bias,
        ml_norm_g=p['ml_norm_g'][l].astype(F32)[None],
        conv_w=jnp.pad(p['hy_conv_w'][l].astype(F32), ((0, 5), (0, 0))), conv_b=p['hy_conv_b'][l].astype(F32)[None],
        filt=(p['filt_w1'][l], p['filt_b1'][l], p['filt_w2'][l], p['filt_b2'][l], p['filt_freq'][l],
              p['filt_w3'][l]),
        hy_bias=p['hy_bias'][l].astype(F32)[None],
        wa=p['w_branch_a'][l].astype(BF16), wb=p['w_branch_b'][l].astype(BF16),
        wc=p['w_branch_c'][l].astype(BF16), wo=p['w_out'][l].astype(BF16),
        norm_ffn_g=p['norm_ffn_g'][l].astype(F32)[None], w_router=w_router, b_router=b_router,
        wg=p['exp_w_gate'][l].astype(BF16), wu=p['exp_w_up'][l].astype(BF16),
        wd=p['exp_w_down'][l].astype(BF16))


def _trunk(x, groups, p):
    T = x.shape[0]
    lbs = jnp.cumsum(jax.nn.softmax(p['hg_lb'].astype(F32), axis=0), axis=0)
    lbs = lbs - lbs[0:1]
    moe = None
    for l in range(DEPTH):
        lp = _pack_layer(l, p)
        lb = lbs[l][None]
        lbrow = jnp.concatenate([jnp.log(lb), jnp.log1p(-lb), 1.0 - lb, jnp.zeros((5, HG_WIDTH), F32)], axis=0)
        if moe is None:
            proj = _in_projection(x, lp['norm_mix_g'], lp['w_proj'])
        else:
            proj, x = _in_projection(x, lp['norm_mix_g'], lp['w_proj'], moe)
        ya, yb, yc = [], [], []
        for row0, B, L in groups:
            ya.append(_hgrn2(proj, lbrow, lp['hg_norm_g'], row0, B, L))
            yb.append(_mlstm(proj, lp['gate_bias'], lp['ml_norm_g'], row0, B, L))
            yc.append(_hyena(proj, lp['conv_w'], lp['conv_b'], lp['filt'], lp['hy_bias'], row0, B, L))
        ya, yb, yc = (jnp.concatenate(v, axis=0) for v in (ya, yb, yc))
        x = _merge(x, lp['norm_mix_g'], lp['w_gate'], ya, yb, yc, lp['wa'], lp['wb'], lp['wc'], lp['wo'])
        xn, route = _router(x, lp['norm_ffn_g'], lp['w_router'], lp['b_router'])
        rows, block_e = _dispatch(route, T)
        e2 = _moe_experts(xn, rows, block_e, lp['wg'], lp['wu'], lp['wd'])
        moe = (e2, route)
    g = p['final_norm_g'].astype(F32)[None]
    return [_final(x, moe[0], moe[1], g, row0, B * L) for row0, B, L in groups]


def kernel(x_prompt, x_sample, norm_mix_g, w_in, hg_lb, hg_norm_g, ml_i_bias, ml_f_bias, ml_norm_g, hy_conv_w, hy_conv_b, filt_w1, filt_b1, filt_w2, filt_b2, filt_freq, filt_w3, hy_bias, w_branch_a, w_branch_b, w_branch_c, w_out, norm_ffn_g, router_group_w, router_group_b, router_expert_w, router_expert_b, exp_w_gate, exp_w_up, exp_w_down, final_norm_g):
    p = dict(norm_mix_g=norm_mix_g, w_in=w_in, hg_lb=hg_lb, hg_norm_g=hg_norm_g, ml_i_bias=ml_i_bias,
             ml_f_bias=ml_f_bias, ml_norm_g=ml_norm_g, hy_conv_w=hy_conv_w, hy_conv_b=hy_conv_b,
             filt_w1=filt_w1, filt_b1=filt_b1, filt_w2=filt_w2, filt_b2=filt_b2, filt_freq=filt_freq,
             filt_w3=filt_w3, hy_bias=hy_bias, w_branch_a=w_branch_a, w_branch_b=w_branch_b,
             w_branch_c=w_branch_c, w_out=w_out, norm_ffn_g=norm_ffn_g, router_group_w=router_group_w,
             router_group_b=router_group_b, router_expert_w=router_expert_w,
             router_expert_b=router_expert_b, exp_w_gate=exp_w_gate, exp_w_up=exp_w_up,
             exp_w_down=exp_w_down, final_norm_g=final_norm_g)
    Bp, Lp, _ = x_prompt.shape
    Bs, Ls, _ = x_sample.shape
    Tp, Ts = Bp * Lp, Bs * Ls
    x = jnp.concatenate([x_prompt.reshape(Tp, D_MODEL), x_sample.reshape(Ts, D_MODEL)], axis=0).astype(F32)
    yp, ys = _trunk(x, ((0, Bp, Lp), (Tp, Bs, Ls)), p)
    return (yp.reshape(Bp, Lp, D_MODEL), ys.reshape(Bs, Ls, D_MODEL))
```

```python
import functools
import math

import numpy as np
import jax
import jax.numpy as jnp
from jax import lax
from jax.experimental import pallas as pl
from jax.experimental.pallas import tpu as pltpu

F32 = jnp.float32
BF16 = jnp.bfloat16
HIGHEST = lax.Precision.HIGHEST

D_MODEL = 1024
EPS = 1e-6
DEPTH = 2
HG_HEADS, HG_KDIM, HG_WIDTH, HG_CHUNK = 8, 64, 512, 32
ML_HEADS, ML_HDIM, ML_WIDTH, ML_CHUNK = 4, 128, 512, 128
HY_WIDTH, HY_BANDS, HY_FILTER_HIDDEN = 512, 16, 64
HY_EMB = 1 + 2 * HY_BANDS
HY_SHORT_DECAY_PCT, HY_LONG_DECAY_PCT, HY_DECAY_TARGET = 0.3, 1.5, 1e-2
N_GROUPS, EXPERTS_PER_GROUP, TOP_K = 4, 8, 2
N_EXPERTS = N_GROUPS * EXPERTS_PER_GROUP
EXPERT_HIDDEN = D_MODEL // 2
MOE_BLOCK = 128

LANES = 128
BRANCH_W = 512
SLAB_HQ, SLAB_HFF, SLAB_HFB, SLAB_HI, SLAB_HG = 0, 1, 2, 3, 4
SLAB_MQ, SLAB_MK, SLAB_MV, SLAB_MO = 5, 6, 7, 8
SLAB_HY = 9
PROJ_MAIN = 12 * BRANCH_W
PROJ_W = PROJ_MAIN + LANES
GATE_BLOCK = PROJ_MAIN // LANES
VMEM_LIMIT = 48 * 1024 * 1024
VMEM_LIMIT_WIDE = 56 * 1024 * 1024


def _cparams(sem, vmem=VMEM_LIMIT):
    return pltpu.CompilerParams(dimension_semantics=sem, vmem_limit_bytes=vmem)


def _rms(x, g):
    return x * lax.rsqrt(jnp.mean(x * x, axis=-1, keepdims=True) + EPS) * g


def _log_sigmoid(z):
    return jnp.minimum(z, 0.0) - jnp.log1p(jnp.exp(-jnp.abs(z)))


def _bf16_terms(x):
    t1 = x.astype(BF16)
    r = x - t1.astype(F32)
    t2 = r.astype(BF16)
    return t1, t2, (r - t2.astype(F32)).astype(BF16)


def _inproj_kernel(x_ref, g_ref, w_ref, o_ref, xn_ref):
    @pl.when(pl.program_id(1) == 0)
    def _():
        xn_ref[...] = _rms(x_ref[...], g_ref[...]).astype(BF16)

    o_ref[...] = jnp.dot(xn_ref[...], w_ref[...], preferred_element_type=F32)


def _inproj_combine_kernel(x_ref, e0_ref, e1_ref, r_ref, g_ref, w_ref, o_ref, xo_ref, xn_ref):
    @pl.when(pl.program_id(1) == 0)
    def _():
        r = r_ref[...]
        x = x_ref[...] + e0_ref[...] * r[:, 2:3] + e1_ref[...] * r[:, 3:4]
        xo_ref[...] = x
        xn_ref[...] = _rms(x, g_ref[...]).astype(BF16)

    o_ref[...] = jnp.dot(xn_ref[...], w_ref[...], preferred_element_type=F32)


def _in_projection(x, g, w, moe=None):
    T = x.shape[0]
    tm = min(1024, T)
    tn = PROJ_W // 7
    grid = (T // tm, PROJ_W // tn)
    xspec = pl.BlockSpec((tm, D_MODEL), lambda i, j: (i, 0))
    gspec = pl.BlockSpec((1, D_MODEL), lambda i, j: (0, 0))
    wspec = pl.BlockSpec((D_MODEL, tn), lambda i, j: (0, j))
    ospec = pl.BlockSpec((tm, tn), lambda i, j: (i, j))
    scratch = [pltpu.VMEM((tm, D_MODEL), BF16)]
    cp = _cparams(("parallel", "arbitrary"))
    if moe is None:
        return pl.pallas_call(
            _inproj_kernel, grid=grid, in_specs=[xspec, gspec, wspec], out_specs=ospec,
            out_shape=jax.ShapeDtypeStruct((T, PROJ_W), F32), scratch_shapes=scratch,
            compiler_params=cp, name="in_projection")(x, g, w)
    e2, route = moe
    nT = T // tm
    e0spec = pl.BlockSpec((tm, D_MODEL), lambda i, j: (i, 0))
    e1spec = pl.BlockSpec((tm, D_MODEL), lambda i, j: (nT + i, 0))
    rspec = pl.BlockSpec((tm, LANES), lambda i, j: (i, 0))
    proj, xo = pl.pallas_call(
        _inproj_combine_kernel, grid=grid, in_specs=[xspec, e0spec, e1spec, rspec, gspec, wspec],
        out_specs=[ospec, xspec],
        out_shape=[jax.ShapeDtypeStruct((T, PROJ_W), F32), jax.ShapeDtypeStruct((T, D_MODEL), F32)],
        scratch_shapes=scratch, compiler_params=_cparams(("parallel", "arbitrary"), VMEM_LIMIT_WIDE),
        name="in_projection_combine")(x, e2, e2, route, g, w)
    return proj, xo


HG_TB = 256
HG_PAIRS = HG_WIDTH // LANES


HG_UNROLL = 8
HG_SUB = 8
HG_NSUB = HG_CHUNK // HG_SUB
HG_XROWS = (HG_NSUB - 1) * HG_HEADS * HG_SUB
HG_XCOLS = HG_SUB * HG_NSUB * (HG_NSUB - 1) // 2


def _hgrn2_tiles(reverse):
    if reverse:
        return [(i, (i + 1) * HG_SUB, HG_CHUNK - (i + 1) * HG_SUB) for i in range(HG_NSUB - 1)]
    return [(i, 0, i * HG_SUB) for i in range(1, HG_NSUB)]


def _hgrn2_kernel(*refs, reverse, finalize):
    if finalize:
        (q_ref, z_ref, v_ref, lb_ref, tri_ref, bd_ref, xm_ref, of_ref, gate_ref, ng_ref,
         o_ref, st_ref) = refs
    else:
        (q_ref, z_ref, v_ref, lb_ref, tri_ref, bd_ref, xm_ref, o_ref, st_ref) = refs
    C, c, nb, W = HG_CHUNK, HG_SUB, HG_NSUB, HG_WIDTH
    n_chunks = HG_TB // C

    @pl.when(pl.program_id(1) == 0)
    def _():
        st_ref[...] = jnp.zeros_like(st_ref)

    log_lb = lb_ref[0:1, :]
    log_1mlb = lb_ref[1:2, :]
    one_m_lb = lb_ref[2:3, :]
    tri3 = tri_ref[...]
    bd = bd_ref[...]
    bd_bf = bd.astype(BF16)
    xmask = xm_ref[...]
    sub_row = lax.broadcasted_iota(jnp.int32, (nb, c, W), 1)
    lane_head = lax.broadcasted_iota(jnp.int32, (c, W), 1) // HG_KDIM
    tiles = _hgrn2_tiles(reverse)

    def chunk(i):
        ci = (n_chunks - 1 - i) if reverse else i
        off = pl.multiple_of(ci * C, C)
        q = q_ref[pl.ds(off, C), :]
        z = z_ref[pl.ds(off, C), :]
        v = v_ref[pl.ds(off, C), :]
        ls = jnp.minimum(z, 0.0) - jnp.log(1.0 + jnp.exp(-jnp.abs(z)))
        hi = log_1mlb + ls
        mx = jnp.maximum(log_lb, hi)
        lf = mx + jnp.log(1.0 + jnp.exp(-jnp.abs(log_lb - hi)))
        kk = one_m_lb * jnp.exp(ls - z)
        b = jnp.dot(tri3, jnp.concatenate(_bf16_terms(lf), axis=0), preferred_element_type=F32)
        btot = b[0:1, :] if reverse else b[C - 1:C, :]
        qb = (q * jnp.exp(b)).astype(BF16)
        kb = (kk * jnp.exp(btot - b)).astype(BF16)
        dec = jnp.exp(btot)
        vb = v.astype(BF16)

        q3, k3, v3 = (a.reshape(nb, c, W) for a in (q, kk, v))
        one = 1 if not reverse else c - 1
        f_sh = (1.0 - kk).reshape(nb, c, W)
        k_sh = k3
        decay = None
        a_rows = [(q * kk).astype(BF16)]
        for d in range(1, c):
            ok = (sub_row < c - d) if reverse else (sub_row >= d)
            decay = f_sh if decay is None else decay * f_sh
            f_sh = pltpu.roll(f_sh, one, axis=1)
            k_sh = pltpu.roll(k_sh, one, axis=1)
            a = jnp.where(ok, q3 * k_sh * decay, 0.0)
            a_rows.append(a.reshape(C, W).astype(BF16))
        a_all = jnp.concatenate(a_rows, axis=0)
        sums = jnp.concatenate(
            [jnp.dot(a_all[:, p * LANES:(p + 1) * LANES], bd_bf, preferred_element_type=F32)
             for p in range(HG_PAIRS)], axis=1)
        acc3 = sums[0:C].reshape(nb, c, W) * v3
        v_sh = v3
        for d in range(1, c):
            v_sh = pltpu.roll(v_sh, one, axis=1)
            acc3 = acc3 + sums[d * C:(d + 1) * C].reshape(nb, c, W) * v_sh
        acc = acc3.reshape(C, W)

        qx, kx, vx = [], [], []
        for (ti, s0, sn) in tiles:
            edge = s0 if reverse else s0 + sn - 1
            r = b[edge:edge + 1, :]
            rows = slice(ti * c, (ti + 1) * c)
            qh = q[rows] * jnp.exp(b[rows] - r)
            qx += [jnp.where(lane_head == h, qh, 0.0) for h in range(HG_HEADS)]
            kx.append(kk[s0:s0 + sn] * jnp.exp(r - b[s0:s0 + sn]))
            vx.append(v[s0:s0 + sn])
        qx = jnp.concatenate(qx, axis=0).astype(BF16)
        kx = jnp.concatenate(kx, axis=0).astype(BF16)
        vx = jnp.concatenate(vx, axis=0).astype(BF16)
        sc = lax.dot_general(qx, kx, (((1,), (1,)), ((), ())), preferred_element_type=F32) * xmask
        px = jnp.dot(sc.astype(BF16), vx, preferred_element_type=F32)
        offd = {}
        for n, (ti, s0, sn) in enumerate(tiles):
            base = n * HG_HEADS * c
            t_acc = jnp.where(lane_head == 0, px[base:base + c], 0.0)
            for h in range(1, HG_HEADS):
                t_acc = t_acc + jnp.where(lane_head == h, px[base + h * c:base + (h + 1) * c], 0.0)
            offd[ti] = t_acc
        acc = acc + jnp.concatenate([offd.get(ti, jnp.zeros((c, W), F32)) for ti in range(nb)], axis=0)

        parts = []
        for p in range(HG_PAIRS):
            sl = slice(p * LANES, (p + 1) * LANES)
            s_t = st_ref[p]
            parts.append(lax.dot_general(qb[:, sl], s_t.astype(BF16), (((1,), (1,)), ((), ())),
                                         preferred_element_type=F32))
            upd = lax.dot_general(vb[:, sl], kb[:, sl], (((0,), (0,)), ((), ())),
                                  preferred_element_type=F32)
            st_ref[p] = s_t * dec[:, sl] + upd * bd
        acc = acc + jnp.concatenate(parts, axis=1)

        if finalize:
            o = acc + of_ref[pl.ds(off, C), :]
            o2 = (o * o).astype(BF16)
            ms = jnp.concatenate(
                [jnp.dot(o2[:, p * LANES:(p + 1) * LANES], bd_bf, preferred_element_type=F32)
                 for p in range(HG_PAIRS)], axis=1) * (1.0 / HG_KDIM)
            g = gate_ref[pl.ds(off, C), :]
            y = o * lax.rsqrt(ms + EPS) * ng_ref[...] * (g * jax.nn.sigmoid(g))
            o_ref[pl.ds(off, C), :] = y.astype(o_ref.dtype)
        else:
            o_ref[pl.ds(off, C), :] = acc

    def chunk_group(j, carry):
        for u in range(HG_UNROLL):
            chunk(HG_UNROLL * j + u)
        return carry

    lax.fori_loop(0, n_chunks // HG_UNROLL, chunk_group, 0)


def _hgrn2(proj, lbrow, norm_g, row0, B, L):
    C = HG_CHUNK
    nT = L // HG_TB
    blk0 = row0 // HG_TB
    tri_f = jnp.asarray(np.tile(np.tril(np.ones((C, C), np.float32)), (1, 3))).astype(BF16)
    tri_b = jnp.asarray(np.tile(np.triu(np.ones((C, C), np.float32)), (1, 3))).astype(BF16)
    head = np.arange(LANES) // HG_KDIM
    bd = jnp.asarray((head[:, None] == head[None, :]).astype(np.float32))

    def tile_mask(reverse):
        m = np.zeros((HG_XROWS, HG_XCOLS), np.float32)
        col = 0
        for n, (_, _, sn) in enumerate(_hgrn2_tiles(reverse)):
            m[n * HG_HEADS * HG_SUB:(n + 1) * HG_HEADS * HG_SUB, col:col + sn] = 1.0
            col += sn
        return jnp.asarray(m)

    def in_spec(slab, reverse):
        if reverse:
            return pl.BlockSpec((HG_TB, BRANCH_W), lambda b, t: (blk0 + b * nT + nT - 1 - t, slab))
        return pl.BlockSpec((HG_TB, BRANCH_W), lambda b, t: (blk0 + b * nT + t, slab))

    def out_spec(reverse):
        if reverse:
            return pl.BlockSpec((HG_TB, BRANCH_W), lambda b, t: (b * nT + nT - 1 - t, 0))
        return pl.BlockSpec((HG_TB, BRANCH_W), lambda b, t: (b * nT + t, 0))

    const = lambda shape: pl.BlockSpec(shape, lambda b, t: (0,) * len(shape))
    scratch = [pltpu.VMEM((HG_PAIRS, LANES, LANES), F32)]
    consts = [const((8, BRANCH_W)), const((C, 3 * C)), const((LANES, LANES)), const((HG_XROWS, HG_XCOLS))]
    cp = _cparams(("parallel", "arbitrary"))
    o_f = pl.pallas_call(
        functools.partial(_hgrn2_kernel, reverse=False, finalize=False), grid=(B, nT),
        in_specs=[in_spec(SLAB_HQ, False), in_spec(SLAB_HFF, False), in_spec(SLAB_HI, False)] + consts,
        out_specs=out_spec(False), out_shape=jax.ShapeDtypeStruct((B * L, BRANCH_W), F32),
        scratch_shapes=scratch, compiler_params=cp, name="hgrn2_fwd")(
            proj, proj, proj, lbrow, tri_f, bd, tile_mask(False))
    return pl.pallas_call(
        functools.partial(_hgrn2_kernel, reverse=True, finalize=True), grid=(B, nT),
        in_specs=[in_spec(SLAB_HQ, True), in_spec(SLAB_HFB, True), in_spec(SLAB_HI, True)] + consts
        + [out_spec(True), in_spec(SLAB_HG, True), const((1, BRANCH_W))],
        out_specs=out_spec(True), out_shape=jax.ShapeDtypeStruct((B * L, BRANCH_W), BF16),
        scratch_shapes=scratch, compiler_params=cp, name="hgrn2_bwd")(
            proj, proj, proj, lbrow, tri_b, bd, tile_mask(True), o_f, proj, norm_g)


def _mlstm_kernel(*refs, reverse, finalize):
    if finalize:
        (q_ref, k_ref, v_ref, gt_ref, gb_ref, tri_ref, hf_ref, og_ref, ng_ref,
         o_ref, c_ref, n_ref, m_ref) = refs
    else:
        (q_ref, k_ref, v_ref, gt_ref, gb_ref, tri_ref, o_ref, c_ref, n_ref, m_ref) = refs
    C = ML_CHUNK

    @pl.when(pl.program_id(1) == 0)
    def _():
        c_ref[...] = jnp.zeros_like(c_ref)
        n_ref[...] = jnp.zeros_like(n_ref)
        m_ref[...] = jnp.zeros_like(m_ref)

    tri = tri_ref[...]
    lane = lax.broadcasted_iota(jnp.int32, (C, LANES), 1)
    ti = lax.broadcasted_iota(jnp.int32, (C, C), 0)
    si = lax.broadcasted_iota(jnp.int32, (C, C), 1)
    causal = (si >= ti) if reverse else (si <= ti)
    dsel = ML_HEADS if reverse else 0
    edge = 0 if reverse else C - 1
    scale = 1.0 / math.sqrt(ML_HDIM)
    order = range(ML_TB // C - 1, -1, -1) if reverse else range(ML_TB // C)
    for ci in order:
        _mlstm_chunk(refs, slice(ci * C, (ci + 1) * C), tri, lane, causal, dsel, edge, scale, finalize)


def _mlstm_chunk(refs, rws, tri, lane, causal, dsel, edge, scale, finalize):
    if finalize:
        (q_ref, k_ref, v_ref, gt_ref, gb_ref, tri_ref, hf_ref, og_ref, ng_ref,
         o_ref, c_ref, n_ref, m_ref) = refs
    else:
        (q_ref, k_ref, v_ref, gt_ref, gb_ref, tri_ref, o_ref, c_ref, n_ref, m_ref) = refs
    gates = gt_ref[rws, :] + gb_ref[...]
    g_col = jnp.where(lane >= 2 * ML_HEADS, _log_sigmoid(gates), gates)
    g_row = g_col.T
    b_col = jnp.dot(tri, jnp.concatenate(_bf16_terms(g_col), axis=0), preferred_element_type=F32)
    b_row = lax.dot_general(jnp.concatenate(_bf16_terms(g_row), axis=1), tri, (((1,), (1,)), ((), ())),
                            preferred_element_type=F32)

    outs = []
    for h in range(ML_HEADS):
        sl = slice(h * ML_HDIM, (h + 1) * ML_HDIM)
        li, lf = dsel + h, 2 * ML_HEADS + dsel + h
        q = q_ref[rws, sl]
        kc = k_ref[rws, sl] * scale
        v = v_ref[rws, sl]
        qb, kb, vb = q.astype(BF16), kc.astype(BF16), v.astype(BF16)
        bc = b_col[:, lf:lf + 1]
        br = b_row[lf:lf + 1, :]
        ic = g_col[:, li:li + 1]
        ir = g_row[li:li + 1, :]
        btot = b_col[edge:edge + 1, lf:lf + 1]
        m_prev = m_ref[0:1, h:h + 1]
        n_prev = n_ref[h:h + 1, :]
        c_prev = c_ref[h]

        log_d = jnp.where(causal, bc - br + ir, -jnp.inf)
        log_inter = bc + m_prev
        m_t = jnp.maximum(log_inter, jnp.max(log_d, axis=1, keepdims=True))
        a_inter = jnp.exp(log_inter - m_t)
        s = lax.dot_general(qb, kb, (((1,), (1,)), ((), ())), preferred_element_type=F32)
        s = s * jnp.exp(log_d - m_t)
        num = a_inter * jnp.dot(qb, c_prev.astype(BF16), preferred_element_type=F32) \
            + jnp.dot(s.astype(BF16), vb, preferred_element_type=F32)
        den = a_inter * jnp.sum(q * n_prev, axis=1, keepdims=True) + jnp.sum(s, axis=1, keepdims=True)
        outs.append(num / jnp.maximum(jnp.abs(den), jnp.exp(-m_t)))

        log_w = btot - bc + ic
        m_new = jnp.maximum(btot + m_prev, jnp.max(log_w, axis=0, keepdims=True))
        a_state = jnp.exp(btot + m_prev - m_new)
        kw = kc * jnp.exp(log_w - m_new)
        c_ref[h] = a_state * c_prev + lax.dot_general(kw.astype(BF16), vb, (((0,), (0,)), ((), ())),
                                                      preferred_element_type=F32)
        n_ref[h:h + 1, :] = a_state * n_prev + jnp.sum(kw, axis=0, keepdims=True)
        m_ref[0:1, h:h + 1] = m_new

    hcur = jnp.concatenate(outs, axis=1)
    if finalize:
        hsum = hcur + hf_ref[rws, :]
        ys = []
        for h in range(ML_HEADS):
            sl = slice(h * ML_HDIM, (h + 1) * ML_HDIM)
            hh = hsum[:, sl]
            ys.append(hh * lax.rsqrt(jnp.mean(hh * hh, axis=1, keepdims=True) + EPS))
        y = jnp.concatenate(ys, axis=1) * ng_ref[...] * jax.nn.sigmoid(og_ref[rws, :])
        o_ref[rws, :] = y.astype(o_ref.dtype)
    else:
        o_ref[rws, :] = hcur


ML_TB = 2 * ML_CHUNK


def _mlstm(proj, gate_bias, norm_g, row0, B, L):
    C = ML_TB
    nT = L // C
    blk0 = row0 // C
    tri_f = jnp.asarray(np.tile(np.tril(np.ones((ML_CHUNK, ML_CHUNK), np.float32)), (1, 3))).astype(BF16)
    tri_b = jnp.asarray(np.tile(np.triu(np.ones((ML_CHUNK, ML_CHUNK), np.float32)), (1, 3))).astype(BF16)

    def rows(reverse):
        if reverse:
            return lambda b, t: blk0 + b * nT + nT - 1 - t
        return lambda b, t: blk0 + b * nT + t

    def in_spec(slab, reverse):
        r = rows(reverse)
        return pl.BlockSpec((C, BRANCH_W), lambda b, t: (r(b, t), slab))

    def gate_spec(reverse):
        r = rows(reverse)
        return pl.BlockSpec((C, LANES), lambda b, t: (r(b, t), GATE_BLOCK))

    def out_spec(reverse):
        if reverse:
            return pl.BlockSpec((C, BRANCH_W), lambda b, t: (b * nT + nT - 1 - t, 0))
        return pl.BlockSpec((C, BRANCH_W), lambda b, t: (b * nT + t, 0))

    const = lambda shape: pl.BlockSpec(shape, lambda b, t: (0,) * len(shape))
    scratch = [pltpu.VMEM((ML_HEADS, ML_HDIM, ML_HDIM), F32), pltpu.VMEM((8, ML_HDIM), F32),
               pltpu.VMEM((8, LANES), F32)]
    cp = _cparams(("parallel", "arbitrary"))
    h_f = pl.pallas_call(
        functools.partial(_mlstm_kernel, reverse=False, finalize=False), grid=(B, nT),
        in_specs=[in_spec(SLAB_MQ, False), in_spec(SLAB_MK, False), in_spec(SLAB_MV, False),
                  gate_spec(False), const((1, LANES)), const((ML_CHUNK, 3 * ML_CHUNK))],
        out_specs=out_spec(False), out_shape=jax.ShapeDtypeStruct((B * L, BRANCH_W), F32),
        scratch_shapes=scratch, compiler_params=cp, name="mlstm_fwd")(
            proj, proj, proj, proj, gate_bias, tri_f)
    return pl.pallas_call(
        functools.partial(_mlstm_kernel, reverse=True, finalize=True), grid=(B, nT),
        in_specs=[in_spec(SLAB_MQ, True), in_spec(SLAB_MK, True), in_spec(SLAB_MV, True),
                  gate_spec(True), const((1, LANES)), const((ML_CHUNK, 3 * ML_CHUNK)),
                  out_spec(True), in_spec(SLAB_MO, True), const((1, BRANCH_W))],
        out_specs=out_spec(True), out_shape=jax.ShapeDtypeStruct((B * L, BRANCH_W), BF16),
        scratch_shapes=scratch, compiler_params=cp, name="mlstm_bwd")(
            proj, proj, proj, proj, gate_bias, tri_b, h_f, proj, norm_g)


HY_TB = 256


def _shortconv_kernel(c_ref, p_ref, n_ref, w_ref, b_ref, x0_ref, u_ref, *, nT):
    t = pl.program_id(1)
    cur = c_ref[...]
    prev_row = jnp.where(t > 0, p_ref[7:8, :], 0.0)
    next_row = jnp.where(t < nT - 1, n_ref[0:1, :], 0.0)
    row = lax.broadcasted_iota(jnp.int32, cur.shape, 0)
    up = jnp.where(row == 0, prev_row, pltpu.roll(cur, 1, axis=0))
    dn = jnp.where(row == HY_TB - 1, next_row, pltpu.roll(cur, HY_TB - 1, axis=0))
    y = up * w_ref[0:1, :] + cur * w_ref[1:2, :] + dn * w_ref[2:3, :] + b_ref[...]
    x0_ref[...] = y[:, :HY_WIDTH]
    u_ref[...] = y[:, HY_WIDTH:2 * HY_WIDTH] * y[:, 2 * HY_WIDTH:]


def _short_conv(proj, w, b, row0, B, L):
    nT = L // HY_TB
    blk0 = row0 // HY_TB
    sub = HY_TB // 8
    W3 = 3 * HY_WIDTH
    slab = SLAB_HY * BRANCH_W // W3
    cur = pl.BlockSpec((HY_TB, W3), lambda bb, t: (blk0 + bb * nT + t, slab))
    prv = pl.BlockSpec((8, W3), lambda bb, t: (jnp.maximum((blk0 + bb * nT + t) * sub - 1, 0), slab))
    nxt = pl.BlockSpec((8, W3), lambda bb, t: (jnp.minimum((blk0 + bb * nT + t + 1) * sub,
                                                           (blk0 + B * nT) * sub - 1), slab))
    const = lambda shape: pl.BlockSpec(shape, lambda bb, t: (0,) * len(shape))
    out = pl.BlockSpec((HY_TB, HY_WIDTH), lambda bb, t: (bb * nT + t, 0))
    return pl.pallas_call(
        functools.partial(_shortconv_kernel, nT=nT), grid=(B, nT),
        in_specs=[cur, prv, nxt, const((8, W3)), const((1, W3))], out_specs=[out, out],
        out_shape=[jax.ShapeDtypeStruct((B * L, HY_WIDTH), F32)] * 2,
        compiler_params=_cparams(("parallel", "parallel")), name="hyena_short_conv")(proj, proj, proj, w, b)


HYF_TB = 256


def _filter_kernel(band_ref, w1_ref, b1_ref, w2_ref, b2_ref, fr_ref, w3_ref, dl_ref, h_ref, l1_ref, *, L):
    i = pl.program_id(0)
    pos = (lax.broadcasted_iota(jnp.int32, (HYF_TB, LANES), 0) + i * HYF_TB).astype(F32)
    lane = lax.broadcasted_iota(jnp.int32, (HYF_TB, LANES), 1)
    t = pos / (L - 1)
    ang = (2.0 * math.pi * pos / L) * band_ref[...]
    z = jnp.where(lane == 0, t,
                  jnp.where(lane <= HY_BANDS, jnp.cos(ang),
                            jnp.where(lane <= 2 * HY_BANDS, -jnp.sin(ang), 0.0)))
    fr = fr_ref[...]
    h = jnp.sin(fr * (jnp.dot(z, w1_ref[...], precision=HIGHEST, preferred_element_type=F32) + b1_ref[...]))
    h = jnp.sin(fr * (jnp.dot(h, w2_ref[...], precision=HIGHEST, preferred_element_type=F32) + b2_ref[...]))
    h = jnp.dot(h, w3_ref[...], precision=HIGHEST, preferred_element_type=F32)
    tt = (lax.broadcasted_iota(jnp.int32, (HYF_TB, HY_WIDTH), 0) + i * HYF_TB).astype(F32) / (L - 1)
    window = jnp.exp(-tt * dl_ref[...])
    rowi = lax.broadcasted_iota(jnp.int32, (HYF_TB, HY_WIDTH), 0) + i * HYF_TB
    hf = h[:, :HY_WIDTH] * window
    hb = jnp.where(rowi == 0, 0.0, h[:, HY_WIDTH:] * window)
    h_ref[0] = hf
    h_ref[1] = hb

    @pl.when(i == 0)
    def _():
        l1_ref[...] = jnp.zeros_like(l1_ref)

    l1_ref[...] += jnp.sum(jnp.abs(hf) + jnp.abs(hb), axis=0, keepdims=True)


def _hyena_filter(L, w1, b1, w2, b2, freq, w3):
    band = np.zeros((1, LANES), np.float32)
    bands = np.linspace(1e-4, HY_BANDS - 1, HY_BANDS, dtype=np.float32)
    band[0, 1:1 + HY_BANDS] = bands
    band[0, 1 + HY_BANDS:1 + 2 * HY_BANDS] = bands
    max_decay = math.log(HY_DECAY_TARGET) / HY_SHORT_DECAY_PCT
    min_decay = math.log(HY_DECAY_TARGET) / HY_LONG_DECAY_PCT
    deltas = np.abs(np.linspace(min_decay, max_decay, HY_WIDTH, dtype=np.float32))[None, :]
    w1p = jnp.zeros((LANES, HY_FILTER_HIDDEN), F32).at[:HY_EMB].set(w1.astype(F32))
    const = lambda shape: pl.BlockSpec(shape, lambda i: (0,) * len(shape))
    H = HY_FILTER_HIDDEN
    return pl.pallas_call(
        functools.partial(_filter_kernel, L=L), grid=(L // HYF_TB,),
        in_specs=[const((1, LANES)), const((LANES, H)), const((1, H)), const((H, H)), const((1, H)),
                  const((1, H)), const((H, 2 * HY_WIDTH)), const((1, HY_WIDTH))],
        out_specs=[pl.BlockSpec((2, HYF_TB, HY_WIDTH), lambda i: (0, i, 0)), const((1, HY_WIDTH))],
        out_shape=[jax.ShapeDtypeStruct((2, L, HY_WIDTH), F32), jax.ShapeDtypeStruct((1, HY_WIDTH), F32)],
        compiler_params=_cparams(("arbitrary",)), name="hyena_filter")(
            jnp.asarray(band), w1p, b1.astype(F32)[None], w2.astype(F32), b2.astype(F32)[None],
            freq.astype(F32)[None], w3.astype(F32), jnp.asarray(deltas))


def _fft_factors(n):
    lg = int(round(math.log2(n)))
    n1 = 1 << (lg // 2)
    return n1, n // n1


def _dft(n):
    k = np.arange(n)
    a = -2.0 * np.pi * ((k[:, None] * k[None, :]) % n) / n
    return np.cos(a), np.sin(a)


FFT_G = 8


def _split3(f):
    f = jnp.asarray(np.asarray(f, np.float32))
    hi = f.astype(BF16)
    lo = (f - hi.astype(F32)).astype(BF16)
    return jnp.concatenate([hi, lo, hi], axis=1)


def _dot3(f3, x):
    hi = x.astype(BF16)
    lo = (x - hi.astype(F32)).astype(BF16)
    return jnp.dot(f3, jnp.concatenate([hi, hi, lo], axis=0), preferred_element_type=F32)


def _fft1_kernel(u_ref, f_ref, yr_ref, yi_ref, *, n1):
    f3 = f_ref[...]
    for g in range(FFT_G):
        y = _dot3(f3, u_ref[0, :, g, :])
        yr_ref[0, :, g, :] = y[:n1]
        yi_ref[0, :, g, :] = y[n1:]


def _fft_stage1(u, n1, n2):
    B, L, C = u.shape
    fr, fi = _dft(n1)
    f3 = _split3(np.concatenate([fr[:, :n1 // 2], fi[:, :n1 // 2]], 0))
    G = FFT_G
    blk = pl.BlockSpec((1, n1 // 2, G, C), lambda b, j: (b, 0, j, 0))
    oblk = pl.BlockSpec((1, n1, G, C), lambda b, j: (b, 0, j, 0))
    shp = jax.ShapeDtypeStruct((B, n1, n2, C), F32)
    yr, yi = pl.pallas_call(
        functools.partial(_fft1_kernel, n1=n1), grid=(B, n2 // G),
        in_specs=[blk, pl.BlockSpec(f3.shape, lambda b, j: (0, 0))],
        out_specs=[oblk, oblk], out_shape=[shp, shp],
        compiler_params=_cparams(("parallel", "parallel")), name="fft_stage1")(u.reshape(B, n1 // 2, n2, C), f3)
    return yr.reshape(B, n1 * n2, C), yi.reshape(B, n1 * n2, C)


def _cmul(ar, ai, br, bi):
    return ar * br - ai * bi, ar * bi + ai * br


def _fft2_fwd(yr, yi, tr, ti, f2, n2, C):
    ar, ai = _cmul(yr, yi, tr, ti)
    p = _dot3(f2, jnp.concatenate([ar, ai], axis=1))
    return p[:n2, :C] - p[n2:, C:], p[:n2, C:] + p[n2:, :C]


FFT_ROWS = 512


def _fft2_filter_kernel(yr_ref, yi_ref, tr_ref, ti_ref, f_ref, l1_ref, kr_ref, ki_ref, *, n2, n):
    C = yr_ref.shape[2]
    f2 = f_ref[...]
    scale = 1.0 / (l1_ref[...] * n)
    for kb in range(FFT_ROWS // n2):
        rws = slice(kb * n2, (kb + 1) * n2)
        tr = jnp.tile(tr_ref[kb], (1, C // LANES))
        ti = jnp.tile(ti_ref[kb], (1, C // LANES))
        gr, gi = _fft2_fwd(yr_ref[0, rws, :], yi_ref[0, rws, :], tr, ti, f2, n2, C)
        hr, hi = _fft2_fwd(yr_ref[1, rws, :], yi_ref[1, rws, :], tr, ti, f2, n2, C)
        kr_ref[rws, :] = (gr + hr) * scale
        ki_ref[rws, :] = (gi - hi) * scale


def _fft2_conv_kernel(yr_ref, yi_ref, tr_ref, ti_ref, f_ref, kr_ref, ki_ref, zr_ref, zi_ref, *, n2):
    C = yr_ref.shape[2]
    f2 = f_ref[...]
    for kb in range(FFT_ROWS // n2):
        rws = slice(kb * n2, (kb + 1) * n2)
        tr = jnp.tile(tr_ref[kb], (1, C // LANES))
        ti = jnp.tile(ti_ref[kb], (1, C // LANES))
        xr, xi = _fft2_fwd(yr_ref[0, rws, :], yi_ref[0, rws, :], tr, ti, f2, n2, C)
        vr, vi = _cmul(xr, xi, kr_ref[rws, :], ki_ref[rws, :])
        q = _dot3(f2, jnp.concatenate([vr, vi], axis=1))
        wr = q[:n2, :C] + q[n2:, C:]
        wi = q[:n2, C:] - q[n2:, :C]
        zr, zi = _cmul(wr, wi, tr, -ti)
        zr_ref[0, rws, :] = zr
        zi_ref[0, rws, :] = zi


def _fft_tables(n1, n2):
    n = n1 * n2
    k1 = jnp.arange(n1, dtype=jnp.int32)[:, None]
    j2 = jnp.arange(n2, dtype=jnp.int32)[None, :]
    a = (-2.0 * math.pi / n) * (k1 * j2).astype(F32)
    tr = jnp.broadcast_to(jnp.cos(a)[:, :, None], (n1, n2, LANES))
    ti = jnp.broadcast_to(jnp.sin(a)[:, :, None], (n1, n2, LANES))
    fr, fi = _dft(n2)
    return tr, ti, _split3(np.concatenate([fr, fi], 0))


def _fft_stage2_filter(yr, yi, tables, l1, n1, n2):
    _, N, C = yr.shape
    tr, ti, f2 = tables
    kb = FFT_ROWS // n2
    blk = pl.BlockSpec((2, FFT_ROWS, C), lambda k: (0, k, 0))
    tblk = pl.BlockSpec((kb, n2, LANES), lambda k: (k, 0, 0))
    oblk = pl.BlockSpec((FFT_ROWS, C), lambda k: (k, 0))
    shp = jax.ShapeDtypeStruct((N, C), F32)
    return pl.pallas_call(
        functools.partial(_fft2_filter_kernel, n2=n2, n=N), grid=(N // FFT_ROWS,),
        in_specs=[blk, blk, tblk, tblk, pl.BlockSpec(f2.shape, lambda k: (0, 0)),
                  pl.BlockSpec((1, C), lambda k: (0, 0))],
        out_specs=[oblk, oblk], out_shape=[shp, shp],
        compiler_params=_cparams(("parallel",)), name="fft_stage2_filter")(yr, yi, tr, ti, f2, l1)


def _fft_stage2_conv(yr, yi, tables, spec_r, spec_i, n1, n2):
    B, N, C = yr.shape
    tr, ti, f2 = tables
    kb = FFT_ROWS // n2
    blk = pl.BlockSpec((1, FFT_ROWS, C), lambda b, k: (b, k, 0))
    tblk = pl.BlockSpec((kb, n2, LANES), lambda b, k: (k, 0, 0))
    sblk = pl.BlockSpec((FFT_ROWS, C), lambda b, k: (k, 0))
    shp = jax.ShapeDtypeStruct((B, N, C), F32)
    return pl.pallas_call(
        functools.partial(_fft2_conv_kernel, n2=n2), grid=(B, N // FFT_ROWS),
        in_specs=[blk, blk, tblk, tblk, pl.BlockSpec(f2.shape, lambda b, k: (0, 0)), sblk, sblk],
        out_specs=[blk, blk], out_shape=[shp, shp],
        compiler_params=_cparams(("parallel", "parallel")), name="fft_stage2_conv")(
            yr, yi, tr, ti, f2, spec_r, spec_i)


def _fft3_kernel(zr_ref, zi_ref, f_ref, x0_ref, u_ref, bias_ref, o_ref):
    f3 = f_ref[...]
    bias = bias_ref[...]
    for g in range(FFT_G):
        conv = _dot3(f3, jnp.concatenate([zr_ref[0, :, g, :], zi_ref[0, :, g, :]], axis=0))
        u = u_ref[0, :, g, :]
        o_ref[0, :, g, :] = x0_ref[0, :, g, :] * (conv + u * bias)


def _fft_stage3(zr, zi, x0, u, bias, n1, n2):
    B, L, C = u.shape
    fr, fi = _dft(n1)
    f3 = _split3(np.concatenate([fr[:n1 // 2], fi[:n1 // 2]], 1))
    G = FFT_G
    zblk = pl.BlockSpec((1, n1, G, C), lambda b, j: (b, 0, j, 0))
    ublk = pl.BlockSpec((1, n1 // 2, G, C), lambda b, j: (b, 0, j, 0))
    v4 = lambda a: a.reshape(B, -1, n2, C)
    out = pl.pallas_call(
        _fft3_kernel, grid=(B, n2 // G),
        in_specs=[zblk, zblk, pl.BlockSpec(f3.shape, lambda b, j: (0, 0)), ublk, ublk,
                  pl.BlockSpec((1, C), lambda b, j: (0, 0))],
        out_specs=ublk, out_shape=jax.ShapeDtypeStruct((B, n1 // 2, n2, C), F32),
        compiler_params=_cparams(("parallel", "parallel")), name="fft_stage3")(
            v4(zr), v4(zi), f3, v4(x0), v4(u), bias)
    return out.reshape(B * L, C)


def _hyena(proj, conv_w, conv_b, filt, hy_bias, row0, B, L):
    n1, n2 = _fft_factors(2 * L)
    x0, u = _short_conv(proj, conv_w, conv_b, row0, B, L)
    x0 = x0.reshape(B, L, HY_WIDTH)
    u = u.reshape(B, L, HY_WIDTH)
    tables = _fft_tables(n1, n2)
    hfb, l1 = _hyena_filter(L, *filt)
    fr, fi = _fft_stage1(hfb, n1, n2)
    sr, si = _fft_stage2_filter(fr, fi, tables, l1, n1, n2)
    yr, yi = _fft_stage1(u, n1, n2)
    zr, zi = _fft_stage2_conv(yr, yi, tables, sr, si, n1, n2)
    return _fft_stage3(zr, zi, x0, u, hy_bias, n1, n2)


def _merge_kernel(x_ref, g_ref, wg_ref, ya_ref, yb_ref, yc_ref, wa_ref, wb_ref, wc_ref, wo_ref, o_ref):
    x = x_ref[...]
    xn = _rms(x, g_ref[...]).astype(BF16)
    merged = None
    for j, (y_ref, w_ref) in enumerate(((ya_ref, wa_ref), (yb_ref, wb_ref), (yc_ref, wc_ref))):
        gate = jnp.dot(xn, wg_ref[:, j * D_MODEL:(j + 1) * D_MODEL], preferred_element_type=F32)
        br = jnp.dot(y_ref[...].astype(BF16), w_ref[...], preferred_element_type=F32)
        term = jax.nn.sigmoid(gate) * br
        merged = term if merged is None else merged + term
    o_ref[...] = x + jnp.dot(merged.astype(BF16), wo_ref[...], preferred_element_type=F32)


def _merge(x, g, wg, ya, yb, yc, wa, wb, wc, wo):
    T = x.shape[0]
    tm = min(512, T)
    xspec = pl.BlockSpec((tm, D_MODEL), lambda i: (i, 0))
    yspec = pl.BlockSpec((tm, BRANCH_W), lambda i: (i, 0))
    const = lambda shape: pl.BlockSpec(shape, lambda i: (0,) * len(shape))
    return pl.pallas_call(
        _merge_kernel, grid=(T // tm,),
        in_specs=[xspec, const((1, D_MODEL)), const((D_MODEL, 3 * D_MODEL)), yspec, yspec, yspec,
                  const((BRANCH_W, D_MODEL)), const((BRANCH_W, D_MODEL)), const((BRANCH_W, D_MODEL)),
                  const((D_MODEL, D_MODEL))],
        out_specs=xspec, out_shape=jax.ShapeDtypeStruct((T, D_MODEL), F32),
        compiler_params=_cparams(("parallel",)), name="merge_out_projection")(
            x, g, wg, ya, yb, yc, wa, wb, wc, wo)


def _router_kernel(x_ref, g_ref, w_ref, b_ref, xn_ref, r_ref):
    xn = _rms(x_ref[...], g_ref[...])
    xn_ref[...] = xn
    lg = jnp.dot(xn, w_ref[...], precision=HIGHEST, preferred_element_type=F32) + b_ref[...]
    lane = lax.broadcasted_iota(jnp.int32, lg.shape, 1).astype(F32)
    neg = -jnp.inf
    is_g = lane < N_GROUPS
    gl = jnp.where(is_g, lg, neg)
    gmax = jnp.max(gl, axis=1, keepdims=True)
    gsel = jnp.min(jnp.where(gl == gmax, lane, float(LANES)), axis=1, keepdims=True)
    gprob = 1.0 / jnp.sum(jnp.where(is_g, jnp.exp(lg - gmax), 0.0), axis=1, keepdims=True)
    lo = N_GROUPS + gsel * EXPERTS_PER_GROUP
    el = jnp.where((lane >= lo) & (lane < lo + EXPERTS_PER_GROUP), lg, neg)
    v1 = jnp.max(el, axis=1, keepdims=True)
    i1 = jnp.min(jnp.where(el == v1, lane, float(LANES)), axis=1, keepdims=True)
    el2 = jnp.where(lane == i1, neg, el)
    v2 = jnp.max(el2, axis=1, keepdims=True)
    i2 = jnp.min(jnp.where(el2 == v2, lane, float(LANES)), axis=1, keepdims=True)
    e = jnp.exp(v2 - v1)
    w1 = gprob / (1.0 + e)
    w2 = w1 * e
    r_ref[...] = jnp.where(lane == 0, i1 - N_GROUPS,
                           jnp.where(lane == 1, i2 - N_GROUPS,
                                     jnp.where(lane == 2, w1, jnp.where(lane == 3, w2, 0.0))))


def _router(x, g, w, b):
    T = x.shape[0]
    tm = min(512, T)
    xspec = pl.BlockSpec((tm, D_MODEL), lambda i: (i, 0))
    const = lambda shape: pl.BlockSpec(shape, lambda i: (0,) * len(shape))
    return pl.pallas_call(
        _router_kernel, grid=(T // tm,),
        in_specs=[xspec, const((1, D_MODEL)), const((D_MODEL, LANES)), const((1, LANES))],
        out_specs=[xspec, pl.BlockSpec((tm, LANES), lambda i: (i, 0))],
        out_shape=[jax.ShapeDtypeStruct((T, D_MODEL), F32), jax.ShapeDtypeStruct((T, LANES), F32)],
        compiler_params=_cparams(("parallel",)), name="router")(x, g, w, b)


MOE_ROWS = 512


def _moe_kernel(be_ref, idx_hbm, xn_hbm, wg_ref, wu_ref, wd_ref, out_hbm, idx, xbuf, obuf, isem, gsem, ssem):
    R = MOE_ROWS
    i = pl.program_id(0)
    n = pl.num_programs(0)
    s = i % 2

    def idx_copy(blk, slot):
        return pltpu.make_async_copy(idx_hbm.at[blk], idx.at[slot], isem.at[slot])

    def issue_gathers(slot):
        for r in range(R):
            pltpu.make_async_copy(xn_hbm.at[pl.ds(idx[slot, r], 1)], xbuf.at[slot, pl.ds(r, 1)],
                                  gsem.at[slot]).start()

    def wait_gathers(slot):
        pltpu.make_async_copy(xn_hbm.at[pl.ds(0, R)], xbuf.at[slot], gsem.at[slot]).wait()

    def issue_scatters(slot):
        for r in range(R):
            pltpu.make_async_copy(obuf.at[slot, pl.ds(r, 1)], out_hbm.at[pl.ds(idx[slot, R + r], 1)],
                                  ssem.at[slot]).start()

    def wait_scatters(slot):
        pltpu.make_async_copy(obuf.at[slot], out_hbm.at[pl.ds(0, R)], ssem.at[slot]).wait()

    @pl.when(i == 0)
    def _():
        c = idx_copy(0, 0)
        c.start()
        c.wait()
        issue_gathers(0)

        @pl.when(n > 1)
        def _():
            idx_copy(1, 1).start()

    wait_gathers(s)

    @pl.when(i + 1 < n)
    def _():
        idx_copy(i + 1, 1 - s).wait()
        issue_gathers(1 - s)

    @pl.when(i >= 2)
    def _():
        wait_scatters(s)

    xb = xbuf[s].astype(BF16)
    h = jax.nn.silu(jnp.dot(xb, wg_ref[0], preferred_element_type=F32)) \
        * jnp.dot(xb, wu_ref[0], preferred_element_type=F32)
    obuf[s] = jnp.dot(h.astype(BF16), wd_ref[0], preferred_element_type=F32)
    issue_scatters(s)

    @pl.when(i + 2 < n)
    def _():
        idx_copy(i + 2, s).start()

    @pl.when(i == n - 1)
    def _():
        wait_scatters(s)

        @pl.when(n > 1)
        def _():
            wait_scatters(1 - s)


def _moe_experts(xn, idx, block_e, wg, wu, wd):
    T = xn.shape[0]
    R = MOE_ROWS
    n_blocks = idx.shape[0]
    grid_spec = pltpu.PrefetchScalarGridSpec(
        num_scalar_prefetch=1, grid=(n_blocks,),
        in_specs=[pl.BlockSpec(memory_space=pl.ANY), pl.BlockSpec(memory_space=pl.ANY),
                  pl.BlockSpec((1, D_MODEL, EXPERT_HIDDEN), lambda i, be: (be[i], 0, 0)),
                  pl.BlockSpec((1, D_MODEL, EXPERT_HIDDEN), lambda i, be: (be[i], 0, 0)),
                  pl.BlockSpec((1, EXPERT_HIDDEN, D_MODEL), lambda i, be: (be[i], 0, 0))],
        out_specs=pl.BlockSpec(memory_space=pl.ANY),
        scratch_shapes=[pltpu.SMEM((2, 2 * R), jnp.int32), pltpu.VMEM((2, R, D_MODEL), F32),
                        pltpu.VMEM((2, R, D_MODEL), F32), pltpu.SemaphoreType.DMA((2,)),
                        pltpu.SemaphoreType.DMA((2,)), pltpu.SemaphoreType.DMA((2,))])
    return pl.pallas_call(
        _moe_kernel, grid_spec=grid_spec, out_shape=jax.ShapeDtypeStruct((TOP_K * T + 2 * R, D_MODEL), F32),
        compiler_params=_cparams(("arbitrary",)), name="moe_experts")(block_e, idx, xn, wg, wu, wd)


def _dispatch(route, T):
    R = MOE_ROWS
    M = T * TOP_K
    expert_id = route[:, :TOP_K].astype(jnp.int32).reshape(-1)
    order = jnp.argsort(expert_id).astype(jnp.int32)
    counts =jnp.bincount(expert_id, length=N_EXPERTS).astype(jnp.int32)
    starts = jnp.cumsum(counts) - counts
    padded = (counts + R - 1) // R * R
    p_ends = jnp.cumsum(padded)
    p_starts = p_ends - padded
    n_blocks = -(-M // R) + N_EXPERTS
    block_e = jnp.minimum(jnp.searchsorted(p_ends, jnp.arange(n_blocks, dtype=jnp.int32) * R, side='right'),
                          N_EXPERTS - 1).astype(jnp.int32)
    row = jnp.arange(n_blocks * R, dtype=jnp.int32).reshape(n_blocks, R)
    k = row - p_starts[block_e][:, None]
    valid = (k < counts[block_e][:, None]) & (row < p_ends[N_EXPERTS - 1])
    a = order[jnp.clip(starts[block_e][:, None] + k, 0, M - 1)]
    token = a >> 1
    src = jnp.where(valid, token, 0)
    scrap = M + row % (2 * R)
    dst = jnp.where(valid, (a & 1) * T + token, scrap)
    return jnp.concatenate([src, dst], axis=1), block_e


def _final_kernel(x_ref, e0_ref, e1_ref, r_ref, g_ref, o_ref):
    r = r_ref[...]
    x = x_ref[...] + e0_ref[...] * r[:, 2:3] + e1_ref[...] * r[:, 3:4]
    o_ref[...] = _rms(x, g_ref[...])


def _final(x, e2, route, g, row0, rows):
    T = x.shape[0]
    tm = min(512, rows)
    nT = T // tm
    b0 = row0 // tm
    xspec = pl.BlockSpec((tm, D_MODEL), lambda i: (b0 + i, 0))
    return pl.pallas_call(
        _final_kernel, grid=(rows // tm,),
        in_specs=[xspec, xspec, pl.BlockSpec((tm, D_MODEL), lambda i: (nT + b0 + i, 0)),
                  pl.BlockSpec((tm, LANES), lambda i: (b0 + i, 0)), pl.BlockSpec((1, D_MODEL), lambda i: (0, 0))],
        out_specs=pl.BlockSpec((tm, D_MODEL), lambda i: (i, 0)),
        out_shape=jax.ShapeDtypeStruct((rows, D_MODEL), F32),
        compiler_params=_cparams(("parallel",)), name="combine_final_norm")(x, e2, e2, route, g)


def _pack_layer(l, p):
    w = p['w_in'][l]
    o_gate = 3 * HG_WIDTH + 2 * HG_WIDTH + 4 * ML_WIDTH
    o_hy = o_gate + 4 * ML_HEADS
    o_g = o_hy + 3 * HY_WIDTH
    gates = jnp.pad(w[:, o_gate:o_hy], ((0, 0), (0, LANES - 4 * ML_HEADS)))
    w_proj = jnp.concatenate([w[:, :o_gate], w[:, o_hy:o_g], gates], axis=1).astype(BF16)
    w_gate = w[:, o_g:].astype(BF16)
    gate_bias = jnp.pad(jnp.concatenate([p['ml_i_bias'][l].reshape(-1), p['ml_f_bias'][l].reshape(-1)]),
                        (0, LANES - 4 * ML_HEADS)).astype(F32)[None]
    w_router = jnp.pad(jnp.concatenate([p['router_group_w'][l], p['router_expert_w'][l]], axis=1),
                       ((0, 0), (0, LANES - N_GROUPS - N_EXPERTS))).astype(F32)
    b_router = jnp.pad(jnp.concatenate([p['router_group_b'][l], p['router_expert_b'][l]]),
                       (0, LANES - N_GROUPS - N_EXPERTS)).astype(F32)[None]
    return dict(
        norm_mix_g=p['norm_mix_g'][l].astype(F32)[None], w_proj=w_proj, w_gate=w_gate,
        hg_norm_g=p['hg_norm_g'][l].astype(F32)[None], gate_bias=gate_bias,
        ml_norm_g=p['ml_norm_g'][l].astype(F32)[None],
        conv_w=jnp.pad(p['hy_conv_w'][l].astype(F32), ((0, 5), (0, 0))), conv_b=p['hy_conv_b'][l].astype(F32)[None],
        filt=(p['filt_w1'][l], p['filt_b1'][l], p['filt_w2'][l], p['filt_b2'][l], p['filt_freq'][l],
              p['filt_w3'][l]),
        hy_bias=p['hy_bias'][l].astype(F32)[None],
        wa=p['w_branch_a'][l].astype(BF16), wb=p['w_branch_b'][l].astype(BF16),
        wc=p['w_branch_c'][l].astype(BF16), wo=p['w_out'][l].astype(BF16),
        norm_ffn_g=p['norm_ffn_g'][l].astype(F32)[None], w_router=w_router, b_router=b_router,
        wg=p['exp_w_gate'][l].astype(BF16), wu=p['exp_w_up'][l].astype(BF16),
        wd=p['exp_w_down'][l].astype(BF16))


def _trunk(x, groups, p):
    T = x.shape[0]
    lbs = jnp.cumsum(jax.nn.softmax(p['hg_lb'].astype(F32), axis=0), axis=0)
    lbs = lbs - lbs[0:1]
    moe = None
    for l in range(DEPTH):
        lp = _pack_layer(l, p)
        lb = lbs[l][None]
        lbrow = jnp.concatenate([jnp.log(lb), jnp.log1p(-lb), 1.0 - lb, jnp.zeros((5, HG_WIDTH), F32)], axis=0)
        if moe is None:
            proj = _in_projection(x, lp['norm_mix_g'], lp['w_proj'])
        else:
            proj, x = _in_projection(x, lp['norm_mix_g'], lp['w_proj'], moe)
        ya, yb, yc = [], [], []
        for row0, B, L in groups:
            ya.append(_hgrn2(proj, lbrow, lp['hg_norm_g'], row0, B, L))
            yb.append(_mlstm(proj, lp['gate_bias'], lp['ml_norm_g'], row0, B, L))
            yc.append(_hyena(proj, lp['conv_w'], lp['conv_b'], lp['filt'], lp['hy_bias'], row0, B, L))
        ya, yb, yc = (jnp.concatenate(v, axis=0) for v in (ya, yb, yc))
        x = _merge(x, lp['norm_mix_g'], lp['w_gate'], ya, yb, yc, lp['wa'], lp['wb'], lp['wc'], lp['wo'])
        xn, route = _router(x, lp['norm_ffn_g'], lp['w_router'], lp['b_router'])
        rows, block_e = _dispatch(route, T)
        e2 = _moe_experts(xn, rows, block_e, lp['wg'], lp['wu'], lp['wd'])
        moe = (e2, route)
    g = p['final_norm_g'].astype(F32)[None]
    return [_final(x, moe[0], moe[1], g, row0, B * L) for row0, B, L in groups]


def kernel(x_prompt, x_sample, norm_mix_g, w_in, hg_lb, hg_norm_g, ml_i_bias, ml_f_bias, ml_norm_g, hy_conv_w, hy_conv_b, filt_w1, filt_b1, filt_w2, filt_b2, filt_freq, filt_w3, hy_bias, w_branch_a, w_branch_b, w_branch_c, w_out, norm_ffn_g, router_group_w, router_group_b, router_expert_w, router_expert_b, exp_w_gate, exp_w_up, exp_w_down, final_norm_g):
    p = dict(norm_mix_g=norm_mix_g, w_in=w_in, hg_lb=hg_lb, hg_norm_g=hg_norm_g, ml_i_bias=ml_i_bias,
             ml_f_bias=ml_f_bias, ml_norm_g=ml_norm_g, hy_conv_w=hy_conv_w, hy_conv_b=hy_conv_b,
             filt_w1=filt_w1, filt_b1=filt_b1, filt_w2=filt_w2, filt_b2=filt_b2, filt_freq=filt_freq,
             filt_w3=filt_w3, hy_bias=hy_bias, w_branch_a=w_branch_a, w_branch_b=w_branch_b,
             w_branch_c=w_branch_c, w_out=w_out, norm_ffn_g=norm_ffn_g, router_group_w=router_group_w,
             router_group_b=router_group_b, router_expert_w=router_expert_w,
             router_expert_b=router_expert_b, exp_w_gate=exp_w_gate, exp_w_up=exp_w_up,
             exp_w_down=exp_w_down, final_norm_g=final_norm_g)
    Bp, Lp, _ = x_prompt.shape
    Bs, Ls, _ = x_sample.shape
    Tp, Ts = Bp * Lp, Bs * Ls
    x = jnp.concatenate([x_prompt.reshape(Tp, D_MODEL), x_sample.reshape(Ts, D_MODEL)], axis=0).astype(F32)
    yp, ys = _trunk(x, ((0, Bp, Lp), (Tp, Bs, Ls)), p)
    return (yp.reshape(Bp, Lp, D_MODEL), ys.reshape(Bs, Ls, D_MODEL))
```

```python
import functools
import math

import numpy as np
import jax
import jax.numpy as jnp
from jax import lax
from jax.experimental import pallas as pl
from jax.experimental.pallas import tpu as pltpu

F32 = jnp.float32
BF16 = jnp.bfloat16
HIGHEST = lax.Precision.HIGHEST

D_MODEL = 1024
EPS = 1e-6
DEPTH = 2
HG_HEADS, HG_KDIM, HG_WIDTH, HG_CHUNK = 8, 64, 512, 32
ML_HEADS, ML_HDIM, ML_WIDTH, ML_CHUNK = 4, 128, 512, 128
HY_WIDTH, HY_BANDS, HY_FILTER_HIDDEN = 512, 16, 64
HY_EMB = 1 + 2 * HY_BANDS
HY_SHORT_DECAY_PCT, HY_LONG_DECAY_PCT, HY_DECAY_TARGET = 0.3, 1.5, 1e-2
N_GROUPS, EXPERTS_PER_GROUP, TOP_K = 4, 8, 2
N_EXPERTS = N_GROUPS * EXPERTS_PER_GROUP
EXPERT_HIDDEN = D_MODEL // 2
MOE_BLOCK = 128

LANES = 128
BRANCH_W = 512
SLAB_HQ, SLAB_HFF, SLAB_HFB, SLAB_HI, SLAB_HG = 0, 1, 2, 3, 4
SLAB_MQ, SLAB_MK, SLAB_MV, SLAB_MO = 5, 6, 7, 8
SLAB_HY = 9
PROJ_MAIN = 12 * BRANCH_W
PROJ_W = PROJ_MAIN + 2 * LANES
PROJ_TN = 1280
GATE_BLOCK = PROJ_MAIN // LANES
VMEM_LIMIT = 48 * 1024 * 1024
VMEM_LIMIT_WIDE = 56 * 1024 * 1024


def _cparams(sem, vmem=VMEM_LIMIT):
    return pltpu.CompilerParams(dimension_semantics=sem, vmem_limit_bytes=vmem)


def _rms(x, g):
    return x * lax.rsqrt(jnp.mean(x * x, axis=-1, keepdims=True) + EPS) * g


def _log_sigmoid(z):
    return jnp.minimum(z, 0.0) - jnp.log1p(jnp.exp(-jnp.abs(z)))


def _bf16_terms(x):
    t1 = x.astype(BF16)
    r = x - t1.astype(F32)
    t2 = r.astype(BF16)
    return t1, t2, (r - t2.astype(F32)).astype(BF16)


def _inproj_kernel(x_ref, g_ref, w_ref, o_ref, xn_ref):
    @pl.when(pl.program_id(1) == 0)
    def _():
        xn_ref[...] = _rms(x_ref[...], g_ref[...]).astype(BF16)

    o_ref[...] = jnp.dot(xn_ref[...], w_ref[...], preferred_element_type=F32)


def _inproj_combine_kernel(x_ref, e0_ref, e1_ref, r_ref, g_ref, w_ref, o_ref, xo_ref, xn_ref):
    @pl.when(pl.program_id(1) == 0)
    def _():
        r = r_ref[...]
        x = x_ref[...] + e0_ref[...] * r[:, 2:3] + e1_ref[...] * r[:, 3:4]
        xo_ref[...] = x
        xn_ref[...] = _rms(x, g_ref[...]).astype(BF16)

    o_ref[...] = jnp.dot(xn_ref[...], w_ref[...], preferred_element_type=F32)


def _in_projection(x, g, w, moe=None):
    T = x.shape[0]
    tm = min(1024, T)
    tn = PROJ_TN
    grid = (T // tm, PROJ_W // tn)
    xspec = pl.BlockSpec((tm, D_MODEL), lambda i, j: (i, 0))
    gspec = pl.BlockSpec((1, D_MODEL), lambda i, j: (0, 0))
    wspec = pl.BlockSpec((D_MODEL, tn), lambda i, j: (0, j))
    ospec = pl.BlockSpec((tm, tn), lambda i, j: (i, j))
    scratch = [pltpu.VMEM((tm, D_MODEL), BF16)]
    cp = _cparams(("parallel", "arbitrary"))
    if moe is None:
        return pl.pallas_call(
            _inproj_kernel, grid=grid, in_specs=[xspec, gspec, wspec], out_specs=ospec,
            out_shape=jax.ShapeDtypeStruct((T, PROJ_W), F32), scratch_shapes=scratch,
            compiler_params=cp, name="in_projection")(x, g, w)
    e2, route = moe
    nT = T // tm
    e0spec = pl.BlockSpec((tm, D_MODEL), lambda i, j: (i, 0))
    e1spec = pl.BlockSpec((tm, D_MODEL), lambda i, j: (nT + i, 0))
    rspec = pl.BlockSpec((tm, LANES), lambda i, j: (i, 0))
    proj, xo = pl.pallas_call(
        _inproj_combine_kernel, grid=grid, in_specs=[xspec, e0spec, e1spec, rspec, gspec, wspec],
        out_specs=[ospec, xspec],
        out_shape=[jax.ShapeDtypeStruct((T, PROJ_W), F32), jax.ShapeDtypeStruct((T, D_MODEL), F32)],
        scratch_shapes=scratch, compiler_params=_cparams(("parallel", "arbitrary"), VMEM_LIMIT_WIDE),
        name="in_projection_combine")(x, e2, e2, route, g, w)
    return proj, xo


HG_TB = 256
HG_PAIRS = HG_WIDTH // LANES


HG_UNROLL = 8
HG_SUB = 8
HG_NSUB = HG_CHUNK // HG_SUB
HG_XROWS = (HG_NSUB - 1) * HG_HEADS * HG_SUB
HG_XCOLS = HG_SUB * HG_NSUB * (HG_NSUB - 1) // 2


def _hgrn2_tiles(reverse):
    if reverse:
        return [(i, (i + 1) * HG_SUB, HG_CHUNK - (i + 1) * HG_SUB) for i in range(HG_NSUB - 1)]
    return [(i, 0, i * HG_SUB) for i in range(1, HG_NSUB)]


def _hgrn2_kernel(*refs, reverse, finalize):
    if finalize:
        (q_ref, z_ref, v_ref, lb_ref, tri_ref, bd_ref, xm_ref, of_ref, gate_ref, ng_ref,
         o_ref, st_ref) = refs
    else:
        (q_ref, z_ref, v_ref, lb_ref, tri_ref, bd_ref, xm_ref, o_ref, st_ref) = refs
    C, c, nb, W = HG_CHUNK, HG_SUB, HG_NSUB, HG_WIDTH
    n_chunks = HG_TB // C

    @pl.when(pl.program_id(1) == 0)
    def _():
        st_ref[...] = jnp.zeros_like(st_ref)

    log_lb = lb_ref[0:1, :]
    log_1mlb = lb_ref[1:2, :]
    one_m_lb = lb_ref[2:3, :]
    tri3 = tri_ref[...]
    bd = bd_ref[...]
    bd_bf = bd.astype(BF16)
    xmask = xm_ref[...]
    sub_row = lax.broadcasted_iota(jnp.int32, (nb, c, W), 1)
    lane_head = lax.broadcasted_iota(jnp.int32, (c, W), 1) // HG_KDIM
    tiles = _hgrn2_tiles(reverse)

    def chunk(i):
        ci = (n_chunks - 1 - i) if reverse else i
        off = pl.multiple_of(ci * C, C)
        q = q_ref[pl.ds(off, C), :]
        z = z_ref[pl.ds(off, C), :]
        v = v_ref[pl.ds(off, C), :]
        ls = jnp.minimum(z, 0.0) - jnp.log(1.0 + jnp.exp(-jnp.abs(z)))
        hi = log_1mlb + ls
        mx = jnp.maximum(log_lb, hi)
        lf = mx + jnp.log(1.0 + jnp.exp(-jnp.abs(log_lb - hi)))
        kk = one_m_lb * jnp.exp(ls - z)
        b = jnp.dot(tri3, jnp.concatenate(_bf16_terms(lf), axis=0), preferred_element_type=F32)
        btot = b[0:1, :] if reverse else b[C - 1:C, :]
        qb = (q * jnp.exp(b)).astype(BF16)
        kb = (kk * jnp.exp(btot - b)).astype(BF16)
        dec = jnp.exp(btot)
        vb = v.astype(BF16)

        q3, k3, v3 = (a.reshape(nb, c, W) for a in (q, kk, v))
        one = 1 if not reverse else c - 1
        f_sh = (1.0 - kk).reshape(nb, c, W)
        k_sh = k3
        decay = None
        a_rows = [(q * kk).astype(BF16)]
        for d in range(1, c):
            ok = (sub_row < c - d) if reverse else (sub_row >= d)
            decay = f_sh if decay is None else decay * f_sh
            f_sh = pltpu.roll(f_sh, one, axis=1)
            k_sh = pltpu.roll(k_sh, one, axis=1)
            a = jnp.where(ok, q3 * k_sh * decay, 0.0)
            a_rows.append(a.reshape(C, W).astype(BF16))
        a_all = jnp.concatenate(a_rows, axis=0)
        sums = jnp.concatenate(
            [jnp.dot(a_all[:, p * LANES:(p + 1) * LANES], bd_bf, preferred_element_type=F32)
             for p in range(HG_PAIRS)], axis=1)
        acc3 = sums[0:C].reshape(nb, c, W) * v3
        v_sh = v3
        for d in range(1, c):
            v_sh = pltpu.roll(v_sh, one, axis=1)
            acc3 = acc3 + sums[d * C:(d + 1) * C].reshape(nb, c, W) * v_sh
        acc = acc3.reshape(C, W)

        qx, kx, vx = [], [], []
        for (ti, s0, sn) in tiles:
            edge = s0 if reverse else s0 + sn - 1
            r = b[edge:edge + 1, :]
            rows = slice(ti * c, (ti + 1) * c)
            qh = q[rows] * jnp.exp(b[rows] - r)
            qx += [jnp.where(lane_head == h, qh, 0.0) for h in range(HG_HEADS)]
            kx.append(kk[s0:s0 + sn] * jnp.exp(r - b[s0:s0 + sn]))
            vx.append(v[s0:s0 + sn])
        qx = jnp.concatenate(qx, axis=0).astype(BF16)
        kx = jnp.concatenate(kx, axis=0).astype(BF16)
        vx = jnp.concatenate(vx, axis=0).astype(BF16)
        sc = lax.dot_general(qx, kx, (((1,), (1,)), ((), ())), preferred_element_type=F32) * xmask
        px = jnp.dot(sc.astype(BF16), vx, preferred_element_type=F32)
        offd = {}
        for n, (ti, s0, sn) in enumerate(tiles):
            base = n * HG_HEADS * c
            t_acc = jnp.where(lane_head == 0, px[base:base + c], 0.0)
            for h in range(1, HG_HEADS):
                t_acc = t_acc + jnp.where(lane_head == h, px[base + h * c:base + (h + 1) * c], 0.0)
            offd[ti] = t_acc
        acc = acc + jnp.concatenate([offd.get(ti, jnp.zeros((c, W), F32)) for ti in range(nb)], axis=0)

        parts = []
        for p in range(HG_PAIRS):
            sl = slice(p * LANES, (p + 1) * LANES)
            s_t = st_ref[p]
            parts.append(lax.dot_general(qb[:, sl], s_t.astype(BF16), (((1,), (1,)), ((), ())),
                                         preferred_element_type=F32))
            upd = lax.dot_general(vb[:, sl], kb[:, sl], (((0,), (0,)), ((), ())),
                                  preferred_element_type=F32)
            st_ref[p] = s_t * dec[:, sl] + upd * bd
        acc = acc + jnp.concatenate(parts, axis=1)

        if finalize:
            o = acc + of_ref[pl.ds(off, C), :]
            o2 = (o * o).astype(BF16)
            ms = jnp.concatenate(
                [jnp.dot(o2[:, p * LANES:(p + 1) * LANES], bd_bf, preferred_element_type=F32)
                 for p in range(HG_PAIRS)], axis=1) * (1.0 / HG_KDIM)
            g = gate_ref[pl.ds(off, C), :]
            y = o * lax.rsqrt(ms + EPS) * ng_ref[...] * (g * jax.nn.sigmoid(g))
            o_ref[pl.ds(off, C), :] = y.astype(o_ref.dtype)
        else:
            o_ref[pl.ds(off, C), :] = acc

    def chunk_group(j, carry):
        for u in range(HG_UNROLL):
            chunk(HG_UNROLL * j + u)
        return carry

    lax.fori_loop(0, n_chunks // HG_UNROLL, chunk_group, 0)


def _hgrn2(proj, lbrow, norm_g, row0, B, L):
    C = HG_CHUNK
    nT = L // HG_TB
    blk0 = row0 // HG_TB
    tri_f = jnp.asarray(np.tile(np.tril(np.ones((C, C), np.float32)), (1, 3))).astype(BF16)
    tri_b = jnp.asarray(np.tile(np.triu(np.ones((C, C), np.float32)), (1, 3))).astype(BF16)
    head = np.arange(LANES) // HG_KDIM
    bd = jnp.asarray((head[:, None] == head[None, :]).astype(np.float32))

    def tile_mask(reverse):
        m = np.zeros((HG_XROWS, HG_XCOLS), np.float32)
        col = 0
        for n, (_, _, sn) in enumerate(_hgrn2_tiles(reverse)):
            m[n * HG_HEADS * HG_SUB:(n + 1) * HG_HEADS * HG_SUB, col:col + sn] = 1.0
            col += sn
        return jnp.asarray(m)

    def in_spec(slab, reverse):
        if reverse:
            return pl.BlockSpec((HG_TB, BRANCH_W), lambda b, t: (blk0 + b * nT + nT - 1 - t, slab))
        return pl.BlockSpec((HG_TB, BRANCH_W), lambda b, t: (blk0 + b * nT + t, slab))

    def out_spec(reverse):
        if reverse:
            return pl.BlockSpec((HG_TB, BRANCH_W), lambda b, t: (b * nT + nT - 1 - t, 0))
        return pl.BlockSpec((HG_TB, BRANCH_W), lambda b, t: (b * nT + t, 0))

    const = lambda shape: pl.BlockSpec(shape, lambda b, t: (0,) * len(shape))
    scratch = [pltpu.VMEM((HG_PAIRS, LANES, LANES), F32)]
    consts = [const((8, BRANCH_W)), const((C, 3 * C)), const((LANES, LANES)), const((HG_XROWS, HG_XCOLS))]
    cp = _cparams(("parallel", "arbitrary"))
    o_f = pl.pallas_call(
        functools.partial(_hgrn2_kernel, reverse=False, finalize=False), grid=(B, nT),
        in_specs=[in_spec(SLAB_HQ, False), in_spec(SLAB_HFF, False), in_spec(SLAB_HI, False)] + consts,
        out_specs=out_spec(False), out_shape=jax.ShapeDtypeStruct((B * L, BRANCH_W), F32),
        scratch_shapes=scratch, compiler_params=cp, name="hgrn2_fwd")(
            proj, proj, proj, lbrow, tri_f, bd, tile_mask(False))
    return pl.pallas_call(
        functools.partial(_hgrn2_kernel, reverse=True, finalize=True), grid=(B, nT),
        in_specs=[in_spec(SLAB_HQ, True), in_spec(SLAB_HFB, True), in_spec(SLAB_HI, True)] + consts
        + [out_spec(True), in_spec(SLAB_HG, True), const((1, BRANCH_W))],
        out_specs=out_spec(True), out_shape=jax.ShapeDtypeStruct((B * L, BRANCH_W), BF16),
        scratch_shapes=scratch, compiler_params=cp, name="hgrn2_bwd")(
            proj, proj, proj, lbrow, tri_b, bd, tile_mask(True), o_f, proj, norm_g)


def _mlstm_kernel(*refs, reverse, finalize):
    if finalize:
        (q_ref, k_ref, v_ref, gt_ref, gb_ref, tri_ref, hf_ref, og_ref, ng_ref,
         o_ref, c_ref, n_ref, m_ref) = refs
    else:
        (q_ref, k_ref, v_ref, gt_ref, gb_ref, tri_ref, o_ref, c_ref, n_ref, m_ref) = refs
    C = ML_CHUNK

    @pl.when(pl.program_id(1) == 0)
    def _():
        c_ref[...] = jnp.zeros_like(c_ref)
        n_ref[...] = jnp.zeros_like(n_ref)
        m_ref[...] = jnp.zeros_like(m_ref)

    tri = tri_ref[...]
    lane = lax.broadcasted_iota(jnp.int32, (C, LANES), 1)
    ti = lax.broadcasted_iota(jnp.int32, (C, C), 0)
    si = lax.broadcasted_iota(jnp.int32, (C, C), 1)
    causal = (si >= ti) if reverse else (si <= ti)
    dsel = ML_HEADS if reverse else 0
    edge = 0 if reverse else C - 1
    scale = 1.0 / math.sqrt(ML_HDIM)
    order = range(ML_TB // C - 1, -1, -1) if reverse else range(ML_TB // C)
    for ci in order:
        _mlstm_chunk(refs, slice(ci * C, (ci + 1) * C), tri, lane, causal, dsel, edge, scale, finalize)


def _mlstm_chunk(refs, rws, tri, lane, causal, dsel, edge, scale, finalize):
    if finalize:
        (q_ref, k_ref, v_ref, gt_ref, gb_ref, tri_ref, hf_ref, og_ref, ng_ref,
         o_ref, c_ref, n_ref, m_ref) = refs
    else:
        (q_ref, k_ref, v_ref, gt_ref, gb_ref, tri_ref, o_ref, c_ref, n_ref, m_ref) = refs
    gates = gt_ref[rws, :] + gb_ref[...]
    g_col = jnp.where(lane >= 2 * ML_HEADS, _log_sigmoid(gates), gates)
    g_row = g_col.T
    b_col = jnp.dot(tri, g_col, precision=HIGHEST, preferred_element_type=F32)
    b_row = lax.dot_general(g_row, tri, (((1,), (1,)), ((), ())), precision=HIGHEST,
                            preferred_element_type=F32)

    outs = []
    for h in range(ML_HEADS):
        sl = slice(h * ML_HDIM, (h + 1) * ML_HDIM)
        li, lf = dsel + h, 2 * ML_HEADS + dsel + h
        q = q_ref[rws, sl]
        kc = k_ref[rws, sl] * scale
        v = v_ref[rws, sl]
        qb, kb, vb = q.astype(BF16), kc.astype(BF16), v.astype(BF16)
        bc = b_col[:, lf:lf + 1]
        br = b_row[lf:lf + 1, :]
        ic = g_col[:, li:li + 1]
        ir = g_row[li:li + 1, :]
        btot = b_col[edge:edge + 1, lf:lf + 1]
        m_prev = m_ref[0:1, h:h + 1]
        n_prev = n_ref[h:h + 1, :]
        c_prev = c_ref[h]

        log_d = jnp.where(causal, bc - br + ir, -jnp.inf)
        log_inter = bc + m_prev
        m_t = jnp.maximum(log_inter, jnp.max(log_d, axis=1, keepdims=True))
        a_inter = jnp.exp(log_inter - m_t)
        s = lax.dot_general(qb, kb, (((1,), (1,)), ((), ())), preferred_element_type=F32)
        s = s * jnp.exp(log_d - m_t)
        num = a_inter * jnp.dot(qb, c_prev.astype(BF16), preferred_element_type=F32) \
            + jnp.dot(s.astype(BF16), vb, preferred_element_type=F32)
        den = a_inter * jnp.sum(q * n_prev, axis=1, keepdims=True) + jnp.sum(s, axis=1, keepdims=True)
        outs.append(num / jnp.maximum(jnp.abs(den), jnp.exp(-m_t)))

        log_w = btot - bc + ic
        m_new = jnp.maximum(btot + m_prev, jnp.max(log_w, axis=0, keepdims=True))
        a_state = jnp.exp(btot + m_prev - m_new)
        kw = kc * jnp.exp(log_w - m_new)
        c_ref[h] = a_state * c_prev + lax.dot_general(kw.astype(BF16), vb, (((0,), (0,)), ((), ())),
                                                      preferred_element_type=F32)
        n_ref[h:h + 1, :] = a_state * n_prev + jnp.sum(kw, axis=0, keepdims=True)
        m_ref[0:1, h:h + 1] = m_new

    hcur = jnp.concatenate(outs, axis=1)
    if finalize:
        hsum = hcur + hf_ref[rws, :]
        ys = []
        for h in range(ML_HEADS):
            sl = slice(h * ML_HDIM, (h + 1) * ML_HDIM)
            hh = hsum[:, sl]
            ys.append(hh * lax.rsqrt(jnp.mean(hh * hh, axis=1, keepdims=True) + EPS))
        y = jnp.concatenate(ys, axis=1) * ng_ref[...] * jax.nn.sigmoid(og_ref[rws, :])
        o_ref[rws, :] = y.astype(o_ref.dtype)
    else:
        o_ref[rws, :] = hcur


ML_TB = 2 * ML_CHUNK


def _mlstm(proj, gate_bias, norm_g, row0, B, L):
    C = ML_TB
    nT = L // C
    blk0 = row0 // C
    tri_f = jnp.asarray(np.tril(np.ones((ML_CHUNK, ML_CHUNK), np.float32)))
    tri_b = jnp.asarray(np.triu(np.ones((ML_CHUNK, ML_CHUNK), np.float32)))

    def rows(reverse):
        if reverse:
            return lambda b, t: blk0 + b * nT + nT - 1 - t
        return lambda b, t: blk0 + b * nT + t

    def in_spec(slab, reverse):
        r = rows(reverse)
        return pl.BlockSpec((C, BRANCH_W), lambda b, t: (r(b, t), slab))

    def gate_spec(reverse):
        r = rows(reverse)
        return pl.BlockSpec((C, LANES), lambda b, t: (r(b, t), GATE_BLOCK))

    def out_spec(reverse):
        if reverse:
            return pl.BlockSpec((C, BRANCH_W), lambda b, t: (b * nT + nT - 1 - t, 0))
        return pl.BlockSpec((C, BRANCH_W), lambda b, t: (b * nT + t, 0))

    const = lambda shape: pl.BlockSpec(shape, lambda b, t: (0,) * len(shape))
    scratch = [pltpu.VMEM((ML_HEADS, ML_HDIM, ML_HDIM), F32), pltpu.VMEM((8, ML_HDIM), F32),
               pltpu.VMEM((8, LANES), F32)]
    cp = _cparams(("parallel", "arbitrary"))
    h_f = pl.pallas_call(
        functools.partial(_mlstm_kernel, reverse=False, finalize=False), grid=(B, nT),
        in_specs=[in_spec(SLAB_MQ, False), in_spec(SLAB_MK, False), in_spec(SLAB_MV, False),
                  gate_spec(False), const((1, LANES)), const((ML_CHUNK, ML_CHUNK))],
        out_specs=out_spec(False), out_shape=jax.ShapeDtypeStruct((B * L, BRANCH_W), F32),
        scratch_shapes=scratch, compiler_params=cp, name="mlstm_fwd")(
            proj, proj, proj, proj, gate_bias, tri_f)
    return pl.pallas_call(
        functools.partial(_mlstm_kernel, reverse=True, finalize=True), grid=(B, nT),
        in_specs=[in_spec(SLAB_MQ, True), in_spec(SLAB_MK, True), in_spec(SLAB_MV, True),
                  gate_spec(True), const((1, LANES)), const((ML_CHUNK, ML_CHUNK)),
                  out_spec(True), in_spec(SLAB_MO, True), const((1, BRANCH_W))],
        out_specs=out_spec(True), out_shape=jax.ShapeDtypeStruct((B * L, BRANCH_W), BF16),
        scratch_shapes=scratch, compiler_params=cp, name="mlstm_bwd")(
            proj, proj, proj, proj, gate_bias, tri_b, h_f, proj, norm_g)


HY_TB = 256


def _shortconv_kernel(c_ref, p_ref, n_ref, w_ref, b_ref, x0_ref, u_ref, *, nT):
    t = pl.program_id(1)
    cur = c_ref[...]
    prev_row = jnp.where(t > 0, p_ref[7:8, :], 0.0)
    next_row = jnp.where(t < nT - 1, n_ref[0:1, :], 0.0)
    row = lax.broadcasted_iota(jnp.int32, cur.shape, 0)
    up = jnp.where(row == 0, prev_row, pltpu.roll(cur, 1, axis=0))
    dn = jnp.where(row == HY_TB - 1, next_row, pltpu.roll(cur, HY_TB - 1, axis=0))
    y = up * w_ref[0:1, :] + cur * w_ref[1:2, :] + dn * w_ref[2:3, :] + b_ref[...]
    x0_ref[...] = y[:, :HY_WIDTH]
    u_ref[...] = y[:, HY_WIDTH:2 * HY_WIDTH] * y[:, 2 * HY_WIDTH:]


def _short_conv(proj, w, b, row0, B, L):
    nT = L // HY_TB
    blk0 = row0 // HY_TB
    sub = HY_TB // 8
    W3 = 3 * HY_WIDTH
    slab = SLAB_HY * BRANCH_W // W3
    cur = pl.BlockSpec((HY_TB, W3), lambda bb, t: (blk0 + bb * nT + t, slab))
    prv = pl.BlockSpec((8, W3), lambda bb, t: (jnp.maximum((blk0 + bb * nT + t) * sub - 1, 0), slab))
    nxt = pl.BlockSpec((8, W3), lambda bb, t: (jnp.minimum((blk0 + bb * nT + t + 1) * sub,
                                                           (blk0 + B * nT) * sub - 1), slab))
    const = lambda shape: pl.BlockSpec(shape, lambda bb, t: (0,) * len(shape))
    out = pl.BlockSpec((HY_TB, HY_WIDTH), lambda bb, t: (bb * nT + t, 0))
    return pl.pallas_call(
        functools.partial(_shortconv_kernel, nT=nT), grid=(B, nT),
        in_specs=[cur, prv, nxt, const((8, W3)), const((1, W3))], out_specs=[out, out],
        out_shape=[jax.ShapeDtypeStruct((B * L, HY_WIDTH), F32)] * 2,
        compiler_params=_cparams(("parallel", "parallel")), name="hyena_short_conv")(proj, proj, proj, w, b)


HYF_TB = 256


def _filter_kernel(band_ref, w1_ref, b1_ref, w2_ref, b2_ref, fr_ref, w3_ref, dl_ref, h_ref, l1_ref, *, L):
    i = pl.program_id(0)
    pos = (lax.broadcasted_iota(jnp.int32, (HYF_TB, LANES), 0) + i * HYF_TB).astype(F32)
    lane = lax.broadcasted_iota(jnp.int32, (HYF_TB, LANES), 1)
    t = pos / (L - 1)
    ang = (2.0 * math.pi * pos / L) * band_ref[...]
    z = jnp.where(lane == 0, t,
                  jnp.where(lane <= HY_BANDS, jnp.cos(ang),
                            jnp.where(lane <= 2 * HY_BANDS, -jnp.sin(ang), 0.0)))
    fr = fr_ref[...]
    h = jnp.sin(fr * (jnp.dot(z, w1_ref[...], precision=HIGHEST, preferred_element_type=F32) + b1_ref[...]))
    h = jnp.sin(fr * (jnp.dot(h, w2_ref[...], precision=HIGHEST, preferred_element_type=F32) + b2_ref[...]))
    h = jnp.dot(h, w3_ref[...], precision=HIGHEST, preferred_element_type=F32)
    tt = (lax.broadcasted_iota(jnp.int32, (HYF_TB, HY_WIDTH), 0) + i * HYF_TB).astype(F32) / (L - 1)
    window = jnp.exp(-tt * dl_ref[...])
    rowi = lax.broadcasted_iota(jnp.int32, (HYF_TB, HY_WIDTH), 0) + i * HYF_TB
    hf = h[:, :HY_WIDTH] * window
    hb = jnp.where(rowi == 0, 0.0, h[:, HY_WIDTH:] * window)
    h_ref[0] = hf
    h_ref[1] = hb

    @pl.when(i == 0)
    def _():
        l1_ref[...] = jnp.zeros_like(l1_ref)

    l1_ref[...] += jnp.sum(jnp.abs(hf) + jnp.abs(hb), axis=0, keepdims=True)


def _hyena_filter(L, w1, b1, w2, b2, freq, w3):
    band = np.zeros((1, LANES), np.float32)
    bands = np.linspace(1e-4, HY_BANDS - 1, HY_BANDS, dtype=np.float32)
    band[0, 1:1 + HY_BANDS] = bands
    band[0, 1 + HY_BANDS:1 + 2 * HY_BANDS] = bands
    max_decay = math.log(HY_DECAY_TARGET) / HY_SHORT_DECAY_PCT
    min_decay = math.log(HY_DECAY_TARGET) / HY_LONG_DECAY_PCT
    deltas = np.abs(np.linspace(min_decay, max_decay, HY_WIDTH, dtype=np.float32))[None, :]
    w1p = jnp.zeros((LANES, HY_FILTER_HIDDEN), F32).at[:HY_EMB].set(w1.astype(F32))
    const = lambda shape: pl.BlockSpec(shape, lambda i: (0,) * len(shape))
    H = HY_FILTER_HIDDEN
    return pl.pallas_call(
        functools.partial(_filter_kernel, L=L), grid=(L // HYF_TB,),
        in_specs=[const((1, LANES)), const((LANES, H)), const((1, H)), const((H, H)), const((1, H)),
                  const((1, H)), const((H, 2 * HY_WIDTH)), const((1, HY_WIDTH))],
        out_specs=[pl.BlockSpec((2, HYF_TB, HY_WIDTH), lambda i: (0, i, 0)), const((1, HY_WIDTH))],
        out_shape=[jax.ShapeDtypeStruct((2, L, HY_WIDTH), F32), jax.ShapeDtypeStruct((1, HY_WIDTH), F32)],
        compiler_params=_cparams(("arbitrary",)), name="hyena_filter")(
            jnp.asarray(band), w1p, b1.astype(F32)[None], w2.astype(F32), b2.astype(F32)[None],
            freq.astype(F32)[None], w3.astype(F32), jnp.asarray(deltas))


def _fft_factors(n):
    lg = int(round(math.log2(n)))
    n1 = 1 << (lg // 2)
    return n1, n // n1


def _dft(n):
    k = np.arange(n)
    a = -2.0 * np.pi * ((k[:, None] * k[None, :]) % n) / n
    return np.cos(a), np.sin(a)


FFT_G = 8


def _split3(f):
    f = jnp.asarray(np.asarray(f, np.float32))
    hi = f.astype(BF16)
    lo = (f - hi.astype(F32)).astype(BF16)
    return jnp.concatenate([hi, lo, hi], axis=1)


def _dot3(f3, x):
    hi = x.astype(BF16)
    lo = (x - hi.astype(F32)).astype(BF16)
    return jnp.dot(f3, jnp.concatenate([hi, hi, lo], axis=0), preferred_element_type=F32)


def _fft1_kernel(u_ref, f_ref, yr_ref, yi_ref, *, n1):
    P, _, G, C = u_ref.shape[1:]
    y = _dot3(f_ref[...], u_ref[0].reshape(P * (n1 // 2) * G, C))
    yr_ref[0] = y[:n1 * G].reshape(n1, G, C)
    yi_ref[0] = y[n1 * G:].reshape(n1, G, C)


def _fft_stage1(u, n1, n2, pair):
    B, L, C = u.shape
    P = 2 if pair else 1
    fr, fi = _dft(n1)
    G = FFT_G
    eye = np.eye(G)
    kr, ki = np.kron(fr[:, :n1 // 2], eye), np.kron(fi[:, :n1 // 2], eye)
    f3 = _split3(np.block([[kr, -ki], [ki, kr]]) if pair else np.concatenate([kr, ki], 0))
    blk = pl.BlockSpec((1, P, n1 // 2, G, C), lambda b, j: (b, 0, 0, j, 0))
    oblk = pl.BlockSpec((1, n1, G, C), lambda b, j: (b, 0, j, 0))
    shp = jax.ShapeDtypeStruct((B // P, n1, n2, C), F32)
    yr, yi = pl.pallas_call(
        functools.partial(_fft1_kernel, n1=n1), grid=(B // P, n2 // G),
        in_specs=[blk, pl.BlockSpec(f3.shape, lambda b, j: (0, 0))],
        out_specs=[oblk, oblk], out_shape=[shp, shp],
        compiler_params=_cparams(("parallel", "parallel")), name="fft_stage1")(
            u.reshape(B // P, P, n1 // 2, n2, C), f3)
    return yr.reshape(B // P, n1 * n2, C), yi.reshape(B // P, n1 * n2, C)


def _cmul(ar, ai, br, bi):
    return ar * br - ai * bi, ar * bi + ai * br


def _fft2_fwd(yr, yi, tr, ti, f2, n2, C):
    ar, ai = _cmul(yr, yi, tr, ti)
    p = _dot3(f2, jnp.concatenate([ar, ai], axis=1))
    return p[:n2, :C] - p[n2:, C:], p[:n2, C:] + p[n2:, :C]


FFT_ROWS = 512


def _fft2_filter_kernel(yr_ref, yi_ref, tr_ref, ti_ref, f_ref, l1_ref, kr_ref, ki_ref, *, n2, n):
    C = yr_ref.shape[2]
    f2 = f_ref[...]
    scale = 1.0 / (l1_ref[...] * n)
    for kb in range(FFT_ROWS // n2):
        rws = slice(kb * n2, (kb + 1) * n2)
        tr = jnp.tile(tr_ref[kb], (1, C // LANES))
        ti = jnp.tile(ti_ref[kb], (1, C // LANES))
        gr, gi = _fft2_fwd(yr_ref[0, rws, :], yi_ref[0, rws, :], tr, ti, f2, n2, C)
        hr, hi = _fft2_fwd(yr_ref[1, rws, :], yi_ref[1, rws, :], tr, ti, f2, n2, C)
        kr_ref[rws, :] = (gr + hr) * scale
        ki_ref[rws, :] = (gi - hi) * scale


def _fft2_conv_kernel(yr_ref, yi_ref, tr_ref, ti_ref, f_ref, kr_ref, ki_ref, zr_ref, zi_ref, *, n2):
    C = yr_ref.shape[2]
    f2 = f_ref[...]
    for kb in range(FFT_ROWS // n2):
        rws = slice(kb * n2, (kb + 1) * n2)
        tr = jnp.tile(tr_ref[kb], (1, C // LANES))
        ti = jnp.tile(ti_ref[kb], (1, C // LANES))
        xr, xi = _fft2_fwd(yr_ref[0, rws, :], yi_ref[0, rws, :], tr, ti, f2, n2, C)
        vr, vi = _cmul(xr, xi, kr_ref[rws, :], ki_ref[rws, :])
        q = _dot3(f2, jnp.concatenate([vr, vi], axis=1))
        wr = q[:n2, :C] + q[n2:, C:]
        wi = q[:n2, C:] - q[n2:, :C]
        zr, zi = _cmul(wr, wi, tr, -ti)
        zr_ref[0, rws, :] = zr
        zi_ref[0, rws, :] = zi


def _fft_tables(n1, n2):
    n = n1 * n2
    k1 = jnp.arange(n1, dtype=jnp.int32)[:, None]
    j2 = jnp.arange(n2, dtype=jnp.int32)[None, :]
    a = (-2.0 * math.pi / n) * (k1 * j2).astype(F32)
    tr = jnp.broadcast_to(jnp.cos(a)[:, :, None], (n1, n2, LANES))
    ti = jnp.broadcast_to(jnp.sin(a)[:, :, None], (n1, n2, LANES))
    fr, fi = _dft(n2)
    return tr, ti, _split3(np.concatenate([fr, fi], 0))


def _fft_stage2_filter(yr, yi, tables, l1, n1, n2):
    _, N, C = yr.shape
    tr, ti, f2 = tables
    kb = FFT_ROWS // n2
    blk = pl.BlockSpec((2, FFT_ROWS, C), lambda k: (0, k, 0))
    tblk = pl.BlockSpec((kb, n2, LANES), lambda k: (k, 0, 0))
    oblk = pl.BlockSpec((FFT_ROWS, C), lambda k: (k, 0))
    shp = jax.ShapeDtypeStruct((N, C), F32)
    return pl.pallas_call(
        functools.partial(_fft2_filter_kernel, n2=n2, n=N), grid=(N // FFT_ROWS,),
        in_specs=[blk, blk, tblk, tblk, pl.BlockSpec(f2.shape, lambda k: (0, 0)),
                  pl.BlockSpec((1, C), lambda k: (0, 0))],
        out_specs=[oblk, oblk], out_shape=[shp, shp],
        compiler_params=_cparams(("parallel",)), name="fft_stage2_filter")(yr, yi, tr, ti, f2, l1)


def _fft_stage2_conv(yr, yi, tables, spec_r, spec_i, n1, n2):
    B, N, C = yr.shape
    tr, ti, f2 = tables
    kb = FFT_ROWS // n2
    blk = pl.BlockSpec((1, FFT_ROWS, C), lambda b, k: (b, k, 0))
    tblk = pl.BlockSpec((kb, n2, LANES), lambda b, k: (k, 0, 0))
    sblk = pl.BlockSpec((FFT_ROWS, C), lambda b, k: (k, 0))
    shp = jax.ShapeDtypeStruct((B, N, C), F32)
    return pl.pallas_call(
        functools.partial(_fft2_conv_kernel, n2=n2), grid=(B, N // FFT_ROWS),
        in_specs=[blk, blk, tblk, tblk, pl.BlockSpec(f2.shape, lambda b, k: (0, 0)), sblk, sblk],
        out_specs=[blk, blk], out_shape=[shp, shp],
        compiler_params=_cparams(("parallel", "parallel")), name="fft_stage2_conv")(
            yr, yi, tr, ti, f2, spec_r, spec_i)


def _fft3_kernel(zr_ref, zi_ref, f_ref, x0_ref, u_ref, bias_ref, o_ref):
    n1, G, C = zr_ref.shape[1:]
    P = u_ref.shape[1]
    z = jnp.concatenate([zr_ref[0].reshape(n1 * G, C), zi_ref[0].reshape(n1 * G, C)], axis=0)
    conv = _dot3(f_ref[...], z).reshape(P, n1 // 2, G, C)
    o_ref[0] = x0_ref[0] * (conv + u_ref[0] * bias_ref[...])


def _fft_stage3(zr, zi, x0, u, bias, n1, n2, pair):
    B, L, C = u.shape
    P = 2 if pair else 1
    fr, fi = _dft(n1)
    G = FFT_G
    eye = np.eye(G)
    kr, ki = np.kron(fr[:n1 // 2], eye), np.kron(fi[:n1 // 2], eye)
    f3 = _split3(np.block([[kr, ki], [-ki, kr]]) if pair else np.concatenate([kr, ki], 1))
    zblk = pl.BlockSpec((1, n1, G, C), lambda b, j: (b, 0, j, 0))
    ublk = pl.BlockSpec((1, P, n1 // 2, G, C), lambda b, j: (b, 0, 0, j, 0))
    v5 = lambda a: a.reshape(B // P, P, n1 // 2, n2, C)
    out = pl.pallas_call(
        _fft3_kernel, grid=(B // P, n2 // G),
        in_specs=[zblk, zblk, pl.BlockSpec(f3.shape, lambda b, j: (0, 0)), ublk, ublk,
                  pl.BlockSpec((1, C), lambda b, j: (0, 0))],
        out_specs=ublk, out_shape=jax.ShapeDtypeStruct((B // P, P, n1 // 2, n2, C), F32),
        compiler_params=_cparams(("parallel", "parallel")), name="fft_stage3")(
            zr.reshape(B // P, n1, n2, C), zi.reshape(B // P, n1, n2, C), f3, v5(x0), v5(u), bias)
    return out.reshape(B * L, C)


def _hyena(proj, conv_w, conv_b, filt, hy_bias, row0, B, L):
    n1, n2 = _fft_factors(2 * L)
    pair = B % 2 == 0
    x0, u = _short_conv(proj, conv_w, conv_b, row0, B, L)
    x0 = x0.reshape(B, L, HY_WIDTH)
    u = u.reshape(B, L, HY_WIDTH)
    tables = _fft_tables(n1, n2)
    hfb, l1 = _hyena_filter(L, *filt)
    fr, fi = _fft_stage1(hfb, n1, n2, False)
    sr, si = _fft_stage2_filter(fr, fi, tables, l1, n1, n2)
    yr, yi = _fft_stage1(u, n1, n2, pair)
    zr, zi = _fft_stage2_conv(yr, yi, tables, sr, si, n1, n2)
    return _fft_stage3(zr, zi, x0, u, hy_bias, n1, n2, pair)


def _merge_kernel(x_ref, g_ref, wg_ref, ya_ref, yb_ref, yc_ref, wa_ref, wb_ref, wc_ref, wo_ref, o_ref):
    x = x_ref[...]
    xn = _rms(x, g_ref[...]).astype(BF16)
    merged = None
    for j, (y_ref, w_ref) in enumerate(((ya_ref, wa_ref), (yb_ref, wb_ref), (yc_ref, wc_ref))):
        gate = jnp.dot(xn, wg_ref[:, j * D_MODEL:(j + 1) * D_MODEL], preferred_element_type=F32)
        br = jnp.dot(y_ref[...].astype(BF16), w_ref[...], preferred_element_type=F32)
        term = jax.nn.sigmoid(gate) * br
        merged = term if merged is None else merged + term
    o_ref[...] = x + jnp.dot(merged.astype(BF16), wo_ref[...], preferred_element_type=F32)


def _merge(x, g, wg, ya, yb, yc, wa, wb, wc, wo):
    T = x.shape[0]
    tm = min(512, T)
    xspec = pl.BlockSpec((tm, D_MODEL), lambda i: (i, 0))
    yspec = pl.BlockSpec((tm, BRANCH_W), lambda i: (i, 0))
    const = lambda shape: pl.BlockSpec(shape, lambda i: (0,) * len(shape))
    return pl.pallas_call(
        _merge_kernel, grid=(T // tm,),
        in_specs=[xspec, const((1, D_MODEL)), const((D_MODEL, 3 * D_MODEL)), yspec, yspec, yspec,
                  const((BRANCH_W, D_MODEL)), const((BRANCH_W, D_MODEL)), const((BRANCH_W, D_MODEL)),
                  const((D_MODEL, D_MODEL))],
        out_specs=xspec, out_shape=jax.ShapeDtypeStruct((T, D_MODEL), F32),
        compiler_params=_cparams(("parallel",)), name="merge_out_projection")(
            x, g, wg, ya, yb, yc, wa, wb, wc, wo)


def _router_kernel(x_ref, g_ref, w_ref, b_ref, xn_ref, r_ref):
    xn = _rms(x_ref[...], g_ref[...])
    xn_ref[...] = xn
    lg = jnp.dot(xn, w_ref[...], precision=HIGHEST, preferred_element_type=F32) + b_ref[...]
    lane = lax.broadcasted_iota(jnp.int32, lg.shape, 1).astype(F32)
    neg = -jnp.inf
    is_g = lane < N_GROUPS
    gl = jnp.where(is_g, lg, neg)
    gmax = jnp.max(gl, axis=1, keepdims=True)
    gsel = jnp.min(jnp.where(gl == gmax, lane, float(LANES)), axis=1, keepdims=True)
    gprob = 1.0 / jnp.sum(jnp.where(is_g, jnp.exp(lg - gmax), 0.0), axis=1, keepdims=True)
    lo = N_GROUPS + gsel * EXPERTS_PER_GROUP
    el = jnp.where((lane >= lo) & (lane < lo + EXPERTS_PER_GROUP), lg, neg)
    v1 = jnp.max(el, axis=1, keepdims=True)
    i1 = jnp.min(jnp.where(el == v1, lane, float(LANES)), axis=1, keepdims=True)
    el2 = jnp.where(lane == i1, neg, el)
    v2 = jnp.max(el2, axis=1, keepdims=True)
    i2 = jnp.min(jnp.where(el2 == v2, lane, float(LANES)), axis=1, keepdims=True)
    e = jnp.exp(v2 - v1)
    w1 = gprob / (1.0 + e)
    w2 = w1 * e
    r_ref[...] = jnp.where(lane == 0, i1 - N_GROUPS,
                           jnp.where(lane == 1, i2 - N_GROUPS,
                                     jnp.where(lane == 2, w1, jnp.where(lane == 3, w2, 0.0))))


def _router(x, g, w, b):
    T = x.shape[0]
    tm = min(512, T)
    xspec = pl.BlockSpec((tm, D_MODEL), lambda i: (i, 0))
    const = lambda shape: pl.BlockSpec(shape, lambda i: (0,) * len(shape))
    return pl.pallas_call(
        _router_kernel, grid=(T // tm,),
        in_specs=[xspec, const((1, D_MODEL)), const((D_MODEL, LANES)), const((1, LANES))],
        out_specs=[xspec, pl.BlockSpec((tm, LANES), lambda i: (i, 0))],
        out_shape=[jax.ShapeDtypeStruct((T, D_MODEL), F32), jax.ShapeDtypeStruct((T, LANES), F32)],
        compiler_params=_cparams(("parallel",)), name="router")(x, g, w, b)


MOE_ROWS = 512


def _moe_kernel(be_ref, nu_ref, idx_hbm, xn_hbm, wg0, wu0, wd0, wg1, wu1, wd1, out_hbm,
                idx, xbuf, obuf, isem, gsem, ssem):
    R = MOE_ROWS
    j = pl.program_id(0)
    nu = nu_ref[0]
    a = 2 * j
    b = a + 1

    def idx_copy(blk, slot):
        return pltpu.make_async_copy(idx_hbm.at[blk], idx.at[slot], isem.at[slot])

    def issue_gathers(slot):
        for r in range(R):
            pltpu.make_async_copy(xn_hbm.at[pl.ds(idx[slot, r], 1)], xbuf.at[slot, pl.ds(r, 1)],
                                  gsem.at[slot]).start()

    def wait_gathers(slot):
        pltpu.make_async_copy(xn_hbm.at[pl.ds(0, R)], xbuf.at[slot], gsem.at[slot]).wait()

    def issue_scatters(slot):
        for r in range(R):
            pltpu.make_async_copy(obuf.at[slot, pl.ds(r, 1)], out_hbm.at[pl.ds(idx[slot, R + r], 1)],
                                  ssem.at[slot]).start()

    def wait_scatters(slot):
        pltpu.make_async_copy(obuf.at[slot], out_hbm.at[pl.ds(0, R)], ssem.at[slot]).wait()

    def compute(slot, wg, wu, wd):
        xb = xbuf[slot].astype(BF16)
        h = jax.nn.silu(jnp.dot(xb, wg[0], preferred_element_type=F32)) \
            * jnp.dot(xb, wu[0], preferred_element_type=F32)
        obuf[slot] = jnp.dot(h.astype(BF16), wd[0], preferred_element_type=F32)

    @pl.when(j == 0)
    def _():
        obuf[0] = jnp.zeros((R, D_MODEL), F32)
        n_rows = out_hbm.shape[0]
        for k in (2, 1):
            z = pltpu.make_async_copy(obuf.at[0], out_hbm.at[pl.ds(n_rows - k * R, R)], ssem.at[0])
            z.start()
            z.wait()
        c = idx_copy(0, 0)
        c.start()
        c.wait()
        issue_gathers(0)

        @pl.when(1 < nu)
        def _():
            idx_copy(1, 1).start()

    @pl.when(a < nu)
    def _():
        wait_gathers(0)

        @pl.when(b < nu)
        def _():
            idx_copy(b, 1).wait()
            issue_gathers(1)

        @pl.when(j >= 1)
        def _():
            wait_scatters(0)

        compute(0, wg0, wu0, wd0)
        issue_scatters(0)

        @pl.when(a + 2 < nu)
        def _():
            idx_copy(a + 2, 0).start()

    @pl.when(b < nu)
    def _():
        wait_gathers(1)

        @pl.when(b + 1 < nu)
        def _():
            idx_copy(b + 1, 0).wait()
            issue_gathers(0)

        @pl.when(j >= 1)
        def _():
            wait_scatters(1)

        compute(1, wg1, wu1, wd1)
        issue_scatters(1)

        @pl.when(b + 2 < nu)
        def _():
            idx_copy(b + 2, 1).start()

    @pl.when((a < nu) & (a + 2 >= nu))
    def _():
        wait_scatters(0)

        @pl.when((b < nu) | (j >= 1))
        def _():
            wait_scatters(1)


def _moe_experts(xn, idx, block_e, n_used, wg, wu, wd):
    T = xn.shape[0]
    R = MOE_ROWS
    n_blocks = idx.shape[0]
    assert n_blocks % 2 == 0
    wspec = lambda o, shp: pl.BlockSpec(shp, lambda j, be, nu: (be[2 * j + o], 0, 0))
    w_in, w_out = (1, D_MODEL, EXPERT_HIDDEN), (1, EXPERT_HIDDEN, D_MODEL)
    grid_spec = pltpu.PrefetchScalarGridSpec(
        num_scalar_prefetch=2, grid=(n_blocks // 2,),
        in_specs=[pl.BlockSpec(memory_space=pl.ANY), pl.BlockSpec(memory_space=pl.ANY),
                  wspec(0, w_in), wspec(0, w_in), wspec(0, w_out),
                  wspec(1, w_in), wspec(1, w_in), wspec(1, w_out)],
        out_specs=pl.BlockSpec(memory_space=pl.ANY),
        scratch_shapes=[pltpu.SMEM((2, 2 * R), jnp.int32), pltpu.VMEM((2, R, D_MODEL), F32),
                        pltpu.VMEM((2, R, D_MODEL), F32), pltpu.SemaphoreType.DMA((2,)),
                        pltpu.SemaphoreType.DMA((2,)), pltpu.SemaphoreType.DMA((2,))])
    return pl.pallas_call(
        _moe_kernel, grid_spec=grid_spec, out_shape=jax.ShapeDtypeStruct((TOP_K * T + 2 * R, D_MODEL), F32),
        compiler_params=_cparams(("arbitrary",)), name="moe_experts")(
            block_e, n_used, idx, xn, wg, wu, wd, wg, wu, wd)


def _dispatch(route, T):
    R = MOE_ROWS
    M = T * TOP_K
    expert_id = route[:, :TOP_K].astype(jnp.int32).reshape(-1)
    order = jnp.argsort(expert_id).astype(jnp.int32)
    counts =jnp.bincount(expert_id, length=N_EXPERTS).astype(jnp.int32)
    starts = jnp.cumsum(counts) - counts
    padded = (counts + R - 1) // R * R
    p_ends = jnp.cumsum(padded)
    p_starts = p_ends - padded
    n_blocks = -(-M // R) + N_EXPERTS
    block_e = jnp.minimum(jnp.searchsorted(p_ends, jnp.arange(n_blocks, dtype=jnp.int32) * R, side='right'),
                          N_EXPERTS - 1).astype(jnp.int32)
    row = jnp.arange(n_blocks * R, dtype=jnp.int32).reshape(n_blocks, R)
    k = row - p_starts[block_e][:, None]
    valid = (k < counts[block_e][:, None]) & (row < p_ends[N_EXPERTS - 1])
    a = order[jnp.clip(starts[block_e][:, None] + k, 0, M - 1)]
    token = a >> 1
    src = jnp.where(valid, token, 0)
    scrap = M + row % (2 * R)
    dst = jnp.where(valid, (a & 1) * T + token, scrap)
    n_used = (p_ends[N_EXPERTS - 1:] // R).astype(jnp.int32)
    return jnp.concatenate([src, dst], axis=1), block_e, n_used


def _final_kernel(x_ref, e0_ref, e1_ref, r_ref, g_ref, o_ref):
    r = r_ref[...]
    x = x_ref[...] + e0_ref[...] * r[:, 2:3] + e1_ref[...] * r[:, 3:4]
    o_ref[...] = _rms(x, g_ref[...])


def _final(x, e2, route, g, row0, rows):
    T = x.shape[0]
    tm = min(512, rows)
    nT = T // tm
    b0 = row0 // tm
    xspec = pl.BlockSpec((tm, D_MODEL), lambda i: (b0 + i, 0))
    return pl.pallas_call(
        _final_kernel, grid=(rows // tm,),
        in_specs=[xspec, xspec, pl.BlockSpec((tm, D_MODEL), lambda i: (nT + b0 + i, 0)),
                  pl.BlockSpec((tm, LANES), lambda i: (b0 + i, 0)), pl.BlockSpec((1, D_MODEL), lambda i: (0, 0))],
        out_specs=pl.BlockSpec((tm, D_MODEL), lambda i: (i, 0)),
        out_shape=jax.ShapeDtypeStruct((rows, D_MODEL), F32),
        compiler_params=_cparams(("parallel",)), name="combine_final_norm")(x, e2, e2, route, g)


def _pack_layer(l, p):
    w = p['w_in'][l]
    o_gate = 3 * HG_WIDTH + 2 * HG_WIDTH + 4 * ML_WIDTH
    o_hy = o_gate + 4 * ML_HEADS
    o_g = o_hy + 3 * HY_WIDTH
    gates = jnp.pad(w[:, o_gate:o_hy], ((0, 0), (0, PROJ_W - PROJ_MAIN - 4 * ML_HEADS)))
    w_proj = jnp.concatenate([w[:, :o_gate], w[:, o_hy:o_g], gates], axis=1).astype(BF16)
    w_gate = w[:, o_g:].astype(BF16)
    gate_bias = jnp.pad(jnp.concatenate([p['ml_i_bias'][l].reshape(-1), p['ml_f_bias'][l].reshape(-1)]),
                        (0, LANES - 4 * ML_HEADS)).astype(F32)[None]
    w_router = jnp.pad(jnp.concatenate([p['router_group_w'][l], p['router_expert_w'][l]], axis=1),
                       ((0, 0), (0, LANES - N_GROUPS - N_EXPERTS))).astype(F32)
    b_router = jnp.pad(jnp.concatenate([p['router_group_b'][l], p['router_expert_b'][l]]),
                       (0, LANES - N_GROUPS - N_EXPERTS)).astype(F32)[None]
    return dict(
        norm_mix_g=p['norm_mix_g'][l].astype(F32)[None], w_proj=w_proj, w_gate=w_gate,
        hg_norm_g=p['hg_norm_g'][l].astype(F32)[None], gate_bias=gate_bias,
        ml_norm_g=p['ml_norm_g'][l].astype(F32)[None],
        conv_w=jnp.pad(p['hy_conv_w'][l].astype(F32), ((0, 5), (0, 0))), conv_b=p['hy_conv_b'][l].astype(F32)[None],
        filt=(p['filt_w1'][l], p['filt_b1'][l], p['filt_w2'][l], p['filt_b2'][l], p['filt_freq'][l],
              p['filt_w3'][l]),
        hy_bias=p['hy_bias'][l].astype(F32)[None],
        wa=p['w_branch_a'][l].astype(BF16), wb=p['w_branch_b'][l].astype(BF16),
        wc=p['w_branch_c'][l].astype(BF16), wo=p['w_out'][l].astype(BF16),
        norm_ffn_g=p['norm_ffn_g'][l].astype(F32)[None], w_router=w_router, b_router=b_router,
        wg=p['exp_w_gate'][l].astype(BF16), wu=p['exp_w_up'][l].astype(BF16),
        wd=p['exp_w_down'][l].astype(BF16))


def _trunk(x, groups, p):
    T = x.shape[0]
    lbs = jnp.cumsum(jax.nn.softmax(p['hg_lb'].astype(F32), axis=0), axis=0)
    lbs = lbs - lbs[0:1]
    moe = None
    for l in range(DEPTH):
        lp = _pack_layer(l, p)
        lb = lbs[l][None]
        lbrow = jnp.concatenate([jnp.log(lb), jnp.log1p(-lb), 1.0 - lb, jnp.zeros((5, HG_WIDTH), F32)], axis=0)
        if moe is None:
            proj = _in_projection(x, lp['norm_mix_g'], lp['w_proj'])
        else:
            proj, x = _in_projection(x, lp['norm_mix_g'], lp['w_proj'], moe)
        ya, yb, yc = [], [], []
        for row0, B, L in groups:
            ya.append(_hgrn2(proj, lbrow, lp['hg_norm_g'], row0, B, L))
            yb.append(_mlstm(proj, lp['gate_bias'], lp['ml_norm_g'], row0, B, L))
            yc.append(_hyena(proj, lp['conv_w'], lp['conv_b'], lp['filt'], lp['hy_bias'], row0, B, L))
        ya, yb, yc = (jnp.concatenate(v, axis=0) for v in (ya, yb, yc))
        x = _merge(x, lp['norm_mix_g'], lp['w_gate'], ya, yb, yc, lp['wa'], lp['wb'], lp['wc'], lp['wo'])
        xn, route = _router(x, lp['norm_ffn_g'], lp['w_router'], lp['b_router'])
        rows, block_e, n_used = _dispatch(route, T)
        e2 = _moe_experts(xn, rows, block_e, n_used, lp['wg'], lp['wu'], lp['wd'])
        moe = (e2, route)
    g = p['final_norm_g'].astype(F32)[None]
    return [_final(x, moe[0], moe[1], g, row0, B * L) for row0, B, L in groups]


def kernel(x_prompt, x_sample, norm_mix_g, w_in, hg_lb, hg_norm_g, ml_i_bias, ml_f_bias, ml_norm_g, hy_conv_w, hy_conv_b, filt_w1, filt_b1, filt_w2, filt_b2, filt_freq, filt_w3, hy_bias, w_branch_a, w_branch_b, w_branch_c, w_out, norm_ffn_g, router_group_w, router_group_b, router_expert_w, router_expert_b, exp_w_gate, exp_w_up, exp_w_down, final_norm_g):
    p = dict(norm_mix_g=norm_mix_g, w_in=w_in, hg_lb=hg_lb, hg_norm_g=hg_norm_g, ml_i_bias=ml_i_bias,
             ml_f_bias=ml_f_bias, ml_norm_g=ml_norm_g, hy_conv_w=hy_conv_w, hy_conv_b=hy_conv_b,
             filt_w1=filt_w1, filt_b1=filt_b1, filt_w2=filt_w2, filt_b2=filt_b2, filt_freq=filt_freq,
             filt_w3=filt_w3, hy_bias=hy_bias, w_branch_a=w_branch_a, w_branch_b=w_branch_b,
             w_branch_c=w_branch_c, w_out=w_out, norm_ffn_g=norm_ffn_g, router_group_w=router_group_w,
             router_group_b=router_group_b, router_expert_w=router_expert_w,
             router_expert_b=router_expert_b, exp_w_gate=exp_w_gate, exp_w_up=exp_w_up,
             exp_w_down=exp_w_down, final_norm_g=final_norm_g)
    Bp, Lp, _ = x_prompt.shape
    Bs, Ls, _ = x_sample.shape
    Tp, Ts = Bp * Lp, Bs * Ls
    x = jnp.concatenate([x_prompt.reshape(Tp, D_MODEL), x_sample.reshape(Ts, D_MODEL)], axis=0).astype(F32)
    yp, ys = _trunk(x, ((0, Bp, Lp), (Tp, Bs, Ls)), p)
    return (yp.reshape(Bp, Lp, D_MODEL), ys.reshape(Bs, Ls, D_MODEL))
```

```python
import functools
import math

import numpy as np
import jax
import jax.numpy as jnp
from jax import lax
from jax.experimental import pallas as pl
from jax.experimental.pallas import tpu as pltpu

F32 = jnp.float32
BF16 = jnp.bfloat16
HIGHEST = lax.Precision.HIGHEST

D_MODEL = 1024
EPS = 1e-6
DEPTH = 2
HG_HEADS, HG_KDIM, HG_WIDTH, HG_CHUNK = 8, 64, 512, 32
ML_HEADS, ML_HDIM, ML_WIDTH, ML_CHUNK = 4, 128, 512, 128
HY_WIDTH, HY_BANDS, HY_FILTER_HIDDEN = 512, 16, 64
HY_EMB = 1 + 2 * HY_BANDS
HY_SHORT_DECAY_PCT, HY_LONG_DECAY_PCT, HY_DECAY_TARGET = 0.3, 1.5, 1e-2
N_GROUPS, EXPERTS_PER_GROUP, TOP_K = 4, 8, 2
N_EXPERTS = N_GROUPS * EXPERTS_PER_GROUP
N_BRANCHES = 3
EXPERT_HIDDEN = D_MODEL // 2
MOE_BLOCK = 128

LANES = 128
BRANCH_W = 512
SLAB_HQ, SLAB_HFF, SLAB_HFB, SLAB_HI, SLAB_HG = 0, 1, 2, 3, 4
SLAB_MQ, SLAB_MK, SLAB_MV, SLAB_MO = 5, 6, 7, 8
SLAB_HY = 9
PROJ_MAIN = 12 * BRANCH_W
PROJ_W = PROJ_MAIN + 2 * LANES
PROJ_TN = 1280
GATE_BLOCK = PROJ_MAIN // LANES
VMEM_LIMIT = 48 * 1024 * 1024
VMEM_LIMIT_WIDE = 56 * 1024 * 1024


def _cparams(sem, vmem=VMEM_LIMIT):
    return pltpu.CompilerParams(dimension_semantics=sem, vmem_limit_bytes=vmem)


def _rms(x, g):
    return x * lax.rsqrt(jnp.mean(x * x, axis=-1, keepdims=True) + EPS) * g


def _log_sigmoid(z):
    return jnp.minimum(z, 0.0) - jnp.log1p(jnp.exp(-jnp.abs(z)))


def _bf16_terms(x):
    t1 = x.astype(BF16)
    r = x - t1.astype(F32)
    t2 = r.astype(BF16)
    return t1, t2, (r - t2.astype(F32)).astype(BF16)


def _inproj_kernel(x_ref, g_ref, w_ref, o_ref, xn_ref):
    @pl.when(pl.program_id(1) == 0)
    def _():
        xn_ref[...] = _rms(x_ref[...], g_ref[...]).astype(BF16)

    o_ref[...] = jnp.dot(xn_ref[...], w_ref[...], preferred_element_type=F32)


def _inproj_combine_kernel(x_ref, e0_ref, e1_ref, r_ref, g_ref, w_ref, o_ref, xo_ref, xn_ref):
    @pl.when(pl.program_id(1) == 0)
    def _():
        r = r_ref[...]
        x = x_ref[...] + e0_ref[...] * r[:, 2:3] + e1_ref[...] * r[:, 3:4]
        xo_ref[...] = x
        xn_ref[...] = _rms(x, g_ref[...]).astype(BF16)

    o_ref[...] = jnp.dot(xn_ref[...], w_ref[...], preferred_element_type=F32)


def _in_projection(x, g, w, moe=None):
    T = x.shape[0]
    tm = min(1024, T)
    tn = PROJ_TN
    grid = (T // tm, PROJ_W // tn)
    xspec = pl.BlockSpec((tm, D_MODEL), lambda i, j: (i, 0))
    gspec = pl.BlockSpec((1, D_MODEL), lambda i, j: (0, 0))
    wspec = pl.BlockSpec((D_MODEL, tn), lambda i, j: (0, j))
    ospec = pl.BlockSpec((tm, tn), lambda i, j: (i, j))
    scratch = [pltpu.VMEM((tm, D_MODEL), BF16)]
    cp = _cparams(("parallel", "arbitrary"))
    if moe is None:
        return pl.pallas_call(
            _inproj_kernel, grid=grid, in_specs=[xspec, gspec, wspec], out_specs=ospec,
            out_shape=jax.ShapeDtypeStruct((T, PROJ_W), F32), scratch_shapes=scratch,
            compiler_params=cp, name="in_projection")(x, g, w)
    e2, route = moe
    nT = T // tm
    e0spec = pl.BlockSpec((tm, D_MODEL), lambda i, j: (i, 0))
    e1spec = pl.BlockSpec((tm, D_MODEL), lambda i, j: (nT + i, 0))
    rspec = pl.BlockSpec((tm, LANES), lambda i, j: (i, 0))
    proj, xo = pl.pallas_call(
        _inproj_combine_kernel, grid=grid, in_specs=[xspec, e0spec, e1spec, rspec, gspec, wspec],
        out_specs=[ospec, xspec],
        out_shape=[jax.ShapeDtypeStruct((T, PROJ_W), F32), jax.ShapeDtypeStruct((T, D_MODEL), F32)],
        scratch_shapes=scratch, compiler_params=_cparams(("parallel", "arbitrary"), VMEM_LIMIT_WIDE),
        name="in_projection_combine")(x, e2, e2, route, g, w)
    return proj, xo


HG_TB = 256
HG_PAIRS = HG_WIDTH // LANES


HG_UNROLL = 8
HG_SUB = 8
HG_NSUB = HG_CHUNK // HG_SUB
HG_XROWS = (HG_NSUB - 1) * HG_HEADS * HG_SUB
HG_XCOLS = HG_SUB * HG_NSUB * (HG_NSUB - 1) // 2


def _hgrn2_tiles(reverse):
    if reverse:
        return [(i, (i + 1) * HG_SUB, HG_CHUNK - (i + 1) * HG_SUB) for i in range(HG_NSUB - 1)]
    return [(i, 0, i * HG_SUB) for i in range(1, HG_NSUB)]


def _hgrn2_kernel(*refs, reverse, finalize):
    if finalize:
        (q_ref, z_ref, v_ref, lb_ref, tri_ref, bd_ref, xm_ref, of_ref, gate_ref, ng_ref,
         o_ref, st_ref) = refs
    else:
        (q_ref, z_ref, v_ref, lb_ref, tri_ref, bd_ref, xm_ref, o_ref, st_ref) = refs
    C, c, nb, W = HG_CHUNK, HG_SUB, HG_NSUB, HG_WIDTH
    n_chunks = HG_TB // C

    @pl.when(pl.program_id(1) == 0)
    def _():
        st_ref[...] = jnp.zeros_like(st_ref)

    log_lb = lb_ref[0:1, :]
    log_1mlb = lb_ref[1:2, :]
    one_m_lb = lb_ref[2:3, :]
    tri3 = tri_ref[...]
    bd = bd_ref[...]
    bd_bf = bd.astype(BF16)
    xmask = xm_ref[...]
    sub_row = lax.broadcasted_iota(jnp.int32, (nb, c, W), 1)
    lane_head = lax.broadcasted_iota(jnp.int32, (c, W), 1) // HG_KDIM
    tiles = _hgrn2_tiles(reverse)

    def chunk(i):
        ci = (n_chunks - 1 - i) if reverse else i
        off = pl.multiple_of(ci * C, C)
        q = q_ref[pl.ds(off, C), :]
        z = z_ref[pl.ds(off, C), :]
        v = v_ref[pl.ds(off, C), :]
        ls = jnp.minimum(z, 0.0) - jnp.log(1.0 + jnp.exp(-jnp.abs(z)))
        hi = log_1mlb + ls
        mx = jnp.maximum(log_lb, hi)
        lf = mx + jnp.log(1.0 + jnp.exp(-jnp.abs(log_lb - hi)))
        kk = one_m_lb * jnp.exp(ls - z)
        b = jnp.dot(tri3, jnp.concatenate(_bf16_terms(lf), axis=0), preferred_element_type=F32)
        btot = b[0:1, :] if reverse else b[C - 1:C, :]
        qb = (q * jnp.exp(b)).astype(BF16)
        kb = (kk * jnp.exp(btot - b)).astype(BF16)
        dec = jnp.exp(btot)
        vb = v.astype(BF16)

        q3, k3, v3 = (a.reshape(nb, c, W) for a in (q, kk, v))
        one = 1 if not reverse else c - 1
        f3 = (1.0 - kk).reshape(nb, c, W)
        kd = k3
        a_rows = [(q * kk).astype(BF16)]
        for d in range(1, c):
            ok = (sub_row < c - d) if reverse else (sub_row >= d)
            kd = pltpu.roll(kd, one, axis=1) * f3
            a_rows.append(jnp.where(ok, q3 * kd, 0.0).reshape(C, W).astype(BF16))
        a_all = jnp.concatenate(a_rows, axis=0)
        sums = jnp.concatenate(
            [jnp.dot(a_all[:, p * LANES:(p + 1) * LANES], bd_bf, preferred_element_type=F32)
             for p in range(HG_PAIRS)], axis=1)
        acc3 = sums[0:C].reshape(nb, c, W) * v3
        v_sh = v3
        for d in range(1, c):
            v_sh = pltpu.roll(v_sh, one, axis=1)
            acc3 = acc3 + sums[d * C:(d + 1) * C].reshape(nb, c, W) * v_sh
        acc = acc3.reshape(C, W)

        qx, kx, vx = [], [], []
        for (ti, s0, sn) in tiles:
            edge = s0 if reverse else s0 + sn - 1
            r = b[edge:edge + 1, :]
            rows = slice(ti * c, (ti + 1) * c)
            qh = q[rows] * jnp.exp(b[rows] - r)
            qx += [jnp.where(lane_head == h, qh, 0.0) for h in range(HG_HEADS)]
            kx.append(kk[s0:s0 + sn] * jnp.exp(r - b[s0:s0 + sn]))
            vx.append(v[s0:s0 + sn])
        qx = jnp.concatenate(qx, axis=0).astype(BF16)
        kx = jnp.concatenate(kx, axis=0).astype(BF16)
        vx = jnp.concatenate(vx, axis=0).astype(BF16)
        sc = lax.dot_general(qx, kx, (((1,), (1,)), ((), ())), preferred_element_type=F32) * xmask
        px = jnp.dot(sc.astype(BF16), vx, preferred_element_type=F32)
        offd = {}
        for n, (ti, s0, sn) in enumerate(tiles):
            base = n * HG_HEADS * c
            t_acc = jnp.where(lane_head == 0, px[base:base + c], 0.0)
            for h in range(1, HG_HEADS):
                t_acc = t_acc + jnp.where(lane_head == h, px[base + h * c:base + (h + 1) * c], 0.0)
            offd[ti] = t_acc
        acc = acc + jnp.concatenate([offd.get(ti, jnp.zeros((c, W), F32)) for ti in range(nb)], axis=0)

        parts = []
        for p in range(HG_PAIRS):
            sl = slice(p * LANES, (p + 1) * LANES)
            s_t = st_ref[p]
            parts.append(lax.dot_general(qb[:, sl], s_t.astype(BF16), (((1,), (1,)), ((), ())),
                                         preferred_element_type=F32))
            upd = lax.dot_general(vb[:, sl], kb[:, sl], (((0,), (0,)), ((), ())),
                                  preferred_element_type=F32)
            st_ref[p] = s_t * dec[:, sl] + upd * bd
        acc = acc + jnp.concatenate(parts, axis=1)

        if finalize:
            o = acc + of_ref[pl.ds(off, C), :]
            o2 = (o * o).astype(BF16)
            ms = jnp.concatenate(
                [jnp.dot(o2[:, p * LANES:(p + 1) * LANES], bd_bf, preferred_element_type=F32)
                 for p in range(HG_PAIRS)], axis=1) * (1.0 / HG_KDIM)
            g = gate_ref[pl.ds(off, C), :]
            y = o * lax.rsqrt(ms + EPS) * ng_ref[...] * (g * jax.nn.sigmoid(g))
            o_ref[pl.ds(off, C), :] = y.astype(o_ref.dtype)
        else:
            o_ref[pl.ds(off, C), :] = acc

    def chunk_group(j, carry):
        for u in range(HG_UNROLL):
            chunk(HG_UNROLL * j + u)
        return carry

    lax.fori_loop(0, n_chunks // HG_UNROLL, chunk_group, 0)


def _hgrn2(proj, lbrow, norm_g, row0, B, L):
    C = HG_CHUNK
    nT = L // HG_TB
    blk0 = row0 // HG_TB
    tri_f = jnp.asarray(np.tile(np.tril(np.ones((C, C), np.float32)), (1, 3))).astype(BF16)
    tri_b = jnp.asarray(np.tile(np.triu(np.ones((C, C), np.float32)), (1, 3))).astype(BF16)
    head = np.arange(LANES) // HG_KDIM
    bd = jnp.asarray((head[:, None] == head[None, :]).astype(np.float32))

    def tile_mask(reverse):
        m = np.zeros((HG_XROWS, HG_XCOLS), np.float32)
        col = 0
        for n, (_, _, sn) in enumerate(_hgrn2_tiles(reverse)):
            m[n * HG_HEADS * HG_SUB:(n + 1) * HG_HEADS * HG_SUB, col:col + sn] = 1.0
            col += sn
        return jnp.asarray(m)

    def in_spec(slab, reverse):
        if reverse:
            return pl.BlockSpec((HG_TB, BRANCH_W), lambda b, t: (blk0 + b * nT + nT - 1 - t, slab))
        return pl.BlockSpec((HG_TB, BRANCH_W), lambda b, t: (blk0 + b * nT + t, slab))

    def out_spec(reverse):
        if reverse:
            return pl.BlockSpec((HG_TB, BRANCH_W), lambda b, t: (b * nT + nT - 1 - t, 0))
        return pl.BlockSpec((HG_TB, BRANCH_W), lambda b, t: (b * nT + t, 0))

    const = lambda shape: pl.BlockSpec(shape, lambda b, t: (0,) * len(shape))
    scratch = [pltpu.VMEM((HG_PAIRS, LANES, LANES), F32)]
    consts = [const((8, BRANCH_W)), const((C, 3 * C)), const((LANES, LANES)), const((HG_XROWS, HG_XCOLS))]
    cp = _cparams(("parallel", "arbitrary"))
    o_f = pl.pallas_call(
        functools.partial(_hgrn2_kernel, reverse=False, finalize=False), grid=(B, nT),
        in_specs=[in_spec(SLAB_HQ, False), in_spec(SLAB_HFF, False), in_spec(SLAB_HI, False)] + consts,
        out_specs=out_spec(False), out_shape=jax.ShapeDtypeStruct((B * L, BRANCH_W), F32),
        scratch_shapes=scratch, compiler_params=cp, name="hgrn2_fwd")(
            proj, proj, proj, lbrow, tri_f, bd, tile_mask(False))
    return pl.pallas_call(
        functools.partial(_hgrn2_kernel, reverse=True, finalize=True), grid=(B, nT),
        in_specs=[in_spec(SLAB_HQ, True), in_spec(SLAB_HFB, True), in_spec(SLAB_HI, True)] + consts
        + [out_spec(True), in_spec(SLAB_HG, True), const((1, BRANCH_W))],
        out_specs=out_spec(True), out_shape=jax.ShapeDtypeStruct((B * L, BRANCH_W), BF16),
        scratch_shapes=scratch, compiler_params=cp, name="hgrn2_bwd")(
            proj, proj, proj, lbrow, tri_b, bd, tile_mask(True), o_f, proj, norm_g)


def _mlstm_kernel(*refs, reverse, finalize):
    if finalize:
        (q_ref, k_ref, v_ref, gt_ref, gb_ref, tri_ref, hf_ref, og_ref, ng_ref,
         o_ref, c_ref, n_ref, m_ref) = refs
    else:
        (q_ref, k_ref, v_ref, gt_ref, gb_ref, tri_ref, o_ref, c_ref, n_ref, m_ref) = refs
    C = ML_CHUNK

    @pl.when(pl.program_id(1) == 0)
    def _():
        c_ref[...] = jnp.zeros_like(c_ref)
        n_ref[...] = jnp.zeros_like(n_ref)
        m_ref[...] = jnp.zeros_like(m_ref)

    tri = tri_ref[...]
    lane = lax.broadcasted_iota(jnp.int32, (C, LANES), 1)
    ti = lax.broadcasted_iota(jnp.int32, (C, C), 0)
    si = lax.broadcasted_iota(jnp.int32, (C, C), 1)
    causal = (si >= ti) if reverse else (si <= ti)
    dsel = ML_HEADS if reverse else 0
    edge = 0 if reverse else C - 1
    scale = 1.0 / math.sqrt(ML_HDIM)
    order = range(ML_TB // C - 1, -1, -1) if reverse else range(ML_TB // C)
    for ci in order:
        _mlstm_chunk(refs, slice(ci * C, (ci + 1) * C), tri, lane, causal, dsel, edge, scale, finalize)


def _mlstm_chunk(refs, rws, tri, lane, causal, dsel, edge, scale, finalize):
    if finalize:
        (q_ref, k_ref, v_ref, gt_ref, gb_ref, tri_ref, hf_ref, og_ref, ng_ref,
         o_ref, c_ref, n_ref, m_ref) = refs
    else:
        (q_ref, k_ref, v_ref, gt_ref, gb_ref, tri_ref, o_ref, c_ref, n_ref, m_ref) = refs
    gates = gt_ref[rws, :] + gb_ref[...]
    g_col = jnp.where(lane >= 2 * ML_HEADS, _log_sigmoid(gates), gates)
    g_row = g_col.T
    b_col = jnp.dot(tri, g_col, precision=HIGHEST, preferred_element_type=F32)
    b_row = lax.dot_general(g_row, tri, (((1,), (1,)), ((), ())), precision=HIGHEST,
                            preferred_element_type=F32)

    outs = []
    for h in range(ML_HEADS):
        sl = slice(h * ML_HDIM, (h + 1) * ML_HDIM)
        li, lf = dsel + h, 2 * ML_HEADS + dsel + h
        q = q_ref[rws, sl]
        kc = k_ref[rws, sl] * scale
        v = v_ref[rws, sl]
        qb, kb, vb = q.astype(BF16), kc.astype(BF16), v.astype(BF16)
        bc = b_col[:, lf:lf + 1]
        br = b_row[lf:lf + 1, :]
        ic = g_col[:, li:li + 1]
        ir = g_row[li:li + 1, :]
        btot = b_col[edge:edge + 1, lf:lf + 1]
        m_prev = m_ref[0:1, h:h + 1]
        n_prev = n_ref[h:h + 1, :]
        c_prev = c_ref[h]

        log_d = jnp.where(causal, bc - br + ir, -jnp.inf)
        log_inter = bc + m_prev
        m_t = jnp.maximum(log_inter, jnp.max(log_d, axis=1, keepdims=True))
        a_inter = jnp.exp(log_inter - m_t)
        s = lax.dot_general(qb, kb, (((1,), (1,)), ((), ())), preferred_element_type=F32)
        s = s * jnp.exp(log_d - m_t)
        num = a_inter * jnp.dot(qb, c_prev.astype(BF16), preferred_element_type=F32) \
            + jnp.dot(s.astype(BF16), vb, preferred_element_type=F32)
        den = a_inter * jnp.sum(q * n_prev, axis=1, keepdims=True) + jnp.sum(s, axis=1, keepdims=True)
        outs.append(num / jnp.maximum(jnp.abs(den), jnp.exp(-m_t)))

        log_w = btot - bc + ic
        m_new = jnp.maximum(btot + m_prev, jnp.max(log_w, axis=0, keepdims=True))
        a_state = jnp.exp(btot + m_prev - m_new)
        kw = kc * jnp.exp(log_w - m_new)
        c_ref[h] = a_state * c_prev + lax.dot_general(kw.astype(BF16), vb, (((0,), (0,)), ((), ())),
                                                      preferred_element_type=F32)
        n_ref[h:h + 1, :] = a_state * n_prev + jnp.sum(kw, axis=0, keepdims=True)
        m_ref[0:1, h:h + 1] = m_new

    hcur = jnp.concatenate(outs, axis=1)
    if finalize:
        hsum = hcur + hf_ref[rws, :]
        ys = []
        for h in range(ML_HEADS):
            sl = slice(h * ML_HDIM, (h + 1) * ML_HDIM)
            hh = hsum[:, sl]
            ys.append(hh * lax.rsqrt(jnp.mean(hh * hh, axis=1, keepdims=True) + EPS))
        y = jnp.concatenate(ys, axis=1) * ng_ref[...] * jax.nn.sigmoid(og_ref[rws, :])
        o_ref[rws, :] = y.astype(o_ref.dtype)
    else:
        o_ref[rws, :] = hcur


ML_TB = 2 * ML_CHUNK


def _mlstm(proj, gate_bias, norm_g, row0, B, L):
    C = ML_TB
    nT = L // C
    blk0 = row0 // C
    tri_f = jnp.asarray(np.tril(np.ones((ML_CHUNK, ML_CHUNK), np.float32)))
    tri_b = jnp.asarray(np.triu(np.ones((ML_CHUNK, ML_CHUNK), np.float32)))

    def rows(reverse):
        if reverse:
            return lambda b, t: blk0 + b * nT + nT - 1 - t
        return lambda b, t: blk0 + b * nT + t

    def in_spec(slab, reverse):
        r = rows(reverse)
        return pl.BlockSpec((C, BRANCH_W), lambda b, t: (r(b, t), slab))

    def gate_spec(reverse):
        r = rows(reverse)
        return pl.BlockSpec((C, LANES), lambda b, t: (r(b, t), GATE_BLOCK))

    def out_spec(reverse):
        if reverse:
            return pl.BlockSpec((C, BRANCH_W), lambda b, t: (b * nT + nT - 1 - t, 0))
        return pl.BlockSpec((C, BRANCH_W), lambda b, t: (b * nT + t, 0))

    const = lambda shape: pl.BlockSpec(shape, lambda b, t: (0,) * len(shape))
    scratch = [pltpu.VMEM((ML_HEADS, ML_HDIM, ML_HDIM), F32), pltpu.VMEM((8, ML_HDIM), F32),
               pltpu.VMEM((8, LANES), F32)]
    cp = _cparams(("parallel", "arbitrary"))
    h_f = pl.pallas_call(
        functools.partial(_mlstm_kernel, reverse=False, finalize=False), grid=(B, nT),
        in_specs=[in_spec(SLAB_MQ, False), in_spec(SLAB_MK, False), in_spec(SLAB_MV, False),
                  gate_spec(False), const((1, LANES)), const((ML_CHUNK, ML_CHUNK))],
        out_specs=out_spec(False), out_shape=jax.ShapeDtypeStruct((B * L, BRANCH_W), F32),
        scratch_shapes=scratch, compiler_params=cp, name="mlstm_fwd")(
            proj, proj, proj, proj, gate_bias, tri_f)
    return pl.pallas_call(
        functools.partial(_mlstm_kernel, reverse=True, finalize=True), grid=(B, nT),
        in_specs=[in_spec(SLAB_MQ, True), in_spec(SLAB_MK, True), in_spec(SLAB_MV, True),
                  gate_spec(True), const((1, LANES)), const((ML_CHUNK, ML_CHUNK)),
                  out_spec(True), in_spec(SLAB_MO, True), const((1, BRANCH_W))],
        out_specs=out_spec(True), out_shape=jax.ShapeDtypeStruct((B * L, BRANCH_W), BF16),
        scratch_shapes=scratch, compiler_params=cp, name="mlstm_bwd")(
            proj, proj, proj, proj, gate_bias, tri_b, h_f, proj, norm_g)


HY_TB = 256


def _shortconv_kernel(c_ref, p_ref, n_ref, w_ref, b_ref, x0_ref, u_ref, *, nT):
    t = pl.program_id(1)
    cur = c_ref[...]
    prev_row = jnp.where(t > 0, p_ref[7:8, :], 0.0)
    next_row = jnp.where(t < nT - 1, n_ref[0:1, :], 0.0)
    row = lax.broadcasted_iota(jnp.int32, cur.shape, 0)
    up = jnp.where(row == 0, prev_row, pltpu.roll(cur, 1, axis=0))
    dn = jnp.where(row == HY_TB - 1, next_row, pltpu.roll(cur, HY_TB - 1, axis=0))
    y = up * w_ref[0:1, :] + cur * w_ref[1:2, :] + dn * w_ref[2:3, :] + b_ref[...]
    x0_ref[...] = y[:, :HY_WIDTH]
    u_ref[...] = y[:, HY_WIDTH:2 * HY_WIDTH] * y[:, 2 * HY_WIDTH:]


def _short_conv(proj, w, b, row0, B, L):
    nT = L // HY_TB
    blk0 = row0 // HY_TB
    sub = HY_TB // 8
    W3 = 3 * HY_WIDTH
    slab = SLAB_HY * BRANCH_W // W3
    cur = pl.BlockSpec((HY_TB, W3), lambda bb, t: (blk0 + bb * nT + t, slab))
    prv = pl.BlockSpec((8, W3), lambda bb, t: (jnp.maximum((blk0 + bb * nT + t) * sub - 1, 0), slab))
    nxt = pl.BlockSpec((8, W3), lambda bb, t: (jnp.minimum((blk0 + bb * nT + t + 1) * sub,
                                                           (blk0 + B * nT) * sub - 1), slab))
    const = lambda shape: pl.BlockSpec(shape, lambda bb, t: (0,) * len(shape))
    out = pl.BlockSpec((HY_TB, HY_WIDTH), lambda bb, t: (bb * nT + t, 0))
    return pl.pallas_call(
        functools.partial(_shortconv_kernel, nT=nT), grid=(B, nT),
        in_specs=[cur, prv, nxt, const((8, W3)), const((1, W3))], out_specs=[out, out],
        out_shape=[jax.ShapeDtypeStruct((B * L, HY_WIDTH), F32)] * 2,
        compiler_params=_cparams(("parallel", "parallel")), name="hyena_short_conv")(proj, proj, proj, w, b)


HYF_TB = 256


def _filter_kernel(band_ref, w1_ref, b1_ref, w2_ref, b2_ref, fr_ref, w3_ref, dl_ref, h_ref, l1_ref, *, L):
    i = pl.program_id(0)
    pos = (lax.broadcasted_iota(jnp.int32, (HYF_TB, LANES), 0) + i * HYF_TB).astype(F32)
    lane = lax.broadcasted_iota(jnp.int32, (HYF_TB, LANES), 1)
    t = pos / (L - 1)
    ang = (2.0 * math.pi * pos / L) * band_ref[...]
    z = jnp.where(lane == 0, t,
                  jnp.where(lane <= HY_BANDS, jnp.cos(ang),
                            jnp.where(lane <= 2 * HY_BANDS, -jnp.sin(ang), 0.0)))
    fr = fr_ref[...]
    h = jnp.sin(fr * (jnp.dot(z, w1_ref[...], precision=HIGHEST, preferred_element_type=F32) + b1_ref[...]))
    h = jnp.sin(fr * (jnp.dot(h, w2_ref[...], precision=HIGHEST, preferred_element_type=F32) + b2_ref[...]))
    h = jnp.dot(h, w3_ref[...], precision=HIGHEST, preferred_element_type=F32)
    tt = (lax.broadcasted_iota(jnp.int32, (HYF_TB, HY_WIDTH), 0) + i * HYF_TB).astype(F32) / (L - 1)
    window = jnp.exp(-tt * dl_ref[...])
    rowi = lax.broadcasted_iota(jnp.int32, (HYF_TB, HY_WIDTH), 0) + i * HYF_TB
    hf = h[:, :HY_WIDTH] * window
    hb = jnp.where(rowi == 0, 0.0, h[:, HY_WIDTH:] * window)
    h_ref[0] = hf
    h_ref[1] = hb

    @pl.when(i == 0)
    def _():
        l1_ref[...] = jnp.zeros_like(l1_ref)

    l1_ref[...] += jnp.sum(jnp.abs(hf) + jnp.abs(hb), axis=0, keepdims=True)


def _hyena_filter(L, w1, b1, w2, b2, freq, w3):
    band = np.zeros((1, LANES), np.float32)
    bands = np.linspace(1e-4, HY_BANDS - 1, HY_BANDS, dtype=np.float32)
    band[0, 1:1 + HY_BANDS] = bands
    band[0, 1 + HY_BANDS:1 + 2 * HY_BANDS] = bands
    max_decay = math.log(HY_DECAY_TARGET) / HY_SHORT_DECAY_PCT
    min_decay = math.log(HY_DECAY_TARGET) / HY_LONG_DECAY_PCT
    deltas = np.abs(np.linspace(min_decay, max_decay, HY_WIDTH, dtype=np.float32))[None, :]
    w1p = jnp.zeros((LANES, HY_FILTER_HIDDEN), F32).at[:HY_EMB].set(w1.astype(F32))
    const = lambda shape: pl.BlockSpec(shape, lambda i: (0,) * len(shape))
    H = HY_FILTER_HIDDEN
    return pl.pallas_call(
        functools.partial(_filter_kernel, L=L), grid=(L // HYF_TB,),
        in_specs=[const((1, LANES)), const((LANES, H)), const((1, H)), const((H, H)), const((1, H)),
                  const((1, H)), const((H, 2 * HY_WIDTH)), const((1, HY_WIDTH))],
        out_specs=[pl.BlockSpec((2, HYF_TB, HY_WIDTH), lambda i: (0, i, 0)), const((1, HY_WIDTH))],
        out_shape=[jax.ShapeDtypeStruct((2, L, HY_WIDTH), F32), jax.ShapeDtypeStruct((1, HY_WIDTH), F32)],
        compiler_params=_cparams(("arbitrary",)), name="hyena_filter")(
            jnp.asarray(band), w1p, b1.astype(F32)[None], w2.astype(F32), b2.astype(F32)[None],
            freq.astype(F32)[None], w3.astype(F32), jnp.asarray(deltas))


def _fft_factors(n):
    lg = int(round(math.log2(n)))
    n1 = 1 << (lg // 2)
    return n1, n // n1


def _dft(n):
    k = np.arange(n)
    a = -2.0 * np.pi * ((k[:, None] * k[None, :]) % n) / n
    return np.cos(a), np.sin(a)


FFT_G = 8


def _split3(f):
    f = jnp.asarray(np.asarray(f, np.float32))
    hi = f.astype(BF16)
    lo = (f - hi.astype(F32)).astype(BF16)
    return jnp.concatenate([hi, lo, hi], axis=1)


def _dot3(f3, x):
    hi = x.astype(BF16)
    lo = (x - hi.astype(F32)).astype(BF16)
    return jnp.dot(f3, jnp.concatenate([hi, hi, lo], axis=0), preferred_element_type=F32)


def _fft1_kernel(u_ref, f_ref, yr_ref, yi_ref, *, n1):
    P, _, G, C = u_ref.shape[1:]
    y = _dot3(f_ref[...], u_ref[0].reshape(P * (n1 // 2) * G, C))
    yr_ref[0] = y[:n1 * G].reshape(n1, G, C)
    yi_ref[0] = y[n1 * G:].reshape(n1, G, C)


def _fft_stage1(u, n1, n2, pair):
    B, L, C = u.shape
    P = 2 if pair else 1
    fr, fi = _dft(n1)
    G = FFT_G
    eye = np.eye(G)
    kr, ki = np.kron(fr[:, :n1 // 2], eye), np.kron(fi[:, :n1 // 2], eye)
    f3 = _split3(np.block([[kr, -ki], [ki, kr]]) if pair else np.concatenate([kr, ki], 0))
    blk = pl.BlockSpec((1, P, n1 // 2, G, C), lambda b, j: (b, 0, 0, j, 0))
    oblk = pl.BlockSpec((1, n1, G, C), lambda b, j: (b, 0, j, 0))
    shp = jax.ShapeDtypeStruct((B // P, n1, n2, C), F32)
    yr, yi = pl.pallas_call(
        functools.partial(_fft1_kernel, n1=n1), grid=(B // P, n2 // G),
        in_specs=[blk, pl.BlockSpec(f3.shape, lambda b, j: (0, 0))],
        out_specs=[oblk, oblk], out_shape=[shp, shp],
        compiler_params=_cparams(("parallel", "parallel")), name="fft_stage1")(
            u.reshape(B // P, P, n1 // 2, n2, C), f3)
    return yr.reshape(B // P, n1 * n2, C), yi.reshape(B // P, n1 * n2, C)


def _cmul(ar, ai, br, bi):
    return ar * br - ai * bi, ar * bi + ai * br


def _fft2_fwd(yr, yi, tr, ti, f2, n2, C):
    ar, ai = _cmul(yr, yi, tr, ti)
    p = _dot3(f2, jnp.concatenate([ar, ai], axis=1))
    return p[:n2, :C] - p[n2:, C:], p[:n2, C:] + p[n2:, :C]


FFT_ROWS = 512


def _fft2_filter_kernel(yr_ref, yi_ref, tr_ref, ti_ref, f_ref, l1_ref, kr_ref, ki_ref, *, n2, n):
    C = yr_ref.shape[2]
    f2 = f_ref[...]
    scale = 1.0 / (l1_ref[...] * n)
    for kb in range(FFT_ROWS // n2):
        rws = slice(kb * n2, (kb + 1) * n2)
        tr = jnp.tile(tr_ref[kb], (1, C // LANES))
        ti = jnp.tile(ti_ref[kb], (1, C // LANES))
        gr, gi = _fft2_fwd(yr_ref[0, rws, :], yi_ref[0, rws, :], tr, ti, f2, n2, C)
        hr, hi = _fft2_fwd(yr_ref[1, rws, :], yi_ref[1, rws, :], tr, ti, f2, n2, C)
        kr_ref[rws, :] = (gr + hr) * scale
        ki_ref[rws, :] = (gi - hi) * scale


def _fft2_conv_kernel(yr_ref, yi_ref, tr_ref, ti_ref, f_ref, kr_ref, ki_ref, zr_ref, zi_ref, *, n2):
    C = yr_ref.shape[2]
    f2 = f_ref[...]
    for kb in range(FFT_ROWS // n2):
        rws = slice(kb * n2, (kb + 1) * n2)
        tr = jnp.tile(tr_ref[kb], (1, C // LANES))
        ti = jnp.tile(ti_ref[kb], (1, C // LANES))
        xr, xi = _fft2_fwd(yr_ref[0, rws, :], yi_ref[0, rws, :], tr, ti, f2, n2, C)
        vr, vi = _cmul(xr, xi, kr_ref[rws, :], ki_ref[rws, :])
        q = _dot3(f2, jnp.concatenate([vr, vi], axis=1))
        wr = q[:n2, :C] + q[n2:, C:]
        wi = q[:n2, C:] - q[n2:, :C]
        zr, zi = _cmul(wr, wi, tr, -ti)
        zr_ref[0, rws, :] = zr
        zi_ref[0, rws, :] = zi


def _fft_tables(n1, n2):
    n = n1 * n2
    k1 = jnp.arange(n1, dtype=jnp.int32)[:, None]
    j2 = jnp.arange(n2, dtype=jnp.int32)[None, :]
    a = (-2.0 * math.pi / n) * (k1 * j2).astype(F32)
    tr = jnp.broadcast_to(jnp.cos(a)[:, :, None], (n1, n2, LANES))
    ti = jnp.broadcast_to(jnp.sin(a)[:, :, None], (n1, n2, LANES))
    fr, fi = _dft(n2)
    return tr, ti, _split3(np.concatenate([fr, fi], 0))


def _fft_stage2_filter(yr, yi, tables, l1, n1, n2):
    _, N, C = yr.shape
    tr, ti, f2 = tables
    kb = FFT_ROWS // n2
    blk = pl.BlockSpec((2, FFT_ROWS, C), lambda k: (0, k, 0))
    tblk = pl.BlockSpec((kb, n2, LANES), lambda k: (k, 0, 0))
    oblk = pl.BlockSpec((FFT_ROWS, C), lambda k: (k, 0))
    shp = jax.ShapeDtypeStruct((N, C), F32)
    return pl.pallas_call(
        functools.partial(_fft2_filter_kernel, n2=n2, n=N), grid=(N // FFT_ROWS,),
        in_specs=[blk, blk, tblk, tblk, pl.BlockSpec(f2.shape, lambda k: (0, 0)),
                  pl.BlockSpec((1, C), lambda k: (0, 0))],
        out_specs=[oblk, oblk], out_shape=[shp, shp],
        compiler_params=_cparams(("parallel",)), name="fft_stage2_filter")(yr, yi, tr, ti, f2, l1)


def _fft_stage2_conv(yr, yi, tables, spec_r, spec_i, n1, n2):
    B, N, C = yr.shape
    tr, ti, f2 = tables
    kb = FFT_ROWS // n2
    blk = pl.BlockSpec((1, FFT_ROWS, C), lambda b, k: (b, k, 0))
    tblk = pl.BlockSpec((kb, n2, LANES), lambda b, k: (k, 0, 0))
    sblk = pl.BlockSpec((FFT_ROWS, C), lambda b, k: (k, 0))
    shp = jax.ShapeDtypeStruct((B, N, C), F32)
    return pl.pallas_call(
        functools.partial(_fft2_conv_kernel, n2=n2), grid=(B, N // FFT_ROWS),
        in_specs=[blk, blk, tblk, tblk, pl.BlockSpec(f2.shape, lambda b, k: (0, 0)), sblk, sblk],
        out_specs=[blk, blk], out_shape=[shp, shp],
        compiler_params=_cparams(("parallel", "parallel")), name="fft_stage2_conv")(
            yr, yi, tr, ti, f2, spec_r, spec_i)


def _fft3_kernel(zr_ref, zi_ref, f_ref, x0_ref, u_ref, bias_ref, o_ref):
    n1, G, C = zr_ref.shape[1:]
    P = u_ref.shape[1]
    z = jnp.concatenate([zr_ref[0].reshape(n1 * G, C), zi_ref[0].reshape(n1 * G, C)], axis=0)
    conv = _dot3(f_ref[...], z).reshape(P, n1 // 2, G, C)
    o_ref[0] = x0_ref[0] * (conv + u_ref[0] * bias_ref[...])


def _fft_stage3(zr, zi, x0, u, bias, n1, n2, pair):
    B, L, C = u.shape
    P = 2 if pair else 1
    fr, fi = _dft(n1)
    G = FFT_G
    eye = np.eye(G)
    kr, ki = np.kron(fr[:n1 // 2], eye), np.kron(fi[:n1 // 2], eye)
    f3 = _split3(np.block([[kr, ki], [-ki, kr]]) if pair else np.concatenate([kr, ki], 1))
    zblk = pl.BlockSpec((1, n1, G, C), lambda b, j: (b, 0, j, 0))
    ublk = pl.BlockSpec((1, P, n1 // 2, G, C), lambda b, j: (b, 0, 0, j, 0))
    v5 = lambda a: a.reshape(B // P, P, n1 // 2, n2, C)
    out = pl.pallas_call(
        _fft3_kernel, grid=(B // P, n2 // G),
        in_specs=[zblk, zblk, pl.BlockSpec(f3.shape, lambda b, j: (0, 0)), ublk, ublk,
                  pl.BlockSpec((1, C), lambda b, j: (0, 0))],
        out_specs=ublk, out_shape=jax.ShapeDtypeStruct((B // P, P, n1 // 2, n2, C), F32),
        compiler_params=_cparams(("parallel", "parallel")), name="fft_stage3")(
            zr.reshape(B // P, n1, n2, C), zi.reshape(B // P, n1, n2, C), f3, v5(x0), v5(u), bias)
    return out.reshape(B * L, C)


def _hyena(proj, conv_w, conv_b, filt, hy_bias, row0, B, L):
    n1, n2 = _fft_factors(2 * L)
    pair = B % 2 == 0
    x0, u = _short_conv(proj, conv_w, conv_b, row0, B, L)
    x0 = x0.reshape(B, L, HY_WIDTH)
    u = u.reshape(B, L, HY_WIDTH)
    tables = _fft_tables(n1, n2)
    hfb, l1 = _hyena_filter(L, *filt)
    fr, fi = _fft_stage1(hfb, n1, n2, False)
    sr, si = _fft_stage2_filter(fr, fi, tables, l1, n1, n2)
    yr, yi = _fft_stage1(u, n1, n2, pair)
    zr, zi = _fft_stage2_conv(yr, yi, tables, sr, si, n1, n2)
    return _fft_stage3(zr, zi, x0, u, hy_bias, n1, n2, pair)


def _group_specs(parts, tm):
    starts, specs = [], []
    s = 0
    for a in parts:
        n = a.shape[0] // tm
        starts.append(s)
        specs.append(pl.BlockSpec((tm, a.shape[1]),
                                  (lambda s, n: lambda i, *_: (jnp.clip(i - s, 0, n - 1), 0))(s, n)))
        s += n
    return starts, specs


def _group_pick(refs, starts):
    i = pl.program_id(0)
    v = refs[0][...]
    for r, s in zip(refs[1:], starts[1:]):
        v = jnp.where(i >= s, r[...], v)
    return v


def _merge_kernel(*refs, starts):
    ng = len(starts)
    x_ref, g_ref, wg_ref = refs[:3]
    y_refs = [refs[3 + j * ng:3 + (j + 1) * ng] for j in range(N_BRANCHES)]
    (wa_ref, wb_ref, wc_ref, wo_ref, g2_ref, wr_ref, br_ref,
     o_ref, xn2_ref, route_ref) = refs[3 + N_BRANCHES * ng:]
    x = x_ref[...]
    xn = _rms(x, g_ref[...]).astype(BF16)
    merged = None
    for j, w_ref in enumerate((wa_ref, wb_ref, wc_ref)):
        gate = jnp.dot(xn, wg_ref[:, j * D_MODEL:(j + 1) * D_MODEL], preferred_element_type=F32)
        y = _group_pick(y_refs[j], starts)
        br = jnp.dot(y.astype(BF16), w_ref[...], preferred_element_type=F32)
        term = jax.nn.sigmoid(gate) * br
        merged = term if merged is None else merged + term
    x = x + jnp.dot(merged.astype(BF16), wo_ref[...], preferred_element_type=F32)
    o_ref[...] = x
    xn2 = _rms(x, g2_ref[...])
    xn2_ref[...] = xn2
    route_ref[...] = _route(xn2, wr_ref[...], br_ref[...])


def _merge(x, g, wg, ya, yb, yc, wa, wb, wc, wo, g2, w_router, b_router):
    T = x.shape[0]
    tm = min(512, T)
    xspec = pl.BlockSpec((tm, D_MODEL), lambda i: (i, 0))
    const = lambda shape: pl.BlockSpec(shape, lambda i: (0,) * len(shape))
    starts, yspecs = _group_specs(ya, tm)
    xshape = jax.ShapeDtypeStruct((T, D_MODEL), F32)
    return pl.pallas_call(
        functools.partial(_merge_kernel, starts=starts), grid=(T // tm,),
        in_specs=[xspec, const((1, D_MODEL)), const((D_MODEL, 3 * D_MODEL))] + yspecs * N_BRANCHES
        + [const((BRANCH_W, D_MODEL)), const((BRANCH_W, D_MODEL)), const((BRANCH_W, D_MODEL)),
           const((D_MODEL, D_MODEL)), const((1, D_MODEL)), const((D_MODEL, LANES)), const((1, LANES))],
        out_specs=[xspec, xspec, pl.BlockSpec((tm, LANES), lambda i: (i, 0))],
        out_shape=[xshape, xshape, jax.ShapeDtypeStruct((T, LANES), F32)],
        compiler_params=_cparams(("parallel",)), name="merge_out_projection")(
            x, g, wg, *ya, *yb, *yc, wa, wb, wc, wo, g2, w_router, b_router)


def _route(xn, w, b):
    lg = jnp.dot(xn, w, precision=HIGHEST, preferred_element_type=F32) + b
    lane = lax.broadcasted_iota(jnp.int32, lg.shape, 1).astype(F32)
    neg = -jnp.inf
    is_g = lane < N_GROUPS
    gl = jnp.where(is_g, lg, neg)
    gmax = jnp.max(gl, axis=1, keepdims=True)
    gsel = jnp.min(jnp.where(gl == gmax, lane, float(LANES)), axis=1, keepdims=True)
    gprob = 1.0 / jnp.sum(jnp.where(is_g, jnp.exp(lg - gmax), 0.0), axis=1, keepdims=True)
    lo = N_GROUPS + gsel * EXPERTS_PER_GROUP
    el = jnp.where((lane >= lo) & (lane < lo + EXPERTS_PER_GROUP), lg, neg)
    v1 = jnp.max(el, axis=1, keepdims=True)
    i1 = jnp.min(jnp.where(el == v1, lane, float(LANES)), axis=1, keepdims=True)
    el2 = jnp.where(lane == i1, neg, el)
    v2 = jnp.max(el2, axis=1, keepdims=True)
    i2 = jnp.min(jnp.where(el2 == v2, lane, float(LANES)), axis=1, keepdims=True)
    e = jnp.exp(v2 - v1)
    w1 = gprob / (1.0 + e)
    w2 = w1 * e
    return jnp.where(lane == 0, i1 - N_GROUPS,
                     jnp.where(lane == 1, i2 - N_GROUPS,
                               jnp.where(lane == 2, w1, jnp.where(lane == 3, w2, 0.0))))


MOE_ROWS = 512


def _moe_kernel(be_ref, nu_ref, idx_hbm, xn_hbm, wg0, wu0, wd0, wg1, wu1, wd1, out_hbm,
                idx, xbuf, obuf, isem, gsem, ssem):
    R = MOE_ROWS
    j = pl.program_id(0)
    nu = nu_ref[0]
    a = 2 * j
    b = a + 1

    def idx_copy(blk, slot):
        return pltpu.make_async_copy(idx_hbm.at[blk], idx.at[slot], isem.at[slot])

    def issue_gathers(slot):
        for r in range(R):
            pltpu.make_async_copy(xn_hbm.at[pl.ds(idx[slot, r], 1)], xbuf.at[slot, pl.ds(r, 1)],
                                  gsem.at[slot]).start()

    def wait_gathers(slot):
        pltpu.make_async_copy(xn_hbm.at[pl.ds(0, R)], xbuf.at[slot], gsem.at[slot]).wait()

    def issue_scatters(slot):
        for r in range(R):
            pltpu.make_async_copy(obuf.at[slot, pl.ds(r, 1)], out_hbm.at[pl.ds(idx[slot, R + r], 1)],
                                  ssem.at[slot]).start()

    def wait_scatters(slot):
        pltpu.make_async_copy(obuf.at[slot], out_hbm.at[pl.ds(0, R)], ssem.at[slot]).wait()

    def compute(slot, wg, wu, wd):
        xb = xbuf[slot].astype(BF16)
        h = jax.nn.silu(jnp.dot(xb, wg[0], preferred_element_type=F32)) \
            * jnp.dot(xb, wu[0], preferred_element_type=F32)
        obuf[slot] = jnp.dot(h.astype(BF16), wd[0], preferred_element_type=F32)

    @pl.when(j == 0)
    def _():
        obuf[0] = jnp.zeros((R, D_MODEL), F32)
        n_rows = out_hbm.shape[0]
        for k in (2, 1):
            z = pltpu.make_async_copy(obuf.at[0], out_hbm.at[pl.ds(n_rows - k * R, R)], ssem.at[0])
            z.start()
            z.wait()
        c = idx_copy(0, 0)
        c.start()
        c.wait()
        issue_gathers(0)

        @pl.when(1 < nu)
        def _():
            idx_copy(1, 1).start()

    @pl.when(a < nu)
    def _():
        wait_gathers(0)

        @pl.when(b < nu)
        def _():
            idx_copy(b, 1).wait()
            issue_gathers(1)

        @pl.when(j >= 1)
        def _():
            wait_scatters(0)

        compute(0, wg0, wu0, wd0)
        issue_scatters(0)

        @pl.when(a + 2 < nu)
        def _():
            idx_copy(a + 2, 0).start()

    @pl.when(b < nu)
    def _():
        wait_gathers(1)

        @pl.when(b + 1 < nu)
        def _():
            idx_copy(b + 1, 0).wait()
            issue_gathers(0)

        @pl.when(j >= 1)
        def _():
            wait_scatters(1)

        compute(1, wg1, wu1, wd1)
        issue_scatters(1)

        @pl.when(b + 2 < nu)
        def _():
            idx_copy(b + 2, 1).start()

    @pl.when((a < nu) & (a + 2 >= nu))
    def _():
        wait_scatters(0)

        @pl.when((b < nu) | (j >= 1))
        def _():
            wait_scatters(1)


def _moe_experts(xn, idx, block_e, n_used, wg, wu, wd):
    T = xn.shape[0]
    R = MOE_ROWS
    n_blocks = idx.shape[0]
    assert n_blocks % 2 == 0
    wspec = lambda o, shp: pl.BlockSpec(shp, lambda j, be, nu: (be[2 * j + o], 0, 0))
    w_in, w_out = (1, D_MODEL, EXPERT_HIDDEN), (1, EXPERT_HIDDEN, D_MODEL)
    grid_spec = pltpu.PrefetchScalarGridSpec(
        num_scalar_prefetch=2, grid=(n_blocks // 2,),
        in_specs=[pl.BlockSpec(memory_space=pl.ANY), pl.BlockSpec(memory_space=pl.ANY),
                  wspec(0, w_in), wspec(0, w_in), wspec(0, w_out),
                  wspec(1, w_in), wspec(1, w_in), wspec(1, w_out)],
        out_specs=pl.BlockSpec(memory_space=pl.ANY),
        scratch_shapes=[pltpu.SMEM((2, 2 * R), jnp.int32), pltpu.VMEM((2, R, D_MODEL), F32),
                        pltpu.VMEM((2, R, D_MODEL), F32), pltpu.SemaphoreType.DMA((2,)),
                        pltpu.SemaphoreType.DMA((2,)), pltpu.SemaphoreType.DMA((2,))])
    return pl.pallas_call(
        _moe_kernel, grid_spec=grid_spec, out_shape=jax.ShapeDtypeStruct((TOP_K * T + 2 * R, D_MODEL), F32),
        compiler_params=_cparams(("arbitrary",)), name="moe_experts")(
            block_e, n_used, idx, xn, wg, wu, wd, wg, wu, wd)


def _dispatch(route, T):
    R = MOE_ROWS
    M = T * TOP_K
    expert_id = route[:, :TOP_K].astype(jnp.int32).reshape(-1)
    order = jnp.argsort(expert_id).astype(jnp.int32)
    counts =jnp.bincount(expert_id, length=N_EXPERTS).astype(jnp.int32)
    starts = jnp.cumsum(counts) - counts
    padded = (counts + R - 1) // R * R
    p_ends = jnp.cumsum(padded)
    p_starts = p_ends - padded
    n_blocks = -(-M // R) + N_EXPERTS
    first_row = jnp.arange(n_blocks, dtype=jnp.int32) * R
    block_e = jnp.minimum(jnp.sum(p_ends[None, :] <= first_row[:, None], axis=1), N_EXPERTS - 1).astype(jnp.int32)
    row = jnp.arange(n_blocks * R, dtype=jnp.int32).reshape(n_blocks, R)
    k = row - p_starts[block_e][:, None]
    valid = (k < counts[block_e][:, None]) & (row < p_ends[N_EXPERTS - 1])
    a = order[jnp.clip(starts[block_e][:, None] + k, 0, M - 1)]
    token = a >> 1
    src = jnp.where(valid, token, 0)
    scrap = M + row % (2 * R)
    dst = jnp.where(valid, (a & 1) * T + token, scrap)
    n_used = (p_ends[N_EXPERTS - 1:] // R).astype(jnp.int32)
    return jnp.concatenate([src, dst], axis=1), block_e, n_used


def _final_kernel(x_ref, e0_ref, e1_ref, r_ref, g_ref, o_ref):
    r = r_ref[...]
    x = x_ref[...] + e0_ref[...] * r[:, 2:3] + e1_ref[...] * r[:, 3:4]
    o_ref[...] = _rms(x, g_ref[...])


def _final(x, e2, route, g, row0, rows):
    T = x.shape[0]
    tm = min(512, rows)
    nT = T // tm
    b0 = row0 // tm
    xspec = pl.BlockSpec((tm, D_MODEL), lambda i: (b0 + i, 0))
    return pl.pallas_call(
        _final_kernel, grid=(rows // tm,),
        in_specs=[xspec, xspec, pl.BlockSpec((tm, D_MODEL), lambda i: (nT + b0 + i, 0)),
                  pl.BlockSpec((tm, LANES), lambda i: (b0 + i, 0)), pl.BlockSpec((1, D_MODEL), lambda i: (0, 0))],
        out_specs=pl.BlockSpec((tm, D_MODEL), lambda i: (i, 0)),
        out_shape=jax.ShapeDtypeStruct((rows, D_MODEL), F32),
        compiler_params=_cparams(("parallel",)), name="combine_final_norm")(x, e2, e2, route, g)


def _pack_layer(l, p):
    w = p['w_in'][l]
    o_gate = 3 * HG_WIDTH + 2 * HG_WIDTH + 4 * ML_WIDTH
    o_hy = o_gate + 4 * ML_HEADS
    o_g = o_hy + 3 * HY_WIDTH
    gates = jnp.pad(w[:, o_gate:o_hy], ((0, 0), (0, PROJ_W - PROJ_MAIN - 4 * ML_HEADS)))
    w_proj = jnp.concatenate([w[:, :o_gate], w[:, o_hy:o_g], gates], axis=1).astype(BF16)
    w_gate = w[:, o_g:].astype(BF16)
    gate_bias = jnp.pad(jnp.concatenate([p['ml_i_bias'][l].reshape(-1), p['ml_f_bias'][l].reshape(-1)]),
                        (0, LANES - 4 * ML_HEADS)).astype(F32)[None]
    w_router = jnp.pad(jnp.concatenate([p['router_group_w'][l], p['router_expert_w'][l]], axis=1),
                       ((0, 0), (0, LANES - N_GROUPS - N_EXPERTS))).astype(F32)
    b_router = jnp.pad(jnp.concatenate([p['router_group_b'][l], p['router_expert_b'][l]]),
                       (0, LANES - N_GROUPS - N_EXPERTS)).astype(F32)[None]
    return dict(
        norm_mix_g=p['norm_mix_g'][l].astype(F32)[None], w_proj=w_proj, w_gate=w_gate,
        hg_norm_g=p['hg_norm_g'][l].astype(F32)[None], gate_bias=gate_bias,
        ml_norm_g=p['ml_norm_g'][l].astype(F32)[None],
        conv_w=jnp.pad(p['hy_conv_w'][l].astype(F32), ((0, 5), (0, 0))), conv_b=p['hy_conv_b'][l].astype(F32)[None],
        filt=(p['filt_w1'][l], p['filt_b1'][l], p['filt_w2'][l], p['filt_b2'][l], p['filt_freq'][l],
              p['filt_w3'][l]),
        hy_bias=p['hy_bias'][l].astype(F32)[None],
        wa=p['w_branch_a'][l].astype(BF16), wb=p['w_branch_b'][l].astype(BF16),
        wc=p['w_branch_c'][l].astype(BF16), wo=p['w_out'][l].astype(BF16),
        norm_ffn_g=p['norm_ffn_g'][l].astype(F32)[None], w_router=w_router, b_router=b_router,
        wg=p['exp_w_gate'][l].astype(BF16), wu=p['exp_w_up'][l].astype(BF16),
        wd=p['exp_w_down'][l].astype(BF16))


def _trunk(x, groups, p):
    T = x.shape[0]
    lbs = jnp.cumsum(jax.nn.softmax(p['hg_lb'].astype(F32), axis=0), axis=0)
    lbs = lbs - lbs[0:1]
    moe = None
    for l in range(DEPTH):
        lp = _pack_layer(l, p)
        lb = lbs[l][None]
        lbrow = jnp.concatenate([jnp.log(lb), jnp.log1p(-lb), 1.0 - lb, jnp.zeros((5, HG_WIDTH), F32)], axis=0)
        if moe is None:
            proj = _in_projection(x, lp['norm_mix_g'], lp['w_proj'])
        else:
            proj, x = _in_projection(x, lp['norm_mix_g'], lp['w_proj'], moe)
        ya, yb, yc = [], [], []
        for row0, B, L in groups:
            ya.append(_hgrn2(proj, lbrow, lp['hg_norm_g'], row0, B, L))
            yb.append(_mlstm(proj, lp['gate_bias'], lp['ml_norm_g'], row0, B, L))
            yc.append(_hyena(proj, lp['conv_w'], lp['conv_b'], lp['filt'], lp['hy_bias'], row0, B, L))
        x, xn, route = _merge(x, lp['norm_mix_g'], lp['w_gate'], ya, yb, yc, lp['wa'], lp['wb'], lp['wc'],
                              lp['wo'], lp['norm_ffn_g'], lp['w_router'], lp['b_router'])
        rows, block_e, n_used = _dispatch(route, T)
        e2 = _moe_experts(xn, rows, block_e, n_used, lp['wg'], lp['wu'], lp['wd'])
        moe = (e2, route)
    g = p['final_norm_g'].astype(F32)[None]
    return [_final(x, moe[0], moe[1], g, row0, B * L) for row0, B, L in groups]


def kernel(x_prompt, x_sample, norm_mix_g, w_in, hg_lb, hg_norm_g, ml_i_bias, ml_f_bias, ml_norm_g, hy_conv_w, hy_conv_b, filt_w1, filt_b1, filt_w2, filt_b2, filt_freq, filt_w3, hy_bias, w_branch_a, w_branch_b, w_branch_c, w_out, norm_ffn_g, router_group_w, router_group_b, router_expert_w, router_expert_b, exp_w_gate, exp_w_up, exp_w_down, final_norm_g):
    p = dict(norm_mix_g=norm_mix_g, w_in=w_in, hg_lb=hg_lb, hg_norm_g=hg_norm_g, ml_i_bias=ml_i_bias,
             ml_f_bias=ml_f_bias, ml_norm_g=ml_norm_g, hy_conv_w=hy_conv_w, hy_conv_b=hy_conv_b,
             filt_w1=filt_w1, filt_b1=filt_b1, filt_w2=filt_w2, filt_b2=filt_b2, filt_freq=filt_freq,
             filt_w3=filt_w3, hy_bias=hy_bias, w_branch_a=w_branch_a, w_branch_b=w_branch_b,
             w_branch_c=w_branch_c, w_out=w_out, norm_ffn_g=norm_ffn_g, router_group_w=router_group_w,
             router_group_b=router_group_b, router_expert_w=router_expert_w,
             router_expert_b=router_expert_b, exp_w_gate=exp_w_gate, exp_w_up=exp_w_up,
             exp_w_down=exp_w_down, final_norm_g=final_norm_g)
    Bp, Lp, _ = x_prompt.shape
    Bs, Ls, _ = x_sample.shape
    Tp, Ts = Bp * Lp, Bs * Ls
    x = jnp.concatenate([x_prompt.reshape(Tp, D_MODEL), x_sample.reshape(Ts, D_MODEL)], axis=0).astype(F32)
    yp, ys = _trunk(x, ((0, Bp, Lp), (Tp, Bs, Ls)), p)
    return (yp.reshape(Bp, Lp, D_MODEL), ys.reshape(Bs, Ls, D_MODEL))
```

```python
import functools
import math

import numpy as np
import jax
import jax.numpy as jnp
from jax import lax
from jax.experimental import pallas as pl
from jax.experimental.pallas import tpu as pltpu

F32 = jnp.float32
BF16 = jnp.bfloat16
HIGHEST = lax.Precision.HIGHEST

D_MODEL = 1024
EPS = 1e-6
DEPTH = 2
HG_HEADS, HG_KDIM, HG_WIDTH, HG_CHUNK = 8, 64, 512, 32
ML_HEADS, ML_HDIM, ML_WIDTH, ML_CHUNK = 4, 128, 512, 128
HY_WIDTH, HY_BANDS, HY_FILTER_HIDDEN = 512, 16, 64
HY_EMB = 1 + 2 * HY_BANDS
HY_SHORT_DECAY_PCT, HY_LONG_DECAY_PCT, HY_DECAY_TARGET = 0.3, 1.5, 1e-2
N_GROUPS, EXPERTS_PER_GROUP, TOP_K = 4, 8, 2
N_EXPERTS = N_GROUPS * EXPERTS_PER_GROUP
N_BRANCHES = 3
EXPERT_HIDDEN = D_MODEL // 2
MOE_BLOCK = 128

LANES = 128
BRANCH_W = 512
SLAB_HQ, SLAB_HFF, SLAB_HFB, SLAB_HI, SLAB_HG = 0, 1, 2, 3, 4
SLAB_MQ, SLAB_MK, SLAB_MV, SLAB_MO = 5, 6, 7, 8
SLAB_HY = 9
PROJ_MAIN = 12 * BRANCH_W
PROJ_W = PROJ_MAIN + 2 * LANES
PROJ_TN = 1280
GATE_BLOCK = PROJ_MAIN // LANES
VMEM_LIMIT = 48 * 1024 * 1024
VMEM_LIMIT_WIDE = 56 * 1024 * 1024


def _cparams(sem, vmem=VMEM_LIMIT):
    return pltpu.CompilerParams(dimension_semantics=sem, vmem_limit_bytes=vmem)


def _rms(x, g):
    return x * lax.rsqrt(jnp.mean(x * x, axis=-1, keepdims=True) + EPS) * g


def _log_sigmoid(z):
    return jnp.minimum(z, 0.0) - jnp.log1p(jnp.exp(-jnp.abs(z)))


def _bf16_terms(x):
    t1 = x.astype(BF16)
    r = x - t1.astype(F32)
    t2 = r.astype(BF16)
    return t1, t2, (r - t2.astype(F32)).astype(BF16)


def _inproj_kernel(x_ref, g_ref, w_ref, o_ref, xn_ref):
    @pl.when(pl.program_id(1) == 0)
    def _():
        xn_ref[...] = _rms(x_ref[...], g_ref[...]).astype(BF16)

    o_ref[...] = jnp.dot(xn_ref[...], w_ref[...], preferred_element_type=F32)


def _inproj_combine_kernel(x_ref, e0_ref, e1_ref, r_ref, g_ref, w_ref, o_ref, xo_ref, xn_ref):
    @pl.when(pl.program_id(1) == 0)
    def _():
        r = r_ref[...]
        x = x_ref[...] + e0_ref[...] * r[:, 2:3] + e1_ref[...] * r[:, 3:4]
        xo_ref[...] = x
        xn_ref[...] = _rms(x, g_ref[...]).astype(BF16)

    o_ref[...] = jnp.dot(xn_ref[...], w_ref[...], preferred_element_type=F32)


def _in_projection(x, g, w, moe=None):
    T = x.shape[0]
    tm = min(2048 if moe is None else 1024, T)
    tn = PROJ_TN
    grid = (T // tm, PROJ_W // tn)
    xspec = pl.BlockSpec((tm, D_MODEL), lambda i, j: (i, 0))
    gspec = pl.BlockSpec((1, D_MODEL), lambda i, j: (0, 0))
    wspec = pl.BlockSpec((D_MODEL, tn), lambda i, j: (0, j))
    ospec = pl.BlockSpec((tm, tn), lambda i, j: (i, j))
    scratch = [pltpu.VMEM((tm, D_MODEL), BF16)]
    cp = _cparams(("parallel", "arbitrary"), VMEM_LIMIT_WIDE)
    if moe is None:
        return pl.pallas_call(
            _inproj_kernel, grid=grid, in_specs=[xspec, gspec, wspec], out_specs=ospec,
            out_shape=jax.ShapeDtypeStruct((T, PROJ_W), F32), scratch_shapes=scratch,
            compiler_params=cp, name="in_projection")(x, g, w)
    e2, route = moe
    nT = T // tm
    e0spec = pl.BlockSpec((tm, D_MODEL), lambda i, j: (i, 0))
    e1spec = pl.BlockSpec((tm, D_MODEL), lambda i, j: (nT + i, 0))
    rspec = pl.BlockSpec((tm, LANES), lambda i, j: (i, 0))
    proj, xo = pl.pallas_call(
        _inproj_combine_kernel, grid=grid, in_specs=[xspec, e0spec, e1spec, rspec, gspec, wspec],
        out_specs=[ospec, xspec],
        out_shape=[jax.ShapeDtypeStruct((T, PROJ_W), F32), jax.ShapeDtypeStruct((T, D_MODEL), F32)],
        scratch_shapes=scratch, compiler_params=cp, name="in_projection_combine")(x, e2, e2, route, g, w)
    return proj, xo


HG_TB = 256
HG_PAIRS = HG_WIDTH // LANES


HG_UNROLL = 8
HG_SUB = 8
HG_NSUB = HG_CHUNK // HG_SUB
HG_XROWS = (HG_NSUB - 1) * HG_HEADS * HG_SUB
HG_XCOLS = HG_SUB * HG_NSUB * (HG_NSUB - 1) // 2


def _hgrn2_tiles(reverse):
    if reverse:
        return [(i, (i + 1) * HG_SUB, HG_CHUNK - (i + 1) * HG_SUB) for i in range(HG_NSUB - 1)]
    return [(i, 0, i * HG_SUB) for i in range(1, HG_NSUB)]


def _hgrn2_kernel(*refs, reverse, finalize):
    if finalize:
        (q_ref, z_ref, v_ref, lb_ref, tri_ref, bd_ref, xm_ref, of_ref, gate_ref, ng_ref,
         o_ref, st_ref) = refs
    else:
        (q_ref, z_ref, v_ref, lb_ref, tri_ref, bd_ref, xm_ref, o_ref, st_ref) = refs
    C, c, nb, W = HG_CHUNK, HG_SUB, HG_NSUB, HG_WIDTH
    n_chunks = HG_TB // C

    @pl.when(pl.program_id(1) == 0)
    def _():
        st_ref[...] = jnp.zeros_like(st_ref)

    log_lb = lb_ref[0:1, :]
    log_1mlb = lb_ref[1:2, :]
    one_m_lb = lb_ref[2:3, :]
    tri3 = tri_ref[...]
    bd = bd_ref[...]
    bd_bf = bd.astype(BF16)
    xmask = xm_ref[...]
    sub_row = lax.broadcasted_iota(jnp.int32, (nb, c, W), 1)
    lane_head = lax.broadcasted_iota(jnp.int32, (c, W), 1) // HG_KDIM
    tiles = _hgrn2_tiles(reverse)

    def chunk(i):
        ci = (n_chunks - 1 - i) if reverse else i
        off = pl.multiple_of(ci * C, C)
        q = q_ref[pl.ds(off, C), :]
        z = z_ref[pl.ds(off, C), :]
        v = v_ref[pl.ds(off, C), :]
        ls = jnp.minimum(z, 0.0) - jnp.log(1.0 + jnp.exp(-jnp.abs(z)))
        hi = log_1mlb + ls
        mx = jnp.maximum(log_lb, hi)
        lf = mx + jnp.log(1.0 + jnp.exp(-jnp.abs(log_lb - hi)))
        kk = one_m_lb * jnp.exp(ls - z)
        b = jnp.dot(tri3, jnp.concatenate(_bf16_terms(lf), axis=0), preferred_element_type=F32)
        btot = b[0:1, :] if reverse else b[C - 1:C, :]
        qb = (q * jnp.exp(b)).astype(BF16)
        kb = (kk * jnp.exp(btot - b)).astype(BF16)
        dec = jnp.exp(btot)
        vb = v.astype(BF16)

        q3, k3, v3 = (a.reshape(nb, c, W) for a in (q, kk, v))
        one = 1 if not reverse else c - 1
        f3 = (1.0 - kk).reshape(nb, c, W)
        kd = k3
        a_rows = [(q * kk).astype(BF16)]
        for d in range(1, c):
            ok = (sub_row < c - d) if reverse else (sub_row >= d)
            kd = pltpu.roll(kd, one, axis=1) * f3
            a_rows.append(jnp.where(ok, q3 * kd, 0.0).reshape(C, W).astype(BF16))
        a_all = jnp.concatenate(a_rows, axis=0)
        sums = jnp.concatenate(
            [jnp.dot(a_all[:, p * LANES:(p + 1) * LANES], bd_bf, preferred_element_type=F32)
             for p in range(HG_PAIRS)], axis=1)
        acc3 = sums[0:C].reshape(nb, c, W) * v3
        v_sh = v3
        for d in range(1, c):
            v_sh = pltpu.roll(v_sh, one, axis=1)
            acc3 = acc3 + sums[d * C:(d + 1) * C].reshape(nb, c, W) * v_sh
        acc = acc3.reshape(C, W)

        qx, kx, vx = [], [], []
        for (ti, s0, sn) in tiles:
            edge = s0 if reverse else s0 + sn - 1
            r = b[edge:edge + 1, :]
            rows = slice(ti * c, (ti + 1) * c)
            qh = q[rows] * jnp.exp(b[rows] - r)
            qx += [jnp.where(lane_head == h, qh, 0.0) for h in range(HG_HEADS)]
            kx.append(kk[s0:s0 + sn] * jnp.exp(r - b[s0:s0 + sn]))
            vx.append(v[s0:s0 + sn])
        qx = jnp.concatenate(qx, axis=0).astype(BF16)
        kx = jnp.concatenate(kx, axis=0).astype(BF16)
        vx = jnp.concatenate(vx, axis=0).astype(BF16)
        sc = lax.dot_general(qx, kx, (((1,), (1,)), ((), ())), preferred_element_type=F32) * xmask
        px = jnp.dot(sc.astype(BF16), vx, preferred_element_type=F32)
        offd = {}
        for n, (ti, s0, sn) in enumerate(tiles):
            base = n * HG_HEADS * c
            t_acc = jnp.where(lane_head == 0, px[base:base + c], 0.0)
            for h in range(1, HG_HEADS):
                t_acc = t_acc + jnp.where(lane_head == h, px[base + h * c:base + (h + 1) * c], 0.0)
            offd[ti] = t_acc
        acc = acc + jnp.concatenate([offd.get(ti, jnp.zeros((c, W), F32)) for ti in range(nb)], axis=0)

        parts = []
        for p in range(HG_PAIRS):
            sl = slice(p * LANES, (p + 1) * LANES)
            s_t = st_ref[p]
            parts.append(lax.dot_general(qb[:, sl], s_t.astype(BF16), (((1,), (1,)), ((), ())),
                                         preferred_element_type=F32))
            upd = lax.dot_general(vb[:, sl], kb[:, sl], (((0,), (0,)), ((), ())),
                                  preferred_element_type=F32)
            st_ref[p] = s_t * dec[:, sl] + upd * bd
        acc = acc + jnp.concatenate(parts, axis=1)

        if finalize:
            o = acc + of_ref[pl.ds(off, C), :]
            o2 = (o * o).astype(BF16)
            ms = jnp.concatenate(
                [jnp.dot(o2[:, p * LANES:(p + 1) * LANES], bd_bf, preferred_element_type=F32)
                 for p in range(HG_PAIRS)], axis=1) * (1.0 / HG_KDIM)
            g = gate_ref[pl.ds(off, C), :]
            y = o * lax.rsqrt(ms + EPS) * ng_ref[...] * (g * jax.nn.sigmoid(g))
            o_ref[pl.ds(off, C), :] = y.astype(o_ref.dtype)
        else:
            o_ref[pl.ds(off, C), :] = acc

    def chunk_group(j, carry):
        for u in range(HG_UNROLL):
            chunk(HG_UNROLL * j + u)
        return carry

    lax.fori_loop(0, n_chunks // HG_UNROLL, chunk_group, 0)


def _hgrn2(proj, lbrow, norm_g, row0, B, L):
    C = HG_CHUNK
    nT = L // HG_TB
    blk0 = row0 // HG_TB
    tri_f = jnp.asarray(np.tile(np.tril(np.ones((C, C), np.float32)), (1, 3))).astype(BF16)
    tri_b = jnp.asarray(np.tile(np.triu(np.ones((C, C), np.float32)), (1, 3))).astype(BF16)
    head = np.arange(LANES) // HG_KDIM
    bd = jnp.asarray((head[:, None] == head[None, :]).astype(np.float32))

    def tile_mask(reverse):
        m = np.zeros((HG_XROWS, HG_XCOLS), np.float32)
        col = 0
        for n, (_, _, sn) in enumerate(_hgrn2_tiles(reverse)):
            m[n * HG_HEADS * HG_SUB:(n + 1) * HG_HEADS * HG_SUB, col:col + sn] = 1.0
            col += sn
        return jnp.asarray(m)

    def in_spec(slab, reverse):
        if reverse:
            return pl.BlockSpec((HG_TB, BRANCH_W), lambda b, t: (blk0 + b * nT + nT - 1 - t, slab))
        return pl.BlockSpec((HG_TB, BRANCH_W), lambda b, t: (blk0 + b * nT + t, slab))

    def out_spec(reverse):
        if reverse:
            return pl.BlockSpec((HG_TB, BRANCH_W), lambda b, t: (b * nT + nT - 1 - t, 0))
        return pl.BlockSpec((HG_TB, BRANCH_W), lambda b, t: (b * nT + t, 0))

    const = lambda shape: pl.BlockSpec(shape, lambda b, t: (0,) * len(shape))
    scratch = [pltpu.VMEM((HG_PAIRS, LANES, LANES), F32)]
    consts = [const((8, BRANCH_W)), const((C, 3 * C)), const((LANES, LANES)), const((HG_XROWS, HG_XCOLS))]
    cp = _cparams(("parallel", "arbitrary"))
    o_f = pl.pallas_call(
        functools.partial(_hgrn2_kernel, reverse=False, finalize=False), grid=(B, nT),
        in_specs=[in_spec(SLAB_HQ, False), in_spec(SLAB_HFF, False), in_spec(SLAB_HI, False)] + consts,
        out_specs=out_spec(False), out_shape=jax.ShapeDtypeStruct((B * L, BRANCH_W), F32),
        scratch_shapes=scratch, compiler_params=cp, name="hgrn2_fwd")(
            proj, proj, proj, lbrow, tri_f, bd, tile_mask(False))
    return pl.pallas_call(
        functools.partial(_hgrn2_kernel, reverse=True, finalize=True), grid=(B, nT),
        in_specs=[in_spec(SLAB_HQ, True), in_spec(SLAB_HFB, True), in_spec(SLAB_HI, True)] + consts
        + [out_spec(True), in_spec(SLAB_HG, True), const((1, BRANCH_W))],
        out_specs=out_spec(True), out_shape=jax.ShapeDtypeStruct((B * L, BRANCH_W), BF16),
        scratch_shapes=scratch, compiler_params=cp, name="hgrn2_bwd")(
            proj, proj, proj, lbrow, tri_b, bd, tile_mask(True), o_f, proj, norm_g)


def _mlstm_kernel(*refs, reverse, finalize):
    if finalize:
        (q_ref, k_ref, v_ref, gt_ref, gb_ref, tri_ref, hf_ref, og_ref, ng_ref,
         o_ref, c_ref, n_ref, m_ref) = refs
    else:
        (q_ref, k_ref, v_ref, gt_ref, gb_ref, tri_ref, o_ref, c_ref, n_ref, m_ref) = refs
    C = ML_CHUNK

    @pl.when(pl.program_id(1) == 0)
    def _():
        c_ref[...] = jnp.zeros_like(c_ref)
        n_ref[...] = jnp.zeros_like(n_ref)
        m_ref[...] = jnp.zeros_like(m_ref)

    tri = tri_ref[...]
    lane = lax.broadcasted_iota(jnp.int32, (C, LANES), 1)
    ti = lax.broadcasted_iota(jnp.int32, (C, C), 0)
    si = lax.broadcasted_iota(jnp.int32, (C, C), 1)
    causal = (si >= ti) if reverse else (si <= ti)
    dsel = ML_HEADS if reverse else 0
    edge = 0 if reverse else C - 1
    scale = 1.0 / math.sqrt(ML_HDIM)
    order = range(ML_TB // C - 1, -1, -1) if reverse else range(ML_TB // C)
    for ci in order:
        _mlstm_chunk(refs, slice(ci * C, (ci + 1) * C), tri, lane, causal, dsel, edge, scale, finalize)


def _mlstm_chunk(refs, rws, tri, lane, causal, dsel, edge, scale, finalize):
    if finalize:
        (q_ref, k_ref, v_ref, gt_ref, gb_ref, tri_ref, hf_ref, og_ref, ng_ref,
         o_ref, c_ref, n_ref, m_ref) = refs
    else:
        (q_ref, k_ref, v_ref, gt_ref, gb_ref, tri_ref, o_ref, c_ref, n_ref, m_ref) = refs
    gates = gt_ref[rws, :] + gb_ref[...]
    g_col = jnp.where(lane >= 2 * ML_HEADS, _log_sigmoid(gates), gates)
    g_row = g_col.T
    b_col = jnp.dot(tri, g_col, precision=HIGHEST, preferred_element_type=F32)
    b_row = lax.dot_general(g_row, tri, (((1,), (1,)), ((), ())), precision=HIGHEST,
                            preferred_element_type=F32)

    outs = []
    for h in range(ML_HEADS):
        sl = slice(h * ML_HDIM, (h + 1) * ML_HDIM)
        li, lf = dsel + h, 2 * ML_HEADS + dsel + h
        q = q_ref[rws, sl]
        kc = k_ref[rws, sl] * scale
        v = v_ref[rws, sl]
        qb, kb, vb = q.astype(BF16), kc.astype(BF16), v.astype(BF16)
        bc = b_col[:, lf:lf + 1]
        br = b_row[lf:lf + 1, :]
        ic = g_col[:, li:li + 1]
        ir = g_row[li:li + 1, :]
        btot = b_col[edge:edge + 1, lf:lf + 1]
        m_prev = m_ref[0:1, h:h + 1]
        n_prev = n_ref[h:h + 1, :]
        c_prev = c_ref[h]

        log_d = jnp.where(causal, bc - br + ir, -jnp.inf)
        log_inter = bc + m_prev
        m_t = jnp.maximum(log_inter, jnp.max(log_d, axis=1, keepdims=True))
        a_inter = jnp.exp(log_inter - m_t)
        s = lax.dot_general(qb, kb, (((1,), (1,)), ((), ())), preferred_element_type=F32)
        s = s * jnp.exp(log_d - m_t)
        num = a_inter * jnp.dot(qb, c_prev.astype(BF16), preferred_element_type=F32) \
            + jnp.dot(s.astype(BF16), vb, preferred_element_type=F32)
        den = a_inter * jnp.sum(q * n_prev, axis=1, keepdims=True) + jnp.sum(s, axis=1, keepdims=True)
        outs.append(num / jnp.maximum(jnp.abs(den), jnp.exp(-m_t)))

        log_w = btot - bc + ic
        m_new = jnp.maximum(btot + m_prev, jnp.max(log_w, axis=0, keepdims=True))
        a_state = jnp.exp(btot + m_prev - m_new)
        kw = kc * jnp.exp(log_w - m_new)
        c_ref[h] = a_state * c_prev + lax.dot_general(kw.astype(BF16), vb, (((0,), (0,)), ((), ())),
                                                      preferred_element_type=F32)
        n_ref[h:h + 1, :] = a_state * n_prev + jnp.sum(kw, axis=0, keepdims=True)
        m_ref[0:1, h:h + 1] = m_new

    hcur = jnp.concatenate(outs, axis=1)
    if finalize:
        hsum = hcur + hf_ref[rws, :]
        ys = []
        for h in range(ML_HEADS):
            sl = slice(h * ML_HDIM, (h + 1) * ML_HDIM)
            hh = hsum[:, sl]
            ys.append(hh * lax.rsqrt(jnp.mean(hh * hh, axis=1, keepdims=True) + EPS))
        y = jnp.concatenate(ys, axis=1) * ng_ref[...] * jax.nn.sigmoid(og_ref[rws, :])
        o_ref[rws, :] = y.astype(o_ref.dtype)
    else:
        o_ref[rws, :] = hcur


ML_TB = 2 * ML_CHUNK


def _mlstm(proj, gate_bias, norm_g, row0, B, L):
    C = ML_TB
    nT = L // C
    blk0 = row0 // C
    tri_f = jnp.asarray(np.tril(np.ones((ML_CHUNK, ML_CHUNK), np.float32)))
    tri_b = jnp.asarray(np.triu(np.ones((ML_CHUNK, ML_CHUNK), np.float32)))

    def rows(reverse):
        if reverse:
            return lambda b, t: blk0 + b * nT + nT - 1 - t
        return lambda b, t: blk0 + b * nT + t

    def in_spec(slab, reverse):
        r = rows(reverse)
        return pl.BlockSpec((C, BRANCH_W), lambda b, t: (r(b, t), slab))

    def gate_spec(reverse):
        r = rows(reverse)
        return pl.BlockSpec((C, LANES), lambda b, t: (r(b, t), GATE_BLOCK))

    def out_spec(reverse):
        if reverse:
            return pl.BlockSpec((C, BRANCH_W), lambda b, t: (b * nT + nT - 1 - t, 0))
        return pl.BlockSpec((C, BRANCH_W), lambda b, t: (b * nT + t, 0))

    const = lambda shape: pl.BlockSpec(shape, lambda b, t: (0,) * len(shape))
    scratch = [pltpu.VMEM((ML_HEADS, ML_HDIM, ML_HDIM), F32), pltpu.VMEM((8, ML_HDIM), F32),
               pltpu.VMEM((8, LANES), F32)]
    cp = _cparams(("parallel", "arbitrary"))
    h_f = pl.pallas_call(
        functools.partial(_mlstm_kernel, reverse=False, finalize=False), grid=(B, nT),
        in_specs=[in_spec(SLAB_MQ, False), in_spec(SLAB_MK, False), in_spec(SLAB_MV, False),
                  gate_spec(False), const((1, LANES)), const((ML_CHUNK, ML_CHUNK))],
        out_specs=out_spec(False), out_shape=jax.ShapeDtypeStruct((B * L, BRANCH_W), F32),
        scratch_shapes=scratch, compiler_params=cp, name="mlstm_fwd")(
            proj, proj, proj, proj, gate_bias, tri_f)
    return pl.pallas_call(
        functools.partial(_mlstm_kernel, reverse=True, finalize=True), grid=(B, nT),
        in_specs=[in_spec(SLAB_MQ, True), in_spec(SLAB_MK, True), in_spec(SLAB_MV, True),
                  gate_spec(True), const((1, LANES)), const((ML_CHUNK, ML_CHUNK)),
                  out_spec(True), in_spec(SLAB_MO, True), const((1, BRANCH_W))],
        out_specs=out_spec(True), out_shape=jax.ShapeDtypeStruct((B * L, BRANCH_W), BF16),
        scratch_shapes=scratch, compiler_params=cp, name="mlstm_bwd")(
            proj, proj, proj, proj, gate_bias, tri_b, h_f, proj, norm_g)


HY_TB = 512


def _shortconv_kernel(c_ref, p_ref, n_ref, w_ref, b_ref, x0_ref, u_ref, *, nT):
    t = pl.program_id(1)
    cur = c_ref[...]
    prev_row = jnp.where(t > 0, p_ref[7:8, :], 0.0)
    next_row = jnp.where(t < nT - 1, n_ref[0:1, :], 0.0)
    tb = cur.shape[0]
    row = lax.broadcasted_iota(jnp.int32, cur.shape, 0)
    up = jnp.where(row == 0, prev_row, pltpu.roll(cur, 1, axis=0))
    dn = jnp.where(row == tb - 1, next_row, pltpu.roll(cur, tb - 1, axis=0))
    y = up * w_ref[0:1, :] + cur * w_ref[1:2, :] + dn * w_ref[2:3, :] + b_ref[...]
    x0_ref[...] = y[:, :HY_WIDTH]
    u_ref[...] = y[:, HY_WIDTH:2 * HY_WIDTH] * y[:, 2 * HY_WIDTH:]


def _short_conv(proj, w, b, row0, B, L):
    tb = min(HY_TB, L)
    nT = L // tb
    blk0 = row0 // tb
    sub = tb // 8
    W3 = 3 * HY_WIDTH
    slab = SLAB_HY * BRANCH_W // W3
    cur = pl.BlockSpec((tb, W3), lambda bb, t: (blk0 + bb * nT + t, slab))
    prv = pl.BlockSpec((8, W3), lambda bb, t: (jnp.maximum((blk0 + bb * nT + t) * sub - 1, 0), slab))
    nxt = pl.BlockSpec((8, W3), lambda bb, t: (jnp.minimum((blk0 + bb * nT + t + 1) * sub,
                                                           (blk0 + B * nT) * sub - 1), slab))
    const = lambda shape: pl.BlockSpec(shape, lambda bb, t: (0,) * len(shape))
    out = pl.BlockSpec((tb, HY_WIDTH), lambda bb, t: (bb * nT + t, 0))
    return pl.pallas_call(
        functools.partial(_shortconv_kernel, nT=nT), grid=(B, nT),
        in_specs=[cur, prv, nxt, const((8, W3)), const((1, W3))], out_specs=[out, out],
        out_shape=[jax.ShapeDtypeStruct((B * L, HY_WIDTH), F32)] * 2,
        compiler_params=_cparams(("parallel", "parallel")), name="hyena_short_conv")(proj, proj, proj, w, b)


HYF_TB = 256


def _filter_kernel(band_ref, w1_ref, b1_ref, w2_ref, b2_ref, fr_ref, w3_ref, dl_ref, h_ref, l1_ref, *, L):
    i = pl.program_id(0)
    pos = (lax.broadcasted_iota(jnp.int32, (HYF_TB, LANES), 0) + i * HYF_TB).astype(F32)
    lane = lax.broadcasted_iota(jnp.int32, (HYF_TB, LANES), 1)
    t = pos / (L - 1)
    ang = (2.0 * math.pi * pos / L) * band_ref[...]
    z = jnp.where(lane == 0, t,
                  jnp.where(lane <= HY_BANDS, jnp.cos(ang),
                            jnp.where(lane <= 2 * HY_BANDS, -jnp.sin(ang), 0.0)))
    fr = fr_ref[...]
    h = jnp.sin(fr * (jnp.dot(z, w1_ref[...], precision=HIGHEST, preferred_element_type=F32) + b1_ref[...]))
    h = jnp.sin(fr * (jnp.dot(h, w2_ref[...], precision=HIGHEST, preferred_element_type=F32) + b2_ref[...]))
    h = jnp.dot(h, w3_ref[...], precision=HIGHEST, preferred_element_type=F32)
    tt = (lax.broadcasted_iota(jnp.int32, (HYF_TB, HY_WIDTH), 0) + i * HYF_TB).astype(F32) / (L - 1)
    window = jnp.exp(-tt * dl_ref[...])
    rowi = lax.broadcasted_iota(jnp.int32, (HYF_TB, HY_WIDTH), 0) + i * HYF_TB
    hf = h[:, :HY_WIDTH] * window
    hb = jnp.where(rowi == 0, 0.0, h[:, HY_WIDTH:] * window)
    h_ref[0] = hf
    h_ref[1] = hb

    @pl.when(i == 0)
    def _():
        l1_ref[...] = jnp.zeros_like(l1_ref)

    l1_ref[...] += jnp.sum(jnp.abs(hf) + jnp.abs(hb), axis=0, keepdims=True)


def _hyena_filter(L, w1, b1, w2, b2, freq, w3):
    band = np.zeros((1, LANES), np.float32)
    bands = np.linspace(1e-4, HY_BANDS - 1, HY_BANDS, dtype=np.float32)
    band[0, 1:1 + HY_BANDS] = bands
    band[0, 1 + HY_BANDS:1 + 2 * HY_BANDS] = bands
    max_decay = math.log(HY_DECAY_TARGET) / HY_SHORT_DECAY_PCT
    min_decay = math.log(HY_DECAY_TARGET) / HY_LONG_DECAY_PCT
    deltas = np.abs(np.linspace(min_decay, max_decay, HY_WIDTH, dtype=np.float32))[None, :]
    w1p = jnp.zeros((LANES, HY_FILTER_HIDDEN), F32).at[:HY_EMB].set(w1.astype(F32))
    const = lambda shape: pl.BlockSpec(shape, lambda i: (0,) * len(shape))
    H = HY_FILTER_HIDDEN
    return pl.pallas_call(
        functools.partial(_filter_kernel, L=L), grid=(L // HYF_TB,),
        in_specs=[const((1, LANES)), const((LANES, H)), const((1, H)), const((H, H)), const((1, H)),
                  const((1, H)), const((H, 2 * HY_WIDTH)), const((1, HY_WIDTH))],
        out_specs=[pl.BlockSpec((2, HYF_TB, HY_WIDTH), lambda i: (0, i, 0)), const((1, HY_WIDTH))],
        out_shape=[jax.ShapeDtypeStruct((2, L, HY_WIDTH), F32), jax.ShapeDtypeStruct((1, HY_WIDTH), F32)],
        compiler_params=_cparams(("arbitrary",)), name="hyena_filter")(
            jnp.asarray(band), w1p, b1.astype(F32)[None], w2.astype(F32), b2.astype(F32)[None],
            freq.astype(F32)[None], w3.astype(F32), jnp.asarray(deltas))


def _fft_factors(n):
    lg = int(round(math.log2(n)))
    n1 = 1 << (lg // 2)
    return n1, n // n1


def _dft(n):
    k = np.arange(n)
    a = -2.0 * np.pi * ((k[:, None] * k[None, :]) % n) / n
    return np.cos(a), np.sin(a)


FFT_G = 8


def _split3(f):
    f = jnp.asarray(np.asarray(f, np.float32))
    hi = f.astype(BF16)
    lo = (f - hi.astype(F32)).astype(BF16)
    return jnp.concatenate([hi, lo, hi], axis=1)


def _dot3(f3, x):
    hi = x.astype(BF16)
    lo = (x - hi.astype(F32)).astype(BF16)
    return jnp.dot(f3, jnp.concatenate([hi, hi, lo], axis=0), preferred_element_type=F32)


def _fft1_kernel(u_ref, f_ref, yr_ref, yi_ref, *, n1):
    P, _, G, C = u_ref.shape[1:]
    y = _dot3(f_ref[...], u_ref[0].reshape(P * (n1 // 2) * G, C))
    yr_ref[0] = y[:n1 * G].reshape(n1, G, C)
    yi_ref[0] = y[n1 * G:].reshape(n1, G, C)


def _fft_stage1(u, n1, n2, pair):
    B, L, C = u.shape
    P = 2 if pair else 1
    fr, fi = _dft(n1)
    G = FFT_G
    eye = np.eye(G)
    kr, ki = np.kron(fr[:, :n1 // 2], eye), np.kron(fi[:, :n1 // 2], eye)
    f3 = _split3(np.block([[kr, -ki], [ki, kr]]) if pair else np.concatenate([kr, ki], 0))
    blk = pl.BlockSpec((1, P, n1 // 2, G, C), lambda b, j: (b, 0, 0, j, 0))
    oblk = pl.BlockSpec((1, n1, G, C), lambda b, j: (b, 0, j, 0))
    shp = jax.ShapeDtypeStruct((B // P, n1, n2, C), F32)
    yr, yi = pl.pallas_call(
        functools.partial(_fft1_kernel, n1=n1), grid=(B // P, n2 // G),
        in_specs=[blk, pl.BlockSpec(f3.shape, lambda b, j: (0, 0))],
        out_specs=[oblk, oblk], out_shape=[shp, shp],
        compiler_params=_cparams(("parallel", "parallel")), name="fft_stage1")(
            u.reshape(B // P, P, n1 // 2, n2, C), f3)
    return yr.reshape(B // P, n1 * n2, C), yi.reshape(B // P, n1 * n2, C)


def _cmul(ar, ai, br, bi):
    return ar * br - ai * bi, ar * bi + ai * br


def _fft2_fwd(yr, yi, tr, ti, f2, n2, C):
    ar, ai = _cmul(yr, yi, tr, ti)
    p = _dot3(f2, jnp.concatenate([ar, ai], axis=1))
    return p[:n2, :C] - p[n2:, C:], p[:n2, C:] + p[n2:, :C]


FFT_ROWS = 512


def _fft2_filter_kernel(yr_ref, yi_ref, tr_ref, ti_ref, f_ref, l1_ref, kr_ref, ki_ref, *, n2, n):
    C = yr_ref.shape[2]
    f2 = f_ref[...]
    scale = 1.0 / (l1_ref[...] * n)
    for kb in range(FFT_ROWS // n2):
        rws = slice(kb * n2, (kb + 1) * n2)
        tr = jnp.tile(tr_ref[kb], (1, C // LANES))
        ti = jnp.tile(ti_ref[kb], (1, C // LANES))
        gr, gi = _fft2_fwd(yr_ref[0, rws, :], yi_ref[0, rws, :], tr, ti, f2, n2, C)
        hr, hi = _fft2_fwd(yr_ref[1, rws, :], yi_ref[1, rws, :], tr, ti, f2, n2, C)
        kr_ref[rws, :] = (gr + hr) * scale
        ki_ref[rws, :] = (gi - hi) * scale


def _fft2_conv_kernel(yr_ref, yi_ref, tr_ref, ti_ref, f_ref, kr_ref, ki_ref, zr_ref, zi_ref, *, n2):
    C = yr_ref.shape[2]
    f2 = f_ref[...]
    for kb in range(FFT_ROWS // n2):
        rws = slice(kb * n2, (kb + 1) * n2)
        tr = jnp.tile(tr_ref[kb], (1, C // LANES))
        ti = jnp.tile(ti_ref[kb], (1, C // LANES))
        xr, xi = _fft2_fwd(yr_ref[0, rws, :], yi_ref[0, rws, :], tr, ti, f2, n2, C)
        vr, vi = _cmul(xr, xi, kr_ref[rws, :], ki_ref[rws, :])
        q = _dot3(f2, jnp.concatenate([vr, vi], axis=1))
        wr = q[:n2, :C] + q[n2:, C:]
        wi = q[:n2, C:] - q[n2:, :C]
        zr, zi = _cmul(wr, wi, tr, -ti)
        zr_ref[0, rws, :] = zr
        zi_ref[0, rws, :] = zi


def _fft_tables(n1, n2):
    n = n1 * n2
    k1 = jnp.arange(n1, dtype=jnp.int32)[:, None]
    j2 = jnp.arange(n2, dtype=jnp.int32)[None, :]
    a = (-2.0 * math.pi / n) * (k1 * j2).astype(F32)
    tr = jnp.broadcast_to(jnp.cos(a)[:, :, None], (n1, n2, LANES))
    ti = jnp.broadcast_to(jnp.sin(a)[:, :, None], (n1, n2, LANES))
    fr, fi = _dft(n2)
    return tr, ti, _split3(np.concatenate([fr, fi], 0))


def _fft_stage2_filter(yr, yi, tables, l1, n1, n2):
    _, N, C = yr.shape
    tr, ti, f2 = tables
    kb = FFT_ROWS // n2
    blk = pl.BlockSpec((2, FFT_ROWS, C), lambda k: (0, k, 0))
    tblk = pl.BlockSpec((kb, n2, LANES), lambda k: (k, 0, 0))
    oblk = pl.BlockSpec((FFT_ROWS, C), lambda k: (k, 0))
    shp = jax.ShapeDtypeStruct((N, C), F32)
    return pl.pallas_call(
        functools.partial(_fft2_filter_kernel, n2=n2, n=N), grid=(N // FFT_ROWS,),
        in_specs=[blk, blk, tblk, tblk, pl.BlockSpec(f2.shape, lambda k: (0, 0)),
                  pl.BlockSpec((1, C), lambda k: (0, 0))],
        out_specs=[oblk, oblk], out_shape=[shp, shp],
        compiler_params=_cparams(("parallel",)), name="fft_stage2_filter")(yr, yi, tr, ti, f2, l1)


def _fft_stage2_conv(yr, yi, tables, spec_r, spec_i, n1, n2):
    B, N, C = yr.shape
    tr, ti, f2 = tables
    kb = FFT_ROWS // n2
    blk = pl.BlockSpec((1, FFT_ROWS, C), lambda b, k: (b, k, 0))
    tblk = pl.BlockSpec((kb, n2, LANES), lambda b, k: (k, 0, 0))
    sblk = pl.BlockSpec((FFT_ROWS, C), lambda b, k: (k, 0))
    shp = jax.ShapeDtypeStruct((B, N, C), F32)
    return pl.pallas_call(
        functools.partial(_fft2_conv_kernel, n2=n2), grid=(B, N // FFT_ROWS),
        in_specs=[blk, blk, tblk, tblk, pl.BlockSpec(f2.shape, lambda b, k: (0, 0)), sblk, sblk],
        out_specs=[blk, blk], out_shape=[shp, shp],
        compiler_params=_cparams(("parallel", "parallel")), name="fft_stage2_conv")(
            yr, yi, tr, ti, f2, spec_r, spec_i)


def _fft3_kernel(zr_ref, zi_ref, f_ref, x0_ref, u_ref, bias_ref, o_ref):
    n1, G, C = zr_ref.shape[1:]
    P = u_ref.shape[1]
    z = jnp.concatenate([zr_ref[0].reshape(n1 * G, C), zi_ref[0].reshape(n1 * G, C)], axis=0)
    conv = _dot3(f_ref[...], z).reshape(P, n1 // 2, G, C)
    o_ref[0] = x0_ref[0] * (conv + u_ref[0] * bias_ref[...])


def _fft_stage3(zr, zi, x0, u, bias, n1, n2, pair):
    B, L, C = u.shape
    P = 2 if pair else 1
    fr, fi = _dft(n1)
    G = FFT_G
    eye = np.eye(G)
    kr, ki = np.kron(fr[:n1 // 2], eye), np.kron(fi[:n1 // 2], eye)
    f3 = _split3(np.block([[kr, ki], [-ki, kr]]) if pair else np.concatenate([kr, ki], 1))
    zblk = pl.BlockSpec((1, n1, G, C), lambda b, j: (b, 0, j, 0))
    ublk = pl.BlockSpec((1, P, n1 // 2, G, C), lambda b, j: (b, 0, 0, j, 0))
    v5 = lambda a: a.reshape(B // P, P, n1 // 2, n2, C)
    out = pl.pallas_call(
        _fft3_kernel, grid=(B // P, n2 // G),
        in_specs=[zblk, zblk, pl.BlockSpec(f3.shape, lambda b, j: (0, 0)), ublk, ublk,
                  pl.BlockSpec((1, C), lambda b, j: (0, 0))],
        out_specs=ublk, out_shape=jax.ShapeDtypeStruct((B // P, P, n1 // 2, n2, C), F32),
        compiler_params=_cparams(("parallel", "parallel")), name="fft_stage3")(
            zr.reshape(B // P, n1, n2, C), zi.reshape(B // P, n1, n2, C), f3, v5(x0), v5(u), bias)
    return out.reshape(B * L, C)


def _hyena(proj, conv_w, conv_b, filt, hy_bias, row0, B, L):
    n1, n2 = _fft_factors(2 * L)
    pair = B % 2 == 0
    x0, u = _short_conv(proj, conv_w, conv_b, row0, B, L)
    x0 = x0.reshape(B, L, HY_WIDTH)
    u = u.reshape(B, L, HY_WIDTH)
    tables = _fft_tables(n1, n2)
    hfb, l1 = _hyena_filter(L, *filt)
    fr, fi = _fft_stage1(hfb, n1, n2, False)
    sr, si = _fft_stage2_filter(fr, fi, tables, l1, n1, n2)
    yr, yi = _fft_stage1(u, n1, n2, pair)
    zr, zi = _fft_stage2_conv(yr, yi, tables, sr, si, n1, n2)
    return _fft_stage3(zr, zi, x0, u, hy_bias, n1, n2, pair)


def _group_specs(parts, tm):
    starts, specs = [], []
    s = 0
    for a in parts:
        n = a.shape[0] // tm
        starts.append(s)
        specs.append(pl.BlockSpec((tm, a.shape[1]),
                                  (lambda s, n: lambda i, *_: (jnp.clip(i - s, 0, n - 1), 0))(s, n)))
        s += n
    return starts, specs


def _group_pick(refs, starts):
    i = pl.program_id(0)
    v = refs[0][...]
    for r, s in zip(refs[1:], starts[1:]):
        v = jnp.where(i >= s, r[...], v)
    return v


def _merge_kernel(*refs, starts):
    ng = len(starts)
    x_ref, g_ref, wg_ref = refs[:3]
    y_refs = [refs[3 + j * ng:3 + (j + 1) * ng] for j in range(N_BRANCHES)]
    wa_ref, wb_ref, wc_ref, wo_ref, o_ref = refs[3 + N_BRANCHES * ng:]
    x = x_ref[...]
    xn = _rms(x, g_ref[...]).astype(BF16)
    merged = None
    for j, w_ref in enumerate((wa_ref, wb_ref, wc_ref)):
        gate = jnp.dot(xn, wg_ref[:, j * D_MODEL:(j + 1) * D_MODEL], preferred_element_type=F32)
        y = _group_pick(y_refs[j], starts)
        br = jnp.dot(y.astype(BF16), w_ref[...], preferred_element_type=F32)
        term = jax.nn.sigmoid(gate) * br
        merged = term if merged is None else merged + term
    o_ref[...] = x + jnp.dot(merged.astype(BF16), wo_ref[...], preferred_element_type=F32)


def _merge(x, g, wg, ya, yb, yc, wa, wb, wc, wo):
    T = x.shape[0]
    tm = min(512, T)
    xspec = pl.BlockSpec((tm, D_MODEL), lambda i: (i, 0))
    const = lambda shape: pl.BlockSpec(shape, lambda i: (0,) * len(shape))
    starts, yspecs = _group_specs(ya, tm)
    return pl.pallas_call(
        functools.partial(_merge_kernel, starts=starts), grid=(T // tm,),
        in_specs=[xspec, const((1, D_MODEL)), const((D_MODEL, 3 * D_MODEL))] + yspecs * N_BRANCHES
        + [const((BRANCH_W, D_MODEL)), const((BRANCH_W, D_MODEL)), const((BRANCH_W, D_MODEL)),
           const((D_MODEL, D_MODEL))],
        out_specs=xspec, out_shape=jax.ShapeDtypeStruct((T, D_MODEL), F32),
        compiler_params=_cparams(("parallel",)), name="merge_out_projection")(
            x, g, wg, *ya, *yb, *yc, wa, wb, wc, wo)


def _route(xn, w, b):
    lg = jnp.dot(xn, w, precision=HIGHEST, preferred_element_type=F32) + b
    lane = lax.broadcasted_iota(jnp.int32, lg.shape, 1).astype(F32)
    neg = -jnp.inf
    is_g = lane < N_GROUPS
    gl = jnp.where(is_g, lg, neg)
    gmax = jnp.max(gl, axis=1, keepdims=True)
    gsel = jnp.min(jnp.where(gl == gmax, lane, float(LANES)), axis=1, keepdims=True)
    gprob = 1.0 / jnp.sum(jnp.where(is_g, jnp.exp(lg - gmax), 0.0), axis=1, keepdims=True)
    lo = N_GROUPS + gsel * EXPERTS_PER_GROUP
    el = jnp.where((lane >= lo) & (lane < lo + EXPERTS_PER_GROUP), lg, neg)
    v1 = jnp.max(el, axis=1, keepdims=True)
    i1 = jnp.min(jnp.where(el == v1, lane, float(LANES)), axis=1, keepdims=True)
    el2 = jnp.where(lane == i1, neg, el)
    v2 = jnp.max(el2, axis=1, keepdims=True)
    i2 = jnp.min(jnp.where(el2 == v2, lane, float(LANES)), axis=1, keepdims=True)
    e = jnp.exp(v2 - v1)
    w1 = gprob / (1.0 + e)
    w2 = w1 * e
    return jnp.where(lane == 0, i1 - N_GROUPS,
                     jnp.where(lane == 1, i2 - N_GROUPS,
                               jnp.where(lane == 2, w1, jnp.where(lane == 3, w2, 0.0))))


def _router_kernel(x_ref, g_ref, w_ref, b_ref, xn_ref, r_ref):
    xn = _rms(x_ref[...], g_ref[...])
    xn_ref[...] = xn
    r_ref[...] = _route(xn, w_ref[...], b_ref[...])


def _router(x, g, w, b):
    T = x.shape[0]
    tm = min(512, T)
    xspec = pl.BlockSpec((tm, D_MODEL), lambda i: (i, 0))
    const = lambda shape: pl.BlockSpec(shape, lambda i: (0,) * len(shape))
    return pl.pallas_call(
        _router_kernel, grid=(T // tm,),
        in_specs=[xspec, const((1, D_MODEL)), const((D_MODEL, LANES)), const((1, LANES))],
        out_specs=[xspec, pl.BlockSpec((tm, LANES), lambda i: (i, 0))],
        out_shape=[jax.ShapeDtypeStruct((T, D_MODEL), F32), jax.ShapeDtypeStruct((T, LANES), F32)],
        compiler_params=_cparams(("parallel",)), name="router")(x, g, w, b)


MOE_ROWS = 512


def _moe_kernel(be_ref, nu_ref, idx_hbm, xn_hbm, wg0, wu0, wd0, wg1, wu1, wd1, out_hbm,
                idx, xbuf, obuf, isem, gsem, ssem):
    R = MOE_ROWS
    j = pl.program_id(0)
    nu = nu_ref[0]
    a = 2 * j
    b = a + 1

    def idx_copy(blk, slot):
        return pltpu.make_async_copy(idx_hbm.at[blk], idx.at[slot], isem.at[slot])

    def issue_gathers(slot):
        for r in range(R):
            pltpu.make_async_copy(xn_hbm.at[pl.ds(idx[slot, r], 1)], xbuf.at[slot, pl.ds(r, 1)],
                                  gsem.at[slot]).start()

    def wait_gathers(slot):
        pltpu.make_async_copy(xn_hbm.at[pl.ds(0, R)], xbuf.at[slot], gsem.at[slot]).wait()

    def issue_scatters(slot):
        for r in range(R):
            pltpu.make_async_copy(obuf.at[slot, pl.ds(r, 1)], out_hbm.at[pl.ds(idx[slot, R + r], 1)],
                                  ssem.at[slot]).start()

    def wait_scatters(slot):
        pltpu.make_async_copy(obuf.at[slot], out_hbm.at[pl.ds(0, R)], ssem.at[slot]).wait()

    def compute(slot, wg, wu, wd):
        xb = xbuf[slot].astype(BF16)
        h = jax.nn.silu(jnp.dot(xb, wg[0], preferred_element_type=F32)) \
            * jnp.dot(xb, wu[0], preferred_element_type=F32)
        obuf[slot] = jnp.dot(h.astype(BF16), wd[0], preferred_element_type=F32)

    @pl.when(j == 0)
    def _():
        obuf[0] = jnp.zeros((R, D_MODEL), F32)
        n_rows = out_hbm.shape[0]
        for k in (2, 1):
            z = pltpu.make_async_copy(obuf.at[0], out_hbm.at[pl.ds(n_rows - k * R, R)], ssem.at[0])
            z.start()
            z.wait()
        c = idx_copy(0, 0)
        c.start()
        c.wait()
        issue_gathers(0)

        @pl.when(1 < nu)
        def _():
            idx_copy(1, 1).start()

    @pl.when(a < nu)
    def _():
        wait_gathers(0)

        @pl.when(b < nu)
        def _():
            idx_copy(b, 1).wait()
            issue_gathers(1)

        @pl.when(j >= 1)
        def _():
            wait_scatters(0)

        compute(0, wg0, wu0, wd0)
        issue_scatters(0)

        @pl.when(a + 2 < nu)
        def _():
            idx_copy(a + 2, 0).start()

    @pl.when(b < nu)
    def _():
        wait_gathers(1)

        @pl.when(b + 1 < nu)
        def _():
            idx_copy(b + 1, 0).wait()
            issue_gathers(0)

        @pl.when(j >= 1)
        def _():
            wait_scatters(1)

        compute(1, wg1, wu1, wd1)
        issue_scatters(1)

        @pl.when(b + 2 < nu)
        def _():
            idx_copy(b + 2, 1).start()

    @pl.when((a < nu) & (a + 2 >= nu))
    def _():
        wait_scatters(0)

        @pl.when((b < nu) | (j >= 1))
        def _():
            wait_scatters(1)


def _moe_experts(xn, idx, block_e, n_used, wg, wu, wd):
    T = xn.shape[0]
    R = MOE_ROWS
    n_blocks = idx.shape[0]
    assert n_blocks % 2 == 0
    wspec = lambda o, shp: pl.BlockSpec(shp, lambda j, be, nu: (be[2 * j + o], 0, 0))
    w_in, w_out = (1, D_MODEL, EXPERT_HIDDEN), (1, EXPERT_HIDDEN, D_MODEL)
    grid_spec = pltpu.PrefetchScalarGridSpec(
        num_scalar_prefetch=2, grid=(n_blocks // 2,),
        in_specs=[pl.BlockSpec(memory_space=pl.ANY), pl.BlockSpec(memory_space=pl.ANY),
                  wspec(0, w_in), wspec(0, w_in), wspec(0, w_out),
                  wspec(1, w_in), wspec(1, w_in), wspec(1, w_out)],
        out_specs=pl.BlockSpec(memory_space=pl.ANY),
        scratch_shapes=[pltpu.SMEM((2, 2 * R), jnp.int32), pltpu.VMEM((2, R, D_MODEL), F32),
                        pltpu.VMEM((2, R, D_MODEL), F32), pltpu.SemaphoreType.DMA((2,)),
                        pltpu.SemaphoreType.DMA((2,)), pltpu.SemaphoreType.DMA((2,))])
    return pl.pallas_call(
        _moe_kernel, grid_spec=grid_spec, out_shape=jax.ShapeDtypeStruct((TOP_K * T + 2 * R, D_MODEL), F32),
        compiler_params=_cparams(("arbitrary",)), name="moe_experts")(
            block_e, n_used, idx, xn, wg, wu, wd, wg, wu, wd)


def _dispatch(route, T):
    R = MOE_ROWS
    M = T * TOP_K
    expert_id = route[:, :TOP_K].astype(jnp.int32).reshape(-1)
    order = jnp.argsort(expert_id).astype(jnp.int32)
    counts =jnp.bincount(expert_id, length=N_EXPERTS).astype(jnp.int32)
    starts = jnp.cumsum(counts) - counts
    padded = (counts + R - 1) // R * R
    p_ends = jnp.cumsum(padded)
    p_starts = p_ends - padded
    n_blocks = -(-M // R) + N_EXPERTS
    first_row = jnp.arange(n_blocks, dtype=jnp.int32) * R
    block_e = jnp.minimum(jnp.sum(p_ends[None, :] <= first_row[:, None], axis=1), N_EXPERTS - 1).astype(jnp.int32)
    row = jnp.arange(n_blocks * R, dtype=jnp.int32).reshape(n_blocks, R)
    k = row - p_starts[block_e][:, None]
    valid = (k < counts[block_e][:, None]) & (row < p_ends[N_EXPERTS - 1])
    a = order[jnp.clip(starts[block_e][:, None] + k, 0, M - 1)]
    token = a >> 1
    src = jnp.where(valid, token, 0)
    scrap = M + row % (2 * R)
    dst = jnp.where(valid, (a & 1) * T + token, scrap)
    n_used = (p_ends[N_EXPERTS - 1:] // R).astype(jnp.int32)
    return jnp.concatenate([src, dst], axis=1), block_e, n_used


def _final_kernel(x_ref, e0_ref, e1_ref, r_ref, g_ref, o_ref):
    r = r_ref[...]
    x = x_ref[...] + e0_ref[...] * r[:, 2:3] + e1_ref[...] * r[:, 3:4]
    o_ref[...] = _rms(x, g_ref[...])


def _final(x, e2, route, g, row0, rows):
    T = x.shape[0]
    tm = min(512, rows)
    nT = T // tm
    b0 = row0 // tm
    xspec = pl.BlockSpec((tm, D_MODEL), lambda i: (b0 + i, 0))
    return pl.pallas_call(
        _final_kernel, grid=(rows // tm,),
        in_specs=[xspec, xspec, pl.BlockSpec((tm, D_MODEL), lambda i: (nT + b0 + i, 0)),
                  pl.BlockSpec((tm, LANES), lambda i: (b0 + i, 0)), pl.BlockSpec((1, D_MODEL), lambda i: (0, 0))],
        out_specs=pl.BlockSpec((tm, D_MODEL), lambda i: (i, 0)),
        out_shape=jax.ShapeDtypeStruct((rows, D_MODEL), F32),
        compiler_params=_cparams(("parallel",)), name="combine_final_norm")(x, e2, e2, route, g)


def _pack_layer(l, p):
    w = p['w_in'][l]
    o_gate = 3 * HG_WIDTH + 2 * HG_WIDTH + 4 * ML_WIDTH
    o_hy = o_gate + 4 * ML_HEADS
    o_g = o_hy + 3 * HY_WIDTH
    gates = jnp.pad(w[:, o_gate:o_hy], ((0, 0), (0, PROJ_W - PROJ_MAIN - 4 * ML_HEADS)))
    w_proj = jnp.concatenate([w[:, :o_gate], w[:, o_hy:o_g], gates], axis=1).astype(BF16)
    w_gate = w[:, o_g:].astype(BF16)
    gate_bias = jnp.pad(jnp.concatenate([p['ml_i_bias'][l].reshape(-1), p['ml_f_bias'][l].reshape(-1)]),
                        (0, LANES - 4 * ML_HEADS)).astype(F32)[None]
    w_router = jnp.pad(jnp.concatenate([p['router_group_w'][l], p['router_expert_w'][l]], axis=1),
                       ((0, 0), (0, LANES - N_GROUPS - N_EXPERTS))).astype(F32)
    b_router = jnp.pad(jnp.concatenate([p['router_group_b'][l], p['router_expert_b'][l]]),
                       (0, LANES - N_GROUPS - N_EXPERTS)).astype(F32)[None]
    return dict(
        norm_mix_g=p['norm_mix_g'][l].astype(F32)[None], w_proj=w_proj, w_gate=w_gate,
        hg_norm_g=p['hg_norm_g'][l].astype(F32)[None], gate_bias=gate_bias,
        ml_norm_g=p['ml_norm_g'][l].astype(F32)[None],
        conv_w=jnp.pad(p['hy_conv_w'][l].astype(F32), ((0, 5), (0, 0))), conv_b=p['hy_conv_b'][l].astype(F32)[None],
        filt=(p['filt_w1'][l], p['filt_b1'][l], p['filt_w2'][l], p['filt_b2'][l], p['filt_freq'][l],
              p['filt_w3'][l]),
        hy_bias=p['hy_bias'][l].astype(F32)[None],
        wa=p['w_branch_a'][l].astype(BF16), wb=p['w_branch_b'][l].astype(BF16),
        wc=p['w_branch_c'][l].astype(BF16), wo=p['w_out'][l].astype(BF16),
        norm_ffn_g=p['norm_ffn_g'][l].astype(F32)[None], w_router=w_router, b_router=b_router,
        wg=p['exp_w_gate'][l].astype(BF16), wu=p['exp_w_up'][l].astype(BF16),
        wd=p['exp_w_down'][l].astype(BF16))


def _trunk(x, groups, p):
    T = x.shape[0]
    lbs = jnp.cumsum(jax.nn.softmax(p['hg_lb'].astype(F32), axis=0), axis=0)
    lbs = lbs - lbs[0:1]
    moe = None
    for l in range(DEPTH):
        lp = _pack_layer(l, p)
        lb = lbs[l][None]
        lbrow = jnp.concatenate([jnp.log(lb), jnp.log1p(-lb), 1.0 - lb, jnp.zeros((5, HG_WIDTH), F32)], axis=0)
        if moe is None:
            proj = _in_projection(x, lp['norm_mix_g'], lp['w_proj'])
        else:
            proj, x = _in_projection(x, lp['norm_mix_g'], lp['w_proj'], moe)
        ya, yb, yc = [], [], []
        for row0, B, L in groups:
            ya.append(_hgrn2(proj, lbrow, lp['hg_norm_g'], row0, B, L))
            yb.append(_mlstm(proj, lp['gate_bias'], lp['ml_norm_g'], row0, B, L))
            yc.append(_hyena(proj, lp['conv_w'], lp['conv_b'], lp['filt'], lp['hy_bias'], row0, B, L))
        x = _merge(x, lp['norm_mix_g'], lp['w_gate'], ya, yb, yc, lp['wa'], lp['wb'], lp['wc'], lp['wo'])
        xn, route = _router(x, lp['norm_ffn_g'], lp['w_router'], lp['b_router'])
        rows, block_e, n_used = _dispatch(route, T)
        e2 = _moe_experts(xn, rows, block_e, n_used, lp['wg'], lp['wu'], lp['wd'])
        moe = (e2, route)
    g = p['final_norm_g'].astype(F32)[None]
    return [_final(x, moe[0], moe[1], g, row0, B * L) for row0, B, L in groups]


def kernel(x_prompt, x_sample, norm_mix_g, w_in, hg_lb, hg_norm_g, ml_i_bias, ml_f_bias, ml_norm_g, hy_conv_w, hy_conv_b, filt_w1, filt_b1, filt_w2, filt_b2, filt_freq, filt_w3, hy_bias, w_branch_a, w_branch_b, w_branch_c, w_out, norm_ffn_g, router_group_w, router_group_b, router_expert_w, router_expert_b, exp_w_gate, exp_w_up, exp_w_down, final_norm_g):
    p = dict(norm_mix_g=norm_mix_g, w_in=w_in, hg_lb=hg_lb, hg_norm_g=hg_norm_g, ml_i_bias=ml_i_bias,
             ml_f_bias=ml_f_bias, ml_norm_g=ml_norm_g, hy_conv_w=hy_conv_w, hy_conv_b=hy_conv_b,
             filt_w1=filt_w1, filt_b1=filt_b1, filt_w2=filt_w2, filt_b2=filt_b2, filt_freq=filt_freq,
             filt_w3=filt_w3, hy_bias=hy_bias, w_branch_a=w_branch_a, w_branch_b=w_branch_b,
             w_branch_c=w_branch_c, w_out=w_out, norm_ffn_g=norm_ffn_g, router_group_w=router_group_w,
             router_group_b=router_group_b, router_expert_w=router_expert_w,
             router_expert_b=router_expert_b, exp_w_gate=exp_w_gate, exp_w_up=exp_w_up,
             exp_w_down=exp_w_down, final_norm_g=final_norm_g)
    Bp, Lp, _ = x_prompt.shape
    Bs, Ls, _ = x_sample.shape
    Tp, Ts = Bp * Lp, Bs * Ls
    x = jnp.concatenate([x_prompt.reshape(Tp, D_MODEL), x_sample.reshape(Ts, D_MODEL)], axis=0).astype(F32)
    yp, ys = _trunk(x, ((0, Bp, Lp), (Tp, Bs, Ls)), p)
    return (yp.reshape(Bp, Lp, D_MODEL), ys.reshape(Bs, Ls, D_MODEL))
```

```python
import functools
import math

import numpy as np
import jax
import jax.numpy as jnp
from jax import lax
from jax.experimental import pallas as pl
from jax.experimental.pallas import tpu as pltpu

F32 = jnp.float32
BF16 = jnp.bfloat16
HIGHEST = lax.Precision.HIGHEST

D_MODEL = 1024
EPS = 1e-6
DEPTH = 2
HG_HEADS, HG_KDIM, HG_WIDTH, HG_CHUNK = 8, 64, 512, 32
ML_HEADS, ML_HDIM, ML_WIDTH, ML_CHUNK = 4, 128, 512, 128
HY_WIDTH, HY_BANDS, HY_FILTER_HIDDEN = 512, 16, 64
HY_EMB = 1 + 2 * HY_BANDS
HY_SHORT_DECAY_PCT, HY_LONG_DECAY_PCT, HY_DECAY_TARGET = 0.3, 1.5, 1e-2
N_GROUPS, EXPERTS_PER_GROUP, TOP_K = 4, 8, 2
N_EXPERTS = N_GROUPS * EXPERTS_PER_GROUP
N_BRANCHES = 3
EXPERT_HIDDEN = D_MODEL // 2
MOE_BLOCK = 128

LANES = 128
BRANCH_W = 512
SLAB_HQ, SLAB_HFF, SLAB_HFB, SLAB_HI, SLAB_HG = 0, 1, 2, 3, 4
SLAB_MQ, SLAB_MK, SLAB_MV, SLAB_MO = 5, 6, 7, 8
SLAB_HY = 9
PROJ_MAIN = 12 * BRANCH_W
PROJ_W = PROJ_MAIN + 2 * LANES
PROJ_TN = 1280
GATE_BLOCK = PROJ_MAIN // LANES
VMEM_LIMIT = 48 * 1024 * 1024
VMEM_LIMIT_WIDE = 56 * 1024 * 1024


def _cparams(sem, vmem=VMEM_LIMIT):
    return pltpu.CompilerParams(dimension_semantics=sem, vmem_limit_bytes=vmem)


def _rms(x, g):
    return x * lax.rsqrt(jnp.mean(x * x, axis=-1, keepdims=True) + EPS) * g


def _log_sigmoid(z):
    return jnp.minimum(z, 0.0) - jnp.log1p(jnp.exp(-jnp.abs(z)))


def _bf16_terms(x):
    t1 = x.astype(BF16)
    r = x - t1.astype(F32)
    t2 = r.astype(BF16)
    return t1, t2, (r - t2.astype(F32)).astype(BF16)


def _inproj_kernel(x_ref, g_ref, w_ref, o_ref, xn_ref):
    @pl.when(pl.program_id(1) == 0)
    def _():
        xn_ref[...] = _rms(x_ref[...], g_ref[...]).astype(BF16)

    o_ref[...] = jnp.dot(xn_ref[...], w_ref[...], preferred_element_type=F32)


def _combine_kernel(x_ref, e0_ref, e1_ref, r_ref, o_ref):
    r = r_ref[...]
    o_ref[...] = x_ref[...] + e0_ref[...] * r[:, 2:3] + e1_ref[...] * r[:, 3:4]


def _combine(x, e2, route):
    T = x.shape[0]
    tm = min(1024, T)
    nT = T // tm
    xspec = pl.BlockSpec((tm, D_MODEL), lambda i: (i, 0))
    return pl.pallas_call(
        _combine_kernel, grid=(nT,),
        in_specs=[xspec, xspec, pl.BlockSpec((tm, D_MODEL), lambda i: (nT + i, 0)),
                  pl.BlockSpec((tm, LANES), lambda i: (i, 0))],
        out_specs=xspec, out_shape=jax.ShapeDtypeStruct((T, D_MODEL), F32),
        compiler_params=_cparams(("parallel",)), name="moe_combine")(x, e2, e2, route)


def _in_projection(x, g, w):
    T = x.shape[0]
    tm = min(2048, T)
    tn = PROJ_TN
    return pl.pallas_call(
        _inproj_kernel, grid=(T // tm, PROJ_W // tn),
        in_specs=[pl.BlockSpec((tm, D_MODEL), lambda i, j: (i, 0)), pl.BlockSpec((1, D_MODEL), lambda i, j: (0, 0)),
                  pl.BlockSpec((D_MODEL, tn), lambda i, j: (0, j))],
        out_specs=pl.BlockSpec((tm, tn), lambda i, j: (i, j)),
        out_shape=jax.ShapeDtypeStruct((T, PROJ_W), F32), scratch_shapes=[pltpu.VMEM((tm, D_MODEL), BF16)],
        compiler_params=_cparams(("parallel", "arbitrary"), VMEM_LIMIT_WIDE), name="in_projection")(x, g, w)


HG_TB = 256
HG_PAIRS = HG_WIDTH // LANES


HG_UNROLL = 8
HG_SUB = 8
HG_NSUB = HG_CHUNK // HG_SUB
HG_XROWS = (HG_NSUB - 1) * HG_HEADS * HG_SUB
HG_XCOLS = HG_SUB * HG_NSUB * (HG_NSUB - 1) // 2


def _hgrn2_tiles(reverse):
    if reverse:
        return [(i, (i + 1) * HG_SUB, HG_CHUNK - (i + 1) * HG_SUB) for i in range(HG_NSUB - 1)]
    return [(i, 0, i * HG_SUB) for i in range(1, HG_NSUB)]


def _hgrn2_kernel(*refs, reverse, finalize):
    if finalize:
        (q_ref, z_ref, v_ref, lb_ref, tri_ref, bd_ref, xm_ref, of_ref, gate_ref, ng_ref,
         o_ref, st_ref, acc_ref) = refs
    else:
        (q_ref, z_ref, v_ref, lb_ref, tri_ref, bd_ref, xm_ref, o_ref, st_ref) = refs
        acc_ref = o_ref
    C, c, nb, W = HG_CHUNK, HG_SUB, HG_NSUB, HG_WIDTH
    n_chunks = HG_TB // C

    @pl.when(pl.program_id(1) == 0)
    def _():
        st_ref[...] = jnp.zeros_like(st_ref)

    log_lb = lb_ref[0:1, :]
    log_1mlb = lb_ref[1:2, :]
    one_m_lb = lb_ref[2:3, :]
    tri3 = tri_ref[...]
    bd = bd_ref[...]
    bd_bf = bd.astype(BF16)
    xmask = xm_ref[...]
    sub_row = lax.broadcasted_iota(jnp.int32, (nb, c, W), 1)
    lane_head = lax.broadcasted_iota(jnp.int32, (c, W), 1) // HG_KDIM
    tiles = _hgrn2_tiles(reverse)

    def chunk(i):
        ci = (n_chunks - 1 - i) if reverse else i
        off = pl.multiple_of(ci * C, C)
        q = q_ref[pl.ds(off, C), :]
        z = z_ref[pl.ds(off, C), :]
        v = v_ref[pl.ds(off, C), :]
        ls = jnp.minimum(z, 0.0) - jnp.log(1.0 + jnp.exp(-jnp.abs(z)))
        hi = log_1mlb + ls
        mx = jnp.maximum(log_lb, hi)
        lf = mx + jnp.log(1.0 + jnp.exp(-jnp.abs(log_lb - hi)))
        kk = one_m_lb * jnp.exp(ls - z)
        b = jnp.dot(tri3, jnp.concatenate(_bf16_terms(lf), axis=0), preferred_element_type=F32)
        btot = b[0:1, :] if reverse else b[C - 1:C, :]
        qb = (q * jnp.exp(b)).astype(BF16)
        kb = (kk * jnp.exp(btot - b)).astype(BF16)
        dec = jnp.exp(btot)
        vb = v.astype(BF16)

        q3, k3, v3 = (a.reshape(nb, c, W) for a in (q, kk, v))
        one = 1 if not reverse else c - 1
        f3 = (1.0 - kk).reshape(nb, c, W)
        kd = k3
        a_rows = [(q * kk).astype(BF16)]
        for d in range(1, c):
            ok = (sub_row < c - d) if reverse else (sub_row >= d)
            kd = pltpu.roll(kd, one, axis=1) * f3
            a_rows.append(jnp.where(ok, q3 * kd, 0.0).reshape(C, W).astype(BF16))
        a_all = jnp.concatenate(a_rows, axis=0)
        sums = jnp.concatenate(
            [jnp.dot(a_all[:, p * LANES:(p + 1) * LANES], bd_bf, preferred_element_type=F32)
             for p in range(HG_PAIRS)], axis=1)
        acc3 = sums[0:C].reshape(nb, c, W) * v3
        v_sh = v3
        for d in range(1, c):
            v_sh = pltpu.roll(v_sh, one, axis=1)
            acc3 = acc3 + sums[d * C:(d + 1) * C].reshape(nb, c, W) * v_sh
        acc = acc3.reshape(C, W)

        qx, kx, vx = [], [], []
        for (ti, s0, sn) in tiles:
            edge = s0 if reverse else s0 + sn - 1
            r = b[edge:edge + 1, :]
            rows = slice(ti * c, (ti + 1) * c)
            qh = q[rows] * jnp.exp(b[rows] - r)
            qx += [jnp.where(lane_head == h, qh, 0.0) for h in range(HG_HEADS)]
            kx.append(kk[s0:s0 + sn] * jnp.exp(r - b[s0:s0 + sn]))
            vx.append(v[s0:s0 + sn])
        qx = jnp.concatenate(qx, axis=0).astype(BF16)
        kx = jnp.concatenate(kx, axis=0).astype(BF16)
        vx = jnp.concatenate(vx, axis=0).astype(BF16)
        sc = lax.dot_general(qx, kx, (((1,), (1,)), ((), ())), preferred_element_type=F32) * xmask
        px = jnp.dot(sc.astype(BF16), vx, preferred_element_type=F32)
        offd = {}
        for n, (ti, s0, sn) in enumerate(tiles):
            base = n * HG_HEADS * c
            t_acc = jnp.where(lane_head == 0, px[base:base + c], 0.0)
            for h in range(1, HG_HEADS):
                t_acc = t_acc + jnp.where(lane_head == h, px[base + h * c:base + (h + 1) * c], 0.0)
            offd[ti] = t_acc
        acc = acc + jnp.concatenate([offd.get(ti, jnp.zeros((c, W), F32)) for ti in range(nb)], axis=0)

        parts = []
        for p in range(HG_PAIRS):
            sl = slice(p * LANES, (p + 1) * LANES)
            s_t = st_ref[p]
            parts.append(lax.dot_general(qb[:, sl], s_t.astype(BF16), (((1,), (1,)), ((), ())),
                                         preferred_element_type=F32))
            upd = lax.dot_general(vb[:, sl], kb[:, sl], (((0,), (0,)), ((), ())),
                                  preferred_element_type=F32)
            st_ref[p] = s_t * dec[:, sl] + upd * bd
        acc = acc + jnp.concatenate(parts, axis=1)

        acc_ref[pl.ds(off, C), :] = acc

    def chunk_group(j, carry):
        for u in range(HG_UNROLL):
            chunk(HG_UNROLL * j + u)
        return carry

    lax.fori_loop(0, n_chunks // HG_UNROLL, chunk_group, 0)

    if finalize:
        o = acc_ref[...] + of_ref[...]
        o2 = (o * o).astype(BF16)
        ms = jnp.concatenate(
            [jnp.dot(o2[:, p * LANES:(p + 1) * LANES], bd_bf, preferred_element_type=F32)
             for p in range(HG_PAIRS)], axis=1) * (1.0 / HG_KDIM)
        g = gate_ref[...]
        y = o * lax.rsqrt(ms + EPS) * ng_ref[...] * (g * jax.nn.sigmoid(g))
        o_ref[...] = y.astype(o_ref.dtype)


def _hgrn2(proj, lbrow, norm_g, row0, B, L):
    C = HG_CHUNK
    nT = L // HG_TB
    blk0 = row0 // HG_TB
    tri_f = jnp.asarray(np.tile(np.tril(np.ones((C, C), np.float32)), (1, 3))).astype(BF16)
    tri_b = jnp.asarray(np.tile(np.triu(np.ones((C, C), np.float32)), (1, 3))).astype(BF16)
    head = np.arange(LANES) // HG_KDIM
    bd = jnp.asarray((head[:, None] == head[None, :]).astype(np.float32))

    def tile_mask(reverse):
        m = np.zeros((HG_XROWS, HG_XCOLS), np.float32)
        col = 0
        for n, (_, _, sn) in enumerate(_hgrn2_tiles(reverse)):
            m[n * HG_HEADS * HG_SUB:(n + 1) * HG_HEADS * HG_SUB, col:col + sn] = 1.0
            col += sn
        return jnp.asarray(m)

    def in_spec(slab, reverse):
        if reverse:
            return pl.BlockSpec((HG_TB, BRANCH_W), lambda b, t: (blk0 + b * nT + nT - 1 - t, slab))
        return pl.BlockSpec((HG_TB, BRANCH_W), lambda b, t: (blk0 + b * nT + t, slab))

    def out_spec(reverse):
        if reverse:
            return pl.BlockSpec((HG_TB, BRANCH_W), lambda b, t: (b * nT + nT - 1 - t, 0))
        return pl.BlockSpec((HG_TB, BRANCH_W), lambda b, t: (b * nT + t, 0))

    const = lambda shape: pl.BlockSpec(shape, lambda b, t: (0,) * len(shape))
    scratch = [pltpu.VMEM((HG_PAIRS, LANES, LANES), F32)]
    consts = [const((8, BRANCH_W)), const((C, 3 * C)), const((LANES, LANES)), const((HG_XROWS, HG_XCOLS))]
    cp = _cparams(("parallel", "arbitrary"))
    o_f = pl.pallas_call(
        functools.partial(_hgrn2_kernel, reverse=False, finalize=False), grid=(B, nT),
        in_specs=[in_spec(SLAB_HQ, False), in_spec(SLAB_HFF, False), in_spec(SLAB_HI, False)] + consts,
        out_specs=out_spec(False), out_shape=jax.ShapeDtypeStruct((B * L, BRANCH_W), F32),
        scratch_shapes=scratch, compiler_params=cp, name="hgrn2_fwd")(
            proj, proj, proj, lbrow, tri_f, bd, tile_mask(False))
    return pl.pallas_call(
        functools.partial(_hgrn2_kernel, reverse=True, finalize=True), grid=(B, nT),
        in_specs=[in_spec(SLAB_HQ, True), in_spec(SLAB_HFB, True), in_spec(SLAB_HI, True)] + consts
        + [out_spec(True), in_spec(SLAB_HG, True), const((1, BRANCH_W))],
        out_specs=out_spec(True), out_shape=jax.ShapeDtypeStruct((B * L, BRANCH_W), BF16),
        scratch_shapes=scratch + [pltpu.VMEM((HG_TB, BRANCH_W), F32)], compiler_params=cp, name="hgrn2_bwd")(
            proj, proj, proj, lbrow, tri_b, bd, tile_mask(True), o_f, proj, norm_g)


def _mlstm_kernel(*refs, reverse, finalize):
    if finalize:
        (q_ref, k_ref, v_ref, gt_ref, gb_ref, tri_ref, hf_ref, og_ref, ng_ref,
         o_ref, c_ref, n_ref, m_ref) = refs
    else:
        (q_ref, k_ref, v_ref, gt_ref, gb_ref, tri_ref, o_ref, c_ref, n_ref, m_ref) = refs
    C = ML_CHUNK

    @pl.when(pl.program_id(1) == 0)
    def _():
        c_ref[...] = jnp.zeros_like(c_ref)
        n_ref[...] = jnp.zeros_like(n_ref)
        m_ref[...] = jnp.zeros_like(m_ref)

    tri = tri_ref[...]
    lane = lax.broadcasted_iota(jnp.int32, (C, LANES), 1)
    ti = lax.broadcasted_iota(jnp.int32, (C, C), 0)
    si = lax.broadcasted_iota(jnp.int32, (C, C), 1)
    causal = (si >= ti) if reverse else (si <= ti)
    dsel = ML_HEADS if reverse else 0
    edge = 0 if reverse else C - 1
    scale = 1.0 / math.sqrt(ML_HDIM)
    order = range(ML_TB // C - 1, -1, -1) if reverse else range(ML_TB // C)
    for ci in order:
        _mlstm_chunk(refs, slice(ci * C, (ci + 1) * C), tri, lane, causal, dsel, edge, scale, finalize)


def _mlstm_chunk(refs, rws, tri, lane, causal, dsel, edge, scale, finalize):
    if finalize:
        (q_ref, k_ref, v_ref, gt_ref, gb_ref, tri_ref, hf_ref, og_ref, ng_ref,
         o_ref, c_ref, n_ref, m_ref) = refs
    else:
        (q_ref, k_ref, v_ref, gt_ref, gb_ref, tri_ref, o_ref, c_ref, n_ref, m_ref) = refs
    gates = gt_ref[rws, :] + gb_ref[...]
    g_col = jnp.where(lane >= 2 * ML_HEADS, _log_sigmoid(gates), gates)
    g_row = g_col.T
    b_col = jnp.dot(tri, g_col, precision=HIGHEST, preferred_element_type=F32)
    b_row = lax.dot_general(g_row, tri, (((1,), (1,)), ((), ())), precision=HIGHEST,
                            preferred_element_type=F32)

    outs = []
    for h in range(ML_HEADS):
        sl = slice(h * ML_HDIM, (h + 1) * ML_HDIM)
        li, lf = dsel + h, 2 * ML_HEADS + dsel + h
        q = q_ref[rws, sl]
        kc = k_ref[rws, sl] * scale
        v = v_ref[rws, sl]
        qb, kb, vb = q.astype(BF16), kc.astype(BF16), v.astype(BF16)
        bc = b_col[:, lf:lf + 1]
        br = b_row[lf:lf + 1, :]
        ic = g_col[:, li:li + 1]
        ir = g_row[li:li + 1, :]
        btot = b_col[edge:edge + 1, lf:lf + 1]
        m_prev = m_ref[0:1, h:h + 1]
        n_prev = n_ref[h:h + 1, :]
        c_prev = c_ref[h]

        log_d = jnp.where(causal, bc - br + ir, -jnp.inf)
        log_inter = bc + m_prev
        m_t = jnp.maximum(log_inter, jnp.max(log_d, axis=1, keepdims=True))
        a_inter = jnp.exp(log_inter - m_t)
        s = lax.dot_general(qb, kb, (((1,), (1,)), ((), ())), preferred_element_type=F32)
        s = s * jnp.exp(log_d - m_t)
        num = a_inter * jnp.dot(qb, c_prev.astype(BF16), preferred_element_type=F32) \
            + jnp.dot(s.astype(BF16), vb, preferred_element_type=F32)
        den = a_inter * jnp.sum(q * n_prev, axis=1, keepdims=True) + jnp.sum(s, axis=1, keepdims=True)
        outs.append(num / jnp.maximum(jnp.abs(den), jnp.exp(-m_t)))

        log_w = btot - bc + ic
        m_new = jnp.maximum(btot + m_prev, jnp.max(log_w, axis=0, keepdims=True))
        a_state = jnp.exp(btot + m_prev - m_new)
        kw = kc * jnp.exp(log_w - m_new)
        c_ref[h] = a_state * c_prev + lax.dot_general(kw.astype(BF16), vb, (((0,), (0,)), ((), ())),
                                                      preferred_element_type=F32)
        n_ref[h:h + 1, :] = a_state * n_prev + jnp.sum(kw, axis=0, keepdims=True)
        m_ref[0:1, h:h + 1] = m_new

    hcur = jnp.concatenate(outs, axis=1)
    if finalize:
        hsum = hcur + hf_ref[rws, :]
        ys = []
        for h in range(ML_HEADS):
            sl = slice(h * ML_HDIM, (h + 1) * ML_HDIM)
            hh = hsum[:, sl]
            ys.append(hh * lax.rsqrt(jnp.mean(hh * hh, axis=1, keepdims=True) + EPS))
        y = jnp.concatenate(ys, axis=1) * ng_ref[...] * jax.nn.sigmoid(og_ref[rws, :])
        o_ref[rws, :] = y.astype(o_ref.dtype)
    else:
        o_ref[rws, :] = hcur


ML_TB = 2 * ML_CHUNK


def _mlstm(proj, gate_bias, norm_g, row0, B, L):
    C = ML_TB
    nT = L // C
    blk0 = row0 // C
    tri_f = jnp.asarray(np.tril(np.ones((ML_CHUNK, ML_CHUNK), np.float32)))
    tri_b = jnp.asarray(np.triu(np.ones((ML_CHUNK, ML_CHUNK), np.float32)))

    def rows(reverse):
        if reverse:
            return lambda b, t: blk0 + b * nT + nT - 1 - t
        return lambda b, t: blk0 + b * nT + t

    def in_spec(slab, reverse):
        r = rows(reverse)
        return pl.BlockSpec((C, BRANCH_W), lambda b, t: (r(b, t), slab))

    def gate_spec(reverse):
        r = rows(reverse)
        return pl.BlockSpec((C, LANES), lambda b, t: (r(b, t), GATE_BLOCK))

    def out_spec(reverse):
        if reverse:
            return pl.BlockSpec((C, BRANCH_W), lambda b, t: (b * nT + nT - 1 - t, 0))
        return pl.BlockSpec((C, BRANCH_W), lambda b, t: (b * nT + t, 0))

    const = lambda shape: pl.BlockSpec(shape, lambda b, t: (0,) * len(shape))
    scratch = [pltpu.VMEM((ML_HEADS, ML_HDIM, ML_HDIM), F32), pltpu.VMEM((8, ML_HDIM), F32),
               pltpu.VMEM((8, LANES), F32)]
    cp = _cparams(("parallel", "arbitrary"))
    h_f = pl.pallas_call(
        functools.partial(_mlstm_kernel, reverse=False, finalize=False), grid=(B, nT),
        in_specs=[in_spec(SLAB_MQ, False), in_spec(SLAB_MK, False), in_spec(SLAB_MV, False),
                  gate_spec(False), const((1, LANES)), const((ML_CHUNK, ML_CHUNK))],
        out_specs=out_spec(False), out_shape=jax.ShapeDtypeStruct((B * L, BRANCH_W), F32),
        scratch_shapes=scratch, compiler_params=cp, name="mlstm_fwd")(
            proj, proj, proj, proj, gate_bias, tri_f)
    return pl.pallas_call(
        functools.partial(_mlstm_kernel, reverse=True, finalize=True), grid=(B, nT),
        in_specs=[in_spec(SLAB_MQ, True), in_spec(SLAB_MK, True), in_spec(SLAB_MV, True),
                  gate_spec(True), const((1, LANES)), const((ML_CHUNK, ML_CHUNK)),
                  out_spec(True), in_spec(SLAB_MO, True), const((1, BRANCH_W))],
        out_specs=out_spec(True), out_shape=jax.ShapeDtypeStruct((B * L, BRANCH_W), BF16),
        scratch_shapes=scratch, compiler_params=cp, name="mlstm_bwd")(
            proj, proj, proj, proj, gate_bias, tri_b, h_f, proj, norm_g)


HY_TB = 512


def _shortconv_kernel(c_ref, p_ref, n_ref, w_ref, b_ref, x0_ref, u_ref, *, nT):
    t = pl.program_id(1)
    cur = c_ref[...]
    prev_row = jnp.where(t > 0, p_ref[7:8, :], 0.0)
    next_row = jnp.where(t < nT - 1, n_ref[0:1, :], 0.0)
    tb = cur.shape[0]
    row = lax.broadcasted_iota(jnp.int32, cur.shape, 0)
    up = jnp.where(row == 0, prev_row, pltpu.roll(cur, 1, axis=0))
    dn = jnp.where(row == tb - 1, next_row, pltpu.roll(cur, tb - 1, axis=0))
    y = up * w_ref[0:1, :] + cur * w_ref[1:2, :] + dn * w_ref[2:3, :] + b_ref[...]
    x0_ref[...] = y[:, :HY_WIDTH]
    u_ref[...] = y[:, HY_WIDTH:2 * HY_WIDTH] * y[:, 2 * HY_WIDTH:]


def _short_conv(proj, w, b, row0, B, L):
    tb = min(HY_TB, L)
    nT = L // tb
    blk0 = row0 // tb
    sub = tb // 8
    W3 = 3 * HY_WIDTH
    slab = SLAB_HY * BRANCH_W // W3
    cur = pl.BlockSpec((tb, W3), lambda bb, t: (blk0 + bb * nT + t, slab))
    prv = pl.BlockSpec((8, W3), lambda bb, t: (jnp.maximum((blk0 + bb * nT + t) * sub - 1, 0), slab))
    nxt = pl.BlockSpec((8, W3), lambda bb, t: (jnp.minimum((blk0 + bb * nT + t + 1) * sub,
                                                           (blk0 + B * nT) * sub - 1), slab))
    const = lambda shape: pl.BlockSpec(shape, lambda bb, t: (0,) * len(shape))
    out = pl.BlockSpec((tb, HY_WIDTH), lambda bb, t: (bb * nT + t, 0))
    return pl.pallas_call(
        functools.partial(_shortconv_kernel, nT=nT), grid=(B, nT),
        in_specs=[cur, prv, nxt, const((8, W3)), const((1, W3))], out_specs=[out, out],
        out_shape=[jax.ShapeDtypeStruct((B * L, HY_WIDTH), F32)] * 2,
        compiler_params=_cparams(("parallel", "parallel")), name="hyena_short_conv")(proj, proj, proj, w, b)


HYF_TB = 256


def _filter_kernel(band_ref, w1_ref, b1_ref, w2_ref, b2_ref, fr_ref, w3_ref, dl_ref, h_ref, l1_ref, *, L):
    i = pl.program_id(0)
    pos = (lax.broadcasted_iota(jnp.int32, (HYF_TB, LANES), 0) + i * HYF_TB).astype(F32)
    lane = lax.broadcasted_iota(jnp.int32, (HYF_TB, LANES), 1)
    t = pos / (L - 1)
    ang = (2.0 * math.pi * pos / L) * band_ref[...]
    z = jnp.where(lane == 0, t,
                  jnp.where(lane <= HY_BANDS, jnp.cos(ang),
                            jnp.where(lane <= 2 * HY_BANDS, -jnp.sin(ang), 0.0)))
    fr = fr_ref[...]
    h = jnp.sin(fr * (jnp.dot(z, w1_ref[...], precision=HIGHEST, preferred_element_type=F32) + b1_ref[...]))
    h = jnp.sin(fr * (jnp.dot(h, w2_ref[...], precision=HIGHEST, preferred_element_type=F32) + b2_ref[...]))
    h = jnp.dot(h, w3_ref[...], precision=HIGHEST, preferred_element_type=F32)
    tt = (lax.broadcasted_iota(jnp.int32, (HYF_TB, HY_WIDTH), 0) + i * HYF_TB).astype(F32) / (L - 1)
    window = jnp.exp(-tt * dl_ref[...])
    rowi = lax.broadcasted_iota(jnp.int32, (HYF_TB, HY_WIDTH), 0) + i * HYF_TB
    hf = h[:, :HY_WIDTH] * window
    hb = jnp.where(rowi == 0, 0.0, h[:, HY_WIDTH:] * window)
    h_ref[0] = hf
    h_ref[1] = hb

    @pl.when(i == 0)
    def _():
        l1_ref[...] = jnp.zeros_like(l1_ref)

    l1_ref[...] += jnp.sum(jnp.abs(hf) + jnp.abs(hb), axis=0, keepdims=True)


def _hyena_filter(L, w1, b1, w2, b2, freq, w3):
    band = np.zeros((1, LANES), np.float32)
    bands = np.linspace(1e-4, HY_BANDS - 1, HY_BANDS, dtype=np.float32)
    band[0, 1:1 + HY_BANDS] = bands
    band[0, 1 + HY_BANDS:1 + 2 * HY_BANDS] = bands
    max_decay = math.log(HY_DECAY_TARGET) / HY_SHORT_DECAY_PCT
    min_decay = math.log(HY_DECAY_TARGET) / HY_LONG_DECAY_PCT
    deltas = np.abs(np.linspace(min_decay, max_decay, HY_WIDTH, dtype=np.float32))[None, :]
    w1p = jnp.zeros((LANES, HY_FILTER_HIDDEN), F32).at[:HY_EMB].set(w1.astype(F32))
    const = lambda shape: pl.BlockSpec(shape, lambda i: (0,) * len(shape))
    H = HY_FILTER_HIDDEN
    return pl.pallas_call(
        functools.partial(_filter_kernel, L=L), grid=(L // HYF_TB,),
        in_specs=[const((1, LANES)), const((LANES, H)), const((1, H)), const((H, H)), const((1, H)),
                  const((1, H)), const((H, 2 * HY_WIDTH)), const((1, HY_WIDTH))],
        out_specs=[pl.BlockSpec((2, HYF_TB, HY_WIDTH), lambda i: (0, i, 0)), const((1, HY_WIDTH))],
        out_shape=[jax.ShapeDtypeStruct((2, L, HY_WIDTH), F32), jax.ShapeDtypeStruct((1, HY_WIDTH), F32)],
        compiler_params=_cparams(("arbitrary",)), name="hyena_filter")(
            jnp.asarray(band), w1p, b1.astype(F32)[None], w2.astype(F32), b2.astype(F32)[None],
            freq.astype(F32)[None], w3.astype(F32), jnp.asarray(deltas))


def _fft_factors(n):
    lg = int(round(math.log2(n)))
    n1 = 1 << (lg // 2)
    return n1, n // n1


def _dft(n):
    k = np.arange(n)
    a = -2.0 * np.pi * ((k[:, None] * k[None, :]) % n) / n
    return np.cos(a), np.sin(a)


FFT_G = 8


def _split3(f):
    f = jnp.asarray(np.asarray(f, np.float32))
    hi = f.astype(BF16)
    lo = (f - hi.astype(F32)).astype(BF16)
    return jnp.concatenate([hi, lo, hi], axis=1)


def _dot3(f3, x):
    hi = x.astype(BF16)
    lo = (x - hi.astype(F32)).astype(BF16)
    return jnp.dot(f3, jnp.concatenate([hi, hi, lo], axis=0), preferred_element_type=F32)


def _fft1_kernel(u_ref, f_ref, yr_ref, yi_ref, *, n1):
    P, _, G, C = u_ref.shape[1:]
    y = _dot3(f_ref[...], u_ref[0].reshape(P * (n1 // 2) * G, C))
    yr_ref[0] = y[:n1 * G].reshape(n1, G, C)
    yi_ref[0] = y[n1 * G:].reshape(n1, G, C)


def _fft_stage1(u, n1, n2, pair):
    B, L, C = u.shape
    P = 2 if pair else 1
    fr, fi = _dft(n1)
    G = FFT_G
    eye = np.eye(G)
    kr, ki = np.kron(fr[:, :n1 // 2], eye), np.kron(fi[:, :n1 // 2], eye)
    f3 = _split3(np.block([[kr, -ki], [ki, kr]]) if pair else np.concatenate([kr, ki], 0))
    blk = pl.BlockSpec((1, P, n1 // 2, G, C), lambda b, j: (b, 0, 0, j, 0))
    oblk = pl.BlockSpec((1, n1, G, C), lambda b, j: (b, 0, j, 0))
    shp = jax.ShapeDtypeStruct((B // P, n1, n2, C), F32)
    yr, yi = pl.pallas_call(
        functools.partial(_fft1_kernel, n1=n1), grid=(B // P, n2 // G),
        in_specs=[blk, pl.BlockSpec(f3.shape, lambda b, j: (0, 0))],
        out_specs=[oblk, oblk], out_shape=[shp, shp],
        compiler_params=_cparams(("parallel", "parallel")), name="fft_stage1")(
            u.reshape(B // P, P, n1 // 2, n2, C), f3)
    return yr.reshape(B // P, n1 * n2, C), yi.reshape(B // P, n1 * n2, C)


def _cmul(ar, ai, br, bi):
    return ar * br - ai * bi, ar * bi + ai * br


def _fft2_fwd(yr, yi, tr, ti, f2, n2, C):
    ar, ai = _cmul(yr, yi, tr, ti)
    p = _dot3(f2, jnp.concatenate([ar, ai], axis=1))
    return p[:n2, :C] - p[n2:, C:], p[:n2, C:] + p[n2:, :C]


FFT_ROWS = 512


def _fft2_filter_kernel(yr_ref, yi_ref, tr_ref, ti_ref, f_ref, l1_ref, kr_ref, ki_ref, *, n2, n):
    C = yr_ref.shape[2]
    f2 = f_ref[...]
    scale = 1.0 / (l1_ref[...] * n)
    for kb in range(FFT_ROWS // n2):
        rws = slice(kb * n2, (kb + 1) * n2)
        tr = jnp.tile(tr_ref[kb], (1, C // LANES))
        ti = jnp.tile(ti_ref[kb], (1, C // LANES))
        gr, gi = _fft2_fwd(yr_ref[0, rws, :], yi_ref[0, rws, :], tr, ti, f2, n2, C)
        hr, hi = _fft2_fwd(yr_ref[1, rws, :], yi_ref[1, rws, :], tr, ti, f2, n2, C)
        kr_ref[rws, :] = (gr + hr) * scale
        ki_ref[rws, :] = (gi - hi) * scale


def _fft2_conv_kernel(yr_ref, yi_ref, tr_ref, ti_ref, f_ref, kr_ref, ki_ref, zr_ref, zi_ref, *, n2):
    C = yr_ref.shape[2]
    f2 = f_ref[...]
    for kb in range(FFT_ROWS // n2):
        rws = slice(kb * n2, (kb + 1) * n2)
        tr = jnp.tile(tr_ref[kb], (1, C // LANES))
        ti = jnp.tile(ti_ref[kb], (1, C // LANES))
        xr, xi = _fft2_fwd(yr_ref[0, rws, :], yi_ref[0, rws, :], tr, ti, f2, n2, C)
        vr, vi = _cmul(xr, xi, kr_ref[rws, :], ki_ref[rws, :])
        q = _dot3(f2, jnp.concatenate([vr, vi], axis=1))
        wr = q[:n2, :C] + q[n2:, C:]
        wi = q[:n2, C:] - q[n2:, :C]
        zr, zi = _cmul(wr, wi, tr, -ti)
        zr_ref[0, rws, :] = zr
        zi_ref[0, rws, :] = zi


def _fft_tables(n1, n2):
    n = n1 * n2
    k1 = jnp.arange(n1, dtype=jnp.int32)[:, None]
    j2 = jnp.arange(n2, dtype=jnp.int32)[None, :]
    a = (-2.0 * math.pi / n) * (k1 * j2).astype(F32)
    tr = jnp.broadcast_to(jnp.cos(a)[:, :, None], (n1, n2, LANES))
    ti = jnp.broadcast_to(jnp.sin(a)[:, :, None], (n1, n2, LANES))
    fr, fi = _dft(n2)
    return tr, ti, _split3(np.concatenate([fr, fi], 0))


def _fft_stage2_filter(yr, yi, tables, l1, n1, n2):
    _, N, C = yr.shape
    tr, ti, f2 = tables
    kb = FFT_ROWS // n2
    blk = pl.BlockSpec((2, FFT_ROWS, C), lambda k: (0, k, 0))
    tblk = pl.BlockSpec((kb, n2, LANES), lambda k: (k, 0, 0))
    oblk = pl.BlockSpec((FFT_ROWS, C), lambda k: (k, 0))
    shp = jax.ShapeDtypeStruct((N, C), F32)
    return pl.pallas_call(
        functools.partial(_fft2_filter_kernel, n2=n2, n=N), grid=(N // FFT_ROWS,),
        in_specs=[blk, blk, tblk, tblk, pl.BlockSpec(f2.shape, lambda k: (0, 0)),
                  pl.BlockSpec((1, C), lambda k: (0, 0))],
        out_specs=[oblk, oblk], out_shape=[shp, shp],
        compiler_params=_cparams(("parallel",)), name="fft_stage2_filter")(yr, yi, tr, ti, f2, l1)


def _fft_stage2_conv(yr, yi, tables, spec_r, spec_i, n1, n2):
    B, N, C = yr.shape
    tr, ti, f2 = tables
    kb = FFT_ROWS // n2
    blk = pl.BlockSpec((1, FFT_ROWS, C), lambda b, k: (b, k, 0))
    tblk = pl.BlockSpec((kb, n2, LANES), lambda b, k: (k, 0, 0))
    sblk = pl.BlockSpec((FFT_ROWS, C), lambda b, k: (k, 0))
    shp = jax.ShapeDtypeStruct((B, N, C), F32)
    return pl.pallas_call(
        functools.partial(_fft2_conv_kernel, n2=n2), grid=(B, N // FFT_ROWS),
        in_specs=[blk, blk, tblk, tblk, pl.BlockSpec(f2.shape, lambda b, k: (0, 0)), sblk, sblk],
        out_specs=[blk, blk], out_shape=[shp, shp],
        compiler_params=_cparams(("parallel", "parallel")), name="fft_stage2_conv")(
            yr, yi, tr, ti, f2, spec_r, spec_i)


def _fft3_kernel(zr_ref, zi_ref, f_ref, x0_ref, u_ref, bias_ref, o_ref):
    n1, G, C = zr_ref.shape[1:]
    P = u_ref.shape[1]
    z = jnp.concatenate([zr_ref[0].reshape(n1 * G, C), zi_ref[0].reshape(n1 * G, C)], axis=0)
    conv = _dot3(f_ref[...], z).reshape(P, n1 // 2, G, C)
    o_ref[0] = x0_ref[0] * (conv + u_ref[0] * bias_ref[...])


def _fft_stage3(zr, zi, x0, u, bias, n1, n2, pair):
    B, L, C = u.shape
    P = 2 if pair else 1
    fr, fi = _dft(n1)
    G = FFT_G
    eye = np.eye(G)
    kr, ki = np.kron(fr[:n1 // 2], eye), np.kron(fi[:n1 // 2], eye)
    f3 = _split3(np.block([[kr, ki], [-ki, kr]]) if pair else np.concatenate([kr, ki], 1))
    zblk = pl.BlockSpec((1, n1, G, C), lambda b, j: (b, 0, j, 0))
    ublk = pl.BlockSpec((1, P, n1 // 2, G, C), lambda b, j: (b, 0, 0, j, 0))
    v5 = lambda a: a.reshape(B // P, P, n1 // 2, n2, C)
    out = pl.pallas_call(
        _fft3_kernel, grid=(B // P, n2 // G),
        in_specs=[zblk, zblk, pl.BlockSpec(f3.shape, lambda b, j: (0, 0)), ublk, ublk,
                  pl.BlockSpec((1, C), lambda b, j: (0, 0))],
        out_specs=ublk, out_shape=jax.ShapeDtypeStruct((B // P, P, n1 // 2, n2, C), F32),
        compiler_params=_cparams(("parallel", "parallel")), name="fft_stage3")(
            zr.reshape(B // P, n1, n2, C), zi.reshape(B // P, n1, n2, C), f3, v5(x0), v5(u), bias)
    return out.reshape(B * L, C)


def _hyena(proj, conv_w, conv_b, filt, hy_bias, row0, B, L):
    n1, n2 = _fft_factors(2 * L)
    pair = B % 2 == 0
    x0, u = _short_conv(proj, conv_w, conv_b, row0, B, L)
    x0 = x0.reshape(B, L, HY_WIDTH)
    u = u.reshape(B, L, HY_WIDTH)
    tables = _fft_tables(n1, n2)
    hfb, l1 = _hyena_filter(L, *filt)
    fr, fi = _fft_stage1(hfb, n1, n2, False)
    sr, si = _fft_stage2_filter(fr, fi, tables, l1, n1, n2)
    yr, yi = _fft_stage1(u, n1, n2, pair)
    zr, zi = _fft_stage2_conv(yr, yi, tables, sr, si, n1, n2)
    return _fft_stage3(zr, zi, x0, u, hy_bias, n1, n2, pair)


def _group_specs(parts, tm):
    starts, specs = [], []
    s = 0
    for a in parts:
        n = a.shape[0] // tm
        starts.append(s)
        specs.append(pl.BlockSpec((tm, a.shape[1]),
                                  (lambda s, n: lambda i, *_: (jnp.clip(i - s, 0, n - 1), 0))(s, n)))
        s += n
    return starts, specs


def _group_pick(refs, starts):
    i = pl.program_id(0)
    v = refs[0][...]
    for r, s in zip(refs[1:], starts[1:]):
        v = jnp.where(i >= s, r[...], v)
    return v


def _merge_kernel(*refs, starts):
    ng = len(starts)
    x_ref, g_ref, wg_ref = refs[:3]
    y_refs = [refs[3 + j * ng:3 + (j + 1) * ng] for j in range(N_BRANCHES)]
    wa_ref, wb_ref, wc_ref, wo_ref, o_ref = refs[3 + N_BRANCHES * ng:]
    x = x_ref[...]
    xn = _rms(x, g_ref[...]).astype(BF16)
    merged = None
    for j, w_ref in enumerate((wa_ref, wb_ref, wc_ref)):
        gate = jnp.dot(xn, wg_ref[:, j * D_MODEL:(j + 1) * D_MODEL], preferred_element_type=F32)
        y = _group_pick(y_refs[j], starts)
        br = jnp.dot(y.astype(BF16), w_ref[...], preferred_element_type=F32)
        term = jax.nn.sigmoid(gate) * br
        merged = term if merged is None else merged + term
    o_ref[...] = x + jnp.dot(merged.astype(BF16), wo_ref[...], preferred_element_type=F32)


def _merge(x, g, wg, ya, yb, yc, wa, wb, wc, wo):
    T = x.shape[0]
    tm = min(512, T)
    xspec = pl.BlockSpec((tm, D_MODEL), lambda i: (i, 0))
    const = lambda shape: pl.BlockSpec(shape, lambda i: (0,) * len(shape))
    starts, yspecs = _group_specs(ya, tm)
    return pl.pallas_call(
        functools.partial(_merge_kernel, starts=starts), grid=(T // tm,),
        in_specs=[xspec, const((1, D_MODEL)), const((D_MODEL, 3 * D_MODEL))] + yspecs * N_BRANCHES
        + [const((BRANCH_W, D_MODEL)), const((BRANCH_W, D_MODEL)), const((BRANCH_W, D_MODEL)),
           const((D_MODEL, D_MODEL))],
        out_specs=xspec, out_shape=jax.ShapeDtypeStruct((T, D_MODEL), F32),
        compiler_params=_cparams(("parallel",)), name="merge_out_projection")(
            x, g, wg, *ya, *yb, *yc, wa, wb, wc, wo)


def _route(xn, w, b):
    lg = jnp.dot(xn, w, precision=HIGHEST, preferred_element_type=F32) + b
    lane = lax.broadcasted_iota(jnp.int32, lg.shape, 1).astype(F32)
    neg = -jnp.inf
    is_g = lane < N_GROUPS
    gl = jnp.where(is_g, lg, neg)
    gmax = jnp.max(gl, axis=1, keepdims=True)
    gsel = jnp.min(jnp.where(gl == gmax, lane, float(LANES)), axis=1, keepdims=True)
    gprob = 1.0 / jnp.sum(jnp.where(is_g, jnp.exp(lg - gmax), 0.0), axis=1, keepdims=True)
    lo = N_GROUPS + gsel * EXPERTS_PER_GROUP
    el = jnp.where((lane >= lo) & (lane < lo + EXPERTS_PER_GROUP), lg, neg)
    v1 = jnp.max(el, axis=1, keepdims=True)
    i1 = jnp.min(jnp.where(el == v1, lane, float(LANES)), axis=1, keepdims=True)
    el2 = jnp.where(lane == i1, neg, el)
    v2 = jnp.max(el2, axis=1, keepdims=True)
    i2 = jnp.min(jnp.where(el2 == v2, lane, float(LANES)), axis=1, keepdims=True)
    e = jnp.exp(v2 - v1)
    w1 = gprob / (1.0 + e)
    w2 = w1 * e
    return jnp.where(lane == 0, i1 - N_GROUPS,
                     jnp.where(lane == 1, i2 - N_GROUPS,
                               jnp.where(lane == 2, w1, jnp.where(lane == 3, w2, 0.0))))


def _router_kernel(x_ref, g_ref, w_ref, b_ref, xn_ref, r_ref):
    xn = _rms(x_ref[...], g_ref[...])
    xn_ref[...] = xn
    r_ref[...] = _route(xn, w_ref[...], b_ref[...])


def _router(x, g, w, b):
    T = x.shape[0]
    tm = min(512, T)
    xspec = pl.BlockSpec((tm, D_MODEL), lambda i: (i, 0))
    const = lambda shape: pl.BlockSpec(shape, lambda i: (0,) * len(shape))
    return pl.pallas_call(
        _router_kernel, grid=(T // tm,),
        in_specs=[xspec, const((1, D_MODEL)), const((D_MODEL, LANES)), const((1, LANES))],
        out_specs=[xspec, pl.BlockSpec((tm, LANES), lambda i: (i, 0))],
        out_shape=[jax.ShapeDtypeStruct((T, D_MODEL), F32), jax.ShapeDtypeStruct((T, LANES), F32)],
        compiler_params=_cparams(("parallel",)), name="router")(x, g, w, b)


MOE_ROWS = 512


def _moe_kernel(be_ref, nu_ref, idx_hbm, xn_hbm, wg0, wu0, wd0, wg1, wu1, wd1, out_hbm,
                idx, xbuf, obuf, isem, gsem, ssem):
    R = MOE_ROWS
    j = pl.program_id(0)
    nu = nu_ref[0]
    a = 2 * j
    b = a + 1

    def idx_copy(blk, slot):
        return pltpu.make_async_copy(idx_hbm.at[blk], idx.at[slot], isem.at[slot])

    def issue_gathers(slot):
        for r in range(R):
            pltpu.make_async_copy(xn_hbm.at[pl.ds(idx[slot, r], 1)], xbuf.at[slot, pl.ds(r, 1)],
                                  gsem.at[slot]).start()

    def wait_gathers(slot):
        pltpu.make_async_copy(xn_hbm.at[pl.ds(0, R)], xbuf.at[slot], gsem.at[slot]).wait()

    def issue_scatters(slot):
        for r in range(R):
            pltpu.make_async_copy(obuf.at[slot, pl.ds(r, 1)], out_hbm.at[pl.ds(idx[slot, R + r], 1)],
                                  ssem.at[slot]).start()

    def wait_scatters(slot):
        pltpu.make_async_copy(obuf.at[slot], out_hbm.at[pl.ds(0, R)], ssem.at[slot]).wait()

    def compute(slot, wg, wu, wd):
        xb = xbuf[slot].astype(BF16)
        h = jax.nn.silu(jnp.dot(xb, wg[0], preferred_element_type=F32)) \
            * jnp.dot(xb, wu[0], preferred_element_type=F32)
        obuf[slot] = jnp.dot(h.astype(BF16), wd[0], preferred_element_type=F32)

    @pl.when(j == 0)
    def _():
        obuf[0] = jnp.zeros((R, D_MODEL), F32)
        n_rows = out_hbm.shape[0]
        for k in (2, 1):
            z = pltpu.make_async_copy(obuf.at[0], out_hbm.at[pl.ds(n_rows - k * R, R)], ssem.at[0])
            z.start()
            z.wait()
        c = idx_copy(0, 0)
        c.start()
        c.wait()
        issue_gathers(0)

        @pl.when(1 < nu)
        def _():
            idx_copy(1, 1).start()

    @pl.when(a < nu)
    def _():
        wait_gathers(0)

        @pl.when(b < nu)
        def _():
            idx_copy(b, 1).wait()
            issue_gathers(1)

        @pl.when(j >= 1)
        def _():
            wait_scatters(0)

        compute(0, wg0, wu0, wd0)
        issue_scatters(0)

        @pl.when(a + 2 < nu)
        def _():
            idx_copy(a + 2, 0).start()

    @pl.when(b < nu)
    def _():
        wait_gathers(1)

        @pl.when(b + 1 < nu)
        def _():
            idx_copy(b + 1, 0).wait()
            issue_gathers(0)

        @pl.when(j >= 1)
        def _():
            wait_scatters(1)

        compute(1, wg1, wu1, wd1)
        issue_scatters(1)

        @pl.when(b + 2 < nu)
        def _():
            idx_copy(b + 2, 1).start()

    @pl.when((a < nu) & (a + 2 >= nu))
    def _():
        wait_scatters(0)

        @pl.when((b < nu) | (j >= 1))
        def _():
            wait_scatters(1)


def _moe_experts(xn, idx, block_e, n_used, wg, wu, wd):
    T = xn.shape[0]
    R = MOE_ROWS
    n_blocks = idx.shape[0]
    assert n_blocks % 2 == 0
    wspec = lambda o, shp: pl.BlockSpec(shp, lambda j, be, nu: (be[2 * j + o], 0, 0))
    w_in, w_out = (1, D_MODEL, EXPERT_HIDDEN), (1, EXPERT_HIDDEN, D_MODEL)
    grid_spec = pltpu.PrefetchScalarGridSpec(
        num_scalar_prefetch=2, grid=(n_blocks // 2,),
        in_specs=[pl.BlockSpec(memory_space=pl.ANY), pl.BlockSpec(memory_space=pl.ANY),
                  wspec(0, w_in), wspec(0, w_in), wspec(0, w_out),
                  wspec(1, w_in), wspec(1, w_in), wspec(1, w_out)],
        out_specs=pl.BlockSpec(memory_space=pl.ANY),
        scratch_shapes=[pltpu.SMEM((2, 2 * R), jnp.int32), pltpu.VMEM((2, R, D_MODEL), F32),
                        pltpu.VMEM((2, R, D_MODEL), F32), pltpu.SemaphoreType.DMA((2,)),
                        pltpu.SemaphoreType.DMA((2,)), pltpu.SemaphoreType.DMA((2,))])
    return pl.pallas_call(
        _moe_kernel, grid_spec=grid_spec, out_shape=jax.ShapeDtypeStruct((TOP_K * T + 2 * R, D_MODEL), F32),
        compiler_params=_cparams(("arbitrary",)), name="moe_experts")(
            block_e, n_used, idx, xn, wg, wu, wd, wg, wu, wd)


def _dispatch(route, T):
    R = MOE_ROWS
    M = T * TOP_K
    expert_id = route[:, :TOP_K].astype(jnp.int32).reshape(-1)
    order = jnp.argsort(expert_id).astype(jnp.int32)
    counts =jnp.bincount(expert_id, length=N_EXPERTS).astype(jnp.int32)
    starts = jnp.cumsum(counts) - counts
    padded = (counts + R - 1) // R * R
    p_ends = jnp.cumsum(padded)
    p_starts = p_ends - padded
    n_blocks = -(-M // R) + N_EXPERTS
    first_row = jnp.arange(n_blocks, dtype=jnp.int32) * R
    block_e = jnp.minimum(jnp.sum(p_ends[None, :] <= first_row[:, None], axis=1), N_EXPERTS - 1).astype(jnp.int32)
    row = jnp.arange(n_blocks * R, dtype=jnp.int32).reshape(n_blocks, R)
    k = row - p_starts[block_e][:, None]
    valid = (k < counts[block_e][:, None]) & (row < p_ends[N_EXPERTS - 1])
    a = order[jnp.clip(starts[block_e][:, None] + k, 0, M - 1)]
    token = a >> 1
    src = jnp.where(valid, token, 0)
    scrap = M + row % (2 * R)
    dst = jnp.where(valid, (a & 1) * T + token, scrap)
    n_used = (p_ends[N_EXPERTS - 1:] // R).astype(jnp.int32)
    return jnp.concatenate([src, dst], axis=1), block_e, n_used


def _final_kernel(x_ref, e0_ref, e1_ref, r_ref, g_ref, o_ref):
    r = r_ref[...]
    x = x_ref[...] + e0_ref[...] * r[:, 2:3] + e1_ref[...] * r[:, 3:4]
    o_ref[...] = _rms(x, g_ref[...])


def _final(x, e2, route, g, row0, rows):
    T = x.shape[0]
    tm = min(512, rows)
    nT = T // tm
    b0 = row0 // tm
    xspec = pl.BlockSpec((tm, D_MODEL), lambda i: (b0 + i, 0))
    return pl.pallas_call(
        _final_kernel, grid=(rows // tm,),
        in_specs=[xspec, xspec, pl.BlockSpec((tm, D_MODEL), lambda i: (nT + b0 + i, 0)),
                  pl.BlockSpec((tm, LANES), lambda i: (b0 + i, 0)), pl.BlockSpec((1, D_MODEL), lambda i: (0, 0))],
        out_specs=pl.BlockSpec((tm, D_MODEL), lambda i: (i, 0)),
        out_shape=jax.ShapeDtypeStruct((rows, D_MODEL), F32),
        compiler_params=_cparams(("parallel",)), name="combine_final_norm")(x, e2, e2, route, g)


def _pack_layer(l, p):
    w = p['w_in'][l]
    o_gate = 3 * HG_WIDTH + 2 * HG_WIDTH + 4 * ML_WIDTH
    o_hy = o_gate + 4 * ML_HEADS
    o_g = o_hy + 3 * HY_WIDTH
    gates = jnp.pad(w[:, o_gate:o_hy], ((0, 0), (0, PROJ_W - PROJ_MAIN - 4 * ML_HEADS)))
    w_proj = jnp.concatenate([w[:, :o_gate], w[:, o_hy:o_g], gates], axis=1).astype(BF16)
    w_gate = w[:, o_g:].astype(BF16)
    gate_bias = jnp.pad(jnp.concatenate([p['ml_i_bias'][l].reshape(-1), p['ml_f_bias'][l].reshape(-1)]),
                        (0, LANES - 4 * ML_HEADS)).astype(F32)[None]
    w_router = jnp.pad(jnp.concatenate([p['router_group_w'][l], p['router_expert_w'][l]], axis=1),
                       ((0, 0), (0, LANES - N_GROUPS - N_EXPERTS))).astype(F32)
    b_router = jnp.pad(jnp.concatenate([p['router_group_b'][l], p['router_expert_b'][l]]),
                       (0, LANES - N_GROUPS - N_EXPERTS)).astype(F32)[None]
    return dict(
        norm_mix_g=p['norm_mix_g'][l].astype(F32)[None], w_proj=w_proj, w_gate=w_gate,
        hg_norm_g=p['hg_norm_g'][l].astype(F32)[None], gate_bias=gate_bias,
        ml_norm_g=p['ml_norm_g'][l].astype(F32)[None],
        conv_w=jnp.pad(p['hy_conv_w'][l].astype(F32), ((0, 5), (0, 0))), conv_b=p['hy_conv_b'][l].astype(F32)[None],
        filt=(p['filt_w1'][l], p['filt_b1'][l], p['filt_w2'][l], p['filt_b2'][l], p['filt_freq'][l],
              p['filt_w3'][l]),
        hy_bias=p['hy_bias'][l].astype(F32)[None],
        wa=p['w_branch_a'][l].astype(BF16), wb=p['w_branch_b'][l].astype(BF16),
        wc=p['w_branch_c'][l].astype(BF16), wo=p['w_out'][l].astype(BF16),
        norm_ffn_g=p['norm_ffn_g'][l].astype(F32)[None], w_router=w_router, b_router=b_router,
        wg=p['exp_w_gate'][l].astype(BF16), wu=p['exp_w_up'][l].astype(BF16),
        wd=p['exp_w_down'][l].astype(BF16))


def _trunk(x, groups, p):
    T = x.shape[0]
    lbs = jnp.cumsum(jax.nn.softmax(p['hg_lb'].astype(F32), axis=0), axis=0)
    lbs = lbs - lbs[0:1]
    moe = None
    for l in range(DEPTH):
        lp = _pack_layer(l, p)
        lb = lbs[l][None]
        lbrow = jnp.concatenate([jnp.log(lb), jnp.log1p(-lb), 1.0 - lb, jnp.zeros((5, HG_WIDTH), F32)], axis=0)
        if moe is not None:
            x = _combine(x, *moe)
        proj = _in_projection(x, lp['norm_mix_g'], lp['w_proj'])
        ya, yb, yc = [], [], []
        for row0, B, L in groups:
            ya.append(_hgrn2(proj, lbrow, lp['hg_norm_g'], row0, B, L))
            yb.append(_mlstm(proj, lp['gate_bias'], lp['ml_norm_g'], row0, B, L))
            yc.append(_hyena(proj, lp['conv_w'], lp['conv_b'], lp['filt'], lp['hy_bias'], row0, B, L))
        x = _merge(x, lp['norm_mix_g'], lp['w_gate'], ya, yb, yc, lp['wa'], lp['wb'], lp['wc'], lp['wo'])
        xn, route = _router(x, lp['norm_ffn_g'], lp['w_router'], lp['b_router'])
        rows, block_e, n_used = _dispatch(route, T)
        e2 = _moe_experts(xn, rows, block_e, n_used, lp['wg'], lp['wu'], lp['wd'])
        moe = (e2, route)
    g = p['final_norm_g'].astype(F32)[None]
    return [_final(x, moe[0], moe[1], g, row0, B * L) for row0, B, L in groups]


def kernel(x_prompt, x_sample, norm_mix_g, w_in, hg_lb, hg_norm_g, ml_i_bias, ml_f_bias, ml_norm_g, hy_conv_w, hy_conv_b, filt_w1, filt_b1, filt_w2, filt_b2, filt_freq, filt_w3, hy_bias, w_branch_a, w_branch_b, w_branch_c, w_out, norm_ffn_g, router_group_w, router_group_b, router_expert_w, router_expert_b, exp_w_gate, exp_w_up, exp_w_down, final_norm_g):
    p = dict(norm_mix_g=norm_mix_g, w_in=w_in, hg_lb=hg_lb, hg_norm_g=hg_norm_g, ml_i_bias=ml_i_bias,
             ml_f_bias=ml_f_bias, ml_norm_g=ml_norm_g, hy_conv_w=hy_conv_w, hy_conv_b=hy_conv_b,
             filt_w1=filt_w1, filt_b1=filt_b1, filt_w2=filt_w2, filt_b2=filt_b2, filt_freq=filt_freq,
             filt_w3=filt_w3, hy_bias=hy_bias, w_branch_a=w_branch_a, w_branch_b=w_branch_b,
             w_branch_c=w_branch_c, w_out=w_out, norm_ffn_g=norm_ffn_g, router_group_w=router_group_w,
             router_group_b=router_group_b, router_expert_w=router_expert_w,
             router_expert_b=router_expert_b, exp_w_gate=exp_w_gate, exp_w_up=exp_w_up,
             exp_w_down=exp_w_down, final_norm_g=final_norm_g)
    Bp, Lp, _ = x_prompt.shape
    Bs, Ls, _ = x_sample.shape
    Tp, Ts = Bp * Lp, Bs * Ls
    x = jnp.concatenate([x_prompt.reshape(Tp, D_MODEL), x_sample.reshape(Ts, D_MODEL)], axis=0).astype(F32)
    yp, ys = _trunk(x, ((0, Bp, Lp), (Tp, Bs, Ls)), p)
    return (yp.reshape(Bp, Lp, D_MODEL), ys.reshape(Bs, Ls, D_MODEL))
```

```python
import functools
import math

import numpy as np
import jax
import jax.numpy as jnp
from jax import lax
from jax.experimental import pallas as pl
from jax.experimental.pallas import tpu as pltpu

F32 = jnp.float32
BF16 = jnp.bfloat16
HIGHEST = lax.Precision.HIGHEST

D_MODEL = 1024
EPS = 1e-6
DEPTH = 2
HG_HEADS, HG_KDIM, HG_WIDTH, HG_CHUNK = 8, 64, 512, 32
ML_HEADS, ML_HDIM, ML_WIDTH, ML_CHUNK = 4, 128, 512, 128
HY_WIDTH, HY_BANDS, HY_FILTER_HIDDEN = 512, 16, 64
HY_EMB = 1 + 2 * HY_BANDS
HY_SHORT_DECAY_PCT, HY_LONG_DECAY_PCT, HY_DECAY_TARGET = 0.3, 1.5, 1e-2
N_GROUPS, EXPERTS_PER_GROUP, TOP_K = 4, 8, 2
N_EXPERTS = N_GROUPS * EXPERTS_PER_GROUP
N_BRANCHES = 3
EXPERT_HIDDEN = D_MODEL // 2
MOE_BLOCK = 128

LANES = 128
BRANCH_W = 512
SLAB_HQ, SLAB_HFF, SLAB_HFB, SLAB_HI, SLAB_HG = 0, 1, 2, 3, 4
SLAB_MQ, SLAB_MK, SLAB_MV, SLAB_MO = 5, 6, 7, 8
SLAB_HY = 9
PROJ_MAIN = 12 * BRANCH_W
PROJ_W = PROJ_MAIN + 2 * LANES
PROJ_TN = 1280
GATE_BLOCK = PROJ_MAIN // LANES
VMEM_LIMIT = 48 * 1024 * 1024
VMEM_LIMIT_WIDE = 56 * 1024 * 1024


def _cparams(sem, vmem=VMEM_LIMIT):
    return pltpu.CompilerParams(dimension_semantics=sem, vmem_limit_bytes=vmem)


def _rms(x, g):
    return x * lax.rsqrt(jnp.mean(x * x, axis=-1, keepdims=True) + EPS) * g


def _log_sigmoid(z):
    return jnp.minimum(z, 0.0) - jnp.log1p(jnp.exp(-jnp.abs(z)))


def _bf16_terms(x):
    t1 = x.astype(BF16)
    r = x - t1.astype(F32)
    t2 = r.astype(BF16)
    return t1, t2, (r - t2.astype(F32)).astype(BF16)


PROJ_RING = 3


def _inproj_kernel(x_ref, g_ref, w_hbm, o_ref, xn_ref, wbuf, wsem):
    nj = pl.num_programs(1)
    steps = pl.num_programs(0) * nj
    s = pl.program_id(0) * nj + pl.program_id(1)

    def w_copy(step):
        col = pl.multiple_of((step % nj) * PROJ_TN, LANES)
        slot = step % PROJ_RING
        return pltpu.make_async_copy(w_hbm.at[:, pl.ds(col, PROJ_TN)], wbuf.at[slot], wsem.at[slot])

    @pl.when(s == 0)
    def _():
        w_copy(0).start()

        @pl.when(steps > 1)
        def _():
            w_copy(1).start()

    @pl.when(s + 2 < steps)
    def _():
        w_copy(s + 2).start()

    @pl.when(pl.program_id(1) == 0)
    def _():
        xn_ref[...] = _rms(x_ref[...], g_ref[...]).astype(BF16)

    w_copy(s).wait()
    o_ref[...] = jnp.dot(xn_ref[...], wbuf[s % PROJ_RING], preferred_element_type=F32)


def _combine_kernel(x_ref, e0_ref, e1_ref, r_ref, o_ref):
    r = r_ref[...]
    o_ref[...] = x_ref[...] + e0_ref[...] * r[:, 2:3] + e1_ref[...] * r[:, 3:4]


def _combine(x, e2, route):
    T = x.shape[0]
    tm = min(1024, T)
    nT = T // tm
    xspec = pl.BlockSpec((tm, D_MODEL), lambda i: (i, 0))
    return pl.pallas_call(
        _combine_kernel, grid=(nT,),
        in_specs=[xspec, xspec, pl.BlockSpec((tm, D_MODEL), lambda i: (nT + i, 0)),
                  pl.BlockSpec((tm, LANES), lambda i: (i, 0))],
        out_specs=xspec, out_shape=jax.ShapeDtypeStruct((T, D_MODEL), F32),
        compiler_params=_cparams(("parallel",)), name="moe_combine")(x, e2, e2, route)


def _in_projection(x, g, w):
    T = x.shape[0]
    tm = min(2048, T)
    tn = PROJ_TN
    return pl.pallas_call(
        _inproj_kernel, grid=(T // tm, PROJ_W // tn),
        in_specs=[pl.BlockSpec((tm, D_MODEL), lambda i, j: (i, 0)), pl.BlockSpec((1, D_MODEL), lambda i, j: (0, 0)),
                  pl.BlockSpec(memory_space=pl.ANY)],
        out_specs=pl.BlockSpec((tm, tn), lambda i, j: (i, j)),
        out_shape=jax.ShapeDtypeStruct((T, PROJ_W), F32),
        scratch_shapes=[pltpu.VMEM((tm, D_MODEL), BF16), pltpu.VMEM((PROJ_RING, D_MODEL, tn), BF16),
                        pltpu.SemaphoreType.DMA((PROJ_RING,))],
        compiler_params=_cparams(("arbitrary", "arbitrary"), VMEM_LIMIT_WIDE), name="in_projection")(x, g, w)


HG_TB = 256
HG_PAIRS = HG_WIDTH // LANES


HG_UNROLL = 8
HG_SUB = 8
HG_NSUB = HG_CHUNK // HG_SUB
HG_XROWS = (HG_NSUB - 1) * HG_HEADS * HG_SUB
HG_XCOLS = HG_SUB * HG_NSUB * (HG_NSUB - 1) // 2


def _hgrn2_tiles(reverse):
    if reverse:
        return [(i, (i + 1) * HG_SUB, HG_CHUNK - (i + 1) * HG_SUB) for i in range(HG_NSUB - 1)]
    return [(i, 0, i * HG_SUB) for i in range(1, HG_NSUB)]


def _hgrn2_kernel(*refs, reverse, finalize):
    if finalize:
        (q_ref, z_ref, v_ref, lb_ref, tri_ref, bd_ref, xm_ref, of_ref, gate_ref, ng_ref,
         o_ref, st_ref, acc_ref) = refs
    else:
        (q_ref, z_ref, v_ref, lb_ref, tri_ref, bd_ref, xm_ref, o_ref, st_ref) = refs
        acc_ref = o_ref
    C, c, nb, W = HG_CHUNK, HG_SUB, HG_NSUB, HG_WIDTH
    n_chunks = HG_TB // C

    @pl.when(pl.program_id(1) == 0)
    def _():
        st_ref[...] = jnp.zeros_like(st_ref)

    log_lb = lb_ref[0:1, :]
    log_1mlb = lb_ref[1:2, :]
    one_m_lb = lb_ref[2:3, :]
    tri3 = tri_ref[...]
    bd = bd_ref[...]
    bd_bf = bd.astype(BF16)
    xmask = xm_ref[...]
    sub_row = lax.broadcasted_iota(jnp.int32, (nb, c, W), 1)
    lane_head = lax.broadcasted_iota(jnp.int32, (c, W), 1) // HG_KDIM
    tiles = _hgrn2_tiles(reverse)

    def chunk(i):
        ci = (n_chunks - 1 - i) if reverse else i
        off = pl.multiple_of(ci * C, C)
        q = q_ref[pl.ds(off, C), :]
        z = z_ref[pl.ds(off, C), :]
        v = v_ref[pl.ds(off, C), :]
        ls = jnp.minimum(z, 0.0) - jnp.log(1.0 + jnp.exp(-jnp.abs(z)))
        hi = log_1mlb + ls
        mx = jnp.maximum(log_lb, hi)
        lf = mx + jnp.log(1.0 + jnp.exp(-jnp.abs(log_lb - hi)))
        kk = one_m_lb * jnp.exp(ls - z)
        b = jnp.dot(tri3, jnp.concatenate(_bf16_terms(lf), axis=0), preferred_element_type=F32)
        btot = b[0:1, :] if reverse else b[C - 1:C, :]
        qb = (q * jnp.exp(b)).astype(BF16)
        kb = (kk * jnp.exp(btot - b)).astype(BF16)
        dec = jnp.exp(btot)
        vb = v.astype(BF16)

        q3, k3, v3 = (a.reshape(nb, c, W) for a in (q, kk, v))
        one = 1 if not reverse else c - 1
        f3 = (1.0 - kk).reshape(nb, c, W)
        kd = k3
        a_rows = [(q * kk).astype(BF16)]
        for d in range(1, c):
            ok = (sub_row < c - d) if reverse else (sub_row >= d)
            kd = pltpu.roll(kd, one, axis=1) * f3
            a_rows.append(jnp.where(ok, q3 * kd, 0.0).reshape(C, W).astype(BF16))
        a_all = jnp.concatenate(a_rows, axis=0)
        sums = jnp.concatenate(
            [jnp.dot(a_all[:, p * LANES:(p + 1) * LANES], bd_bf, preferred_element_type=F32)
             for p in range(HG_PAIRS)], axis=1)
        acc3 = sums[0:C].reshape(nb, c, W) * v3
        v_sh = v3
        for d in range(1, c):
            v_sh = pltpu.roll(v_sh, one, axis=1)
            acc3 = acc3 + sums[d * C:(d + 1) * C].reshape(nb, c, W) * v_sh
        acc = acc3.reshape(C, W)

        qx, kx, vx = [], [], []
        for (ti, s0, sn) in tiles:
            edge = s0 if reverse else s0 + sn - 1
            r = b[edge:edge + 1, :]
            rows = slice(ti * c, (ti + 1) * c)
            qh = q[rows] * jnp.exp(b[rows] - r)
            qx += [jnp.where(lane_head == h, qh, 0.0) for h in range(HG_HEADS)]
            kx.append(kk[s0:s0 + sn] * jnp.exp(r - b[s0:s0 + sn]))
            vx.append(v[s0:s0 + sn])
        qx = jnp.concatenate(qx, axis=0).astype(BF16)
        kx = jnp.concatenate(kx, axis=0).astype(BF16)
        vx = jnp.concatenate(vx, axis=0).astype(BF16)
        sc = lax.dot_general(qx, kx, (((1,), (1,)), ((), ())), preferred_element_type=F32) * xmask
        px = jnp.dot(sc.astype(BF16), vx, preferred_element_type=F32)
        offd = {}
        for n, (ti, s0, sn) in enumerate(tiles):
            base = n * HG_HEADS * c
            t_acc = jnp.where(lane_head == 0, px[base:base + c], 0.0)
            for h in range(1, HG_HEADS):
                t_acc = t_acc + jnp.where(lane_head == h, px[base + h * c:base + (h + 1) * c], 0.0)
            offd[ti] = t_acc
        acc = acc + jnp.concatenate([offd.get(ti, jnp.zeros((c, W), F32)) for ti in range(nb)], axis=0)

        parts = []
        for p in range(HG_PAIRS):
            sl = slice(p * LANES, (p + 1) * LANES)
            s_t = st_ref[p]
            parts.append(lax.dot_general(qb[:, sl], s_t.astype(BF16), (((1,), (1,)), ((), ())),
                                         preferred_element_type=F32))
            upd = lax.dot_general(vb[:, sl], kb[:, sl], (((0,), (0,)), ((), ())),
                                  preferred_element_type=F32)
            st_ref[p] = s_t * dec[:, sl] + upd * bd
        acc = acc + jnp.concatenate(parts, axis=1)

        acc_ref[pl.ds(off, C), :] = acc

    def chunk_group(j, carry):
        for u in range(HG_UNROLL):
            chunk(HG_UNROLL * j + u)
        return carry

    lax.fori_loop(0, n_chunks // HG_UNROLL, chunk_group, 0)

    if finalize:
        o = acc_ref[...] + of_ref[...]
        o2 = (o * o).astype(BF16)
        ms = jnp.concatenate(
            [jnp.dot(o2[:, p * LANES:(p + 1) * LANES], bd_bf, preferred_element_type=F32)
             for p in range(HG_PAIRS)], axis=1) * (1.0 / HG_KDIM)
        g = gate_ref[...]
        y = o * lax.rsqrt(ms + EPS) * ng_ref[...] * (g * jax.nn.sigmoid(g))
        o_ref[...] = y.astype(o_ref.dtype)


def _hgrn2(proj, lbrow, norm_g, row0, B, L):
    C = HG_CHUNK
    nT = L // HG_TB
    blk0 = row0 // HG_TB
    tri_f = jnp.asarray(np.tile(np.tril(np.ones((C, C), np.float32)), (1, 3))).astype(BF16)
    tri_b = jnp.asarray(np.tile(np.triu(np.ones((C, C), np.float32)), (1, 3))).astype(BF16)
    head = np.arange(LANES) // HG_KDIM
    bd = jnp.asarray((head[:, None] == head[None, :]).astype(np.float32))

    def tile_mask(reverse):
        m = np.zeros((HG_XROWS, HG_XCOLS), np.float32)
        col = 0
        for n, (_, _, sn) in enumerate(_hgrn2_tiles(reverse)):
            m[n * HG_HEADS * HG_SUB:(n + 1) * HG_HEADS * HG_SUB, col:col + sn] = 1.0
            col += sn
        return jnp.asarray(m)

    def in_spec(slab, reverse):
        if reverse:
            return pl.BlockSpec((HG_TB, BRANCH_W), lambda b, t: (blk0 + b * nT + nT - 1 - t, slab))
        return pl.BlockSpec((HG_TB, BRANCH_W), lambda b, t: (blk0 + b * nT + t, slab))

    def out_spec(reverse):
        if reverse:
            return pl.BlockSpec((HG_TB, BRANCH_W), lambda b, t: (b * nT + nT - 1 - t, 0))
        return pl.BlockSpec((HG_TB, BRANCH_W), lambda b, t: (b * nT + t, 0))

    const = lambda shape: pl.BlockSpec(shape, lambda b, t: (0,) * len(shape))
    scratch = [pltpu.VMEM((HG_PAIRS, LANES, LANES), F32)]
    consts = [const((8, BRANCH_W)), const((C, 3 * C)), const((LANES, LANES)), const((HG_XROWS, HG_XCOLS))]
    cp = _cparams(("parallel", "arbitrary"))
    o_f = pl.pallas_call(
        functools.partial(_hgrn2_kernel, reverse=False, finalize=False), grid=(B, nT),
        in_specs=[in_spec(SLAB_HQ, False), in_spec(SLAB_HFF, False), in_spec(SLAB_HI, False)] + consts,
        out_specs=out_spec(False), out_shape=jax.ShapeDtypeStruct((B * L, BRANCH_W), F32),
        scratch_shapes=scratch, compiler_params=cp, name="hgrn2_fwd")(
            proj, proj, proj, lbrow, tri_f, bd, tile_mask(False))
    return pl.pallas_call(
        functools.partial(_hgrn2_kernel, reverse=True, finalize=True), grid=(B, nT),
        in_specs=[in_spec(SLAB_HQ, True), in_spec(SLAB_HFB, True), in_spec(SLAB_HI, True)] + consts
        + [out_spec(True), in_spec(SLAB_HG, True), const((1, BRANCH_W))],
        out_specs=out_spec(True), out_shape=jax.ShapeDtypeStruct((B * L, BRANCH_W), BF16),
        scratch_shapes=scratch + [pltpu.VMEM((HG_TB, BRANCH_W), F32)], compiler_params=cp, name="hgrn2_bwd")(
            proj, proj, proj, lbrow, tri_b, bd, tile_mask(True), o_f, proj, norm_g)


def _mlstm_kernel(*refs, reverse, finalize):
    if finalize:
        (q_ref, k_ref, v_ref, gt_ref, gb_ref, tri_ref, hf_ref, og_ref, ng_ref,
         o_ref, c_ref, n_ref, m_ref) = refs
    else:
        (q_ref, k_ref, v_ref, gt_ref, gb_ref, tri_ref, o_ref, c_ref, n_ref, m_ref) = refs
    C = ML_CHUNK

    @pl.when(pl.program_id(1) == 0)
    def _():
        c_ref[...] = jnp.zeros_like(c_ref)
        n_ref[...] = jnp.zeros_like(n_ref)
        m_ref[...] = jnp.zeros_like(m_ref)

    tri = tri_ref[...]
    lane = lax.broadcasted_iota(jnp.int32, (C, LANES), 1)
    ti = lax.broadcasted_iota(jnp.int32, (C, C), 0)
    si = lax.broadcasted_iota(jnp.int32, (C, C), 1)
    causal = (si >= ti) if reverse else (si <= ti)
    dsel = ML_HEADS if reverse else 0
    edge = 0 if reverse else C - 1
    scale = 1.0 / math.sqrt(ML_HDIM)
    order = range(ML_TB // C - 1, -1, -1) if reverse else range(ML_TB // C)
    for ci in order:
        _mlstm_chunk(refs, slice(ci * C, (ci + 1) * C), tri, lane, causal, dsel, edge, scale, finalize)


def _mlstm_chunk(refs, rws, tri, lane, causal, dsel, edge, scale, finalize):
    if finalize:
        (q_ref, k_ref, v_ref, gt_ref, gb_ref, tri_ref, hf_ref, og_ref, ng_ref,
         o_ref, c_ref, n_ref, m_ref) = refs
    else:
        (q_ref, k_ref, v_ref, gt_ref, gb_ref, tri_ref, o_ref, c_ref, n_ref, m_ref) = refs
    gates = gt_ref[rws, :] + gb_ref[...]
    g_col = jnp.where(lane >= 2 * ML_HEADS, _log_sigmoid(gates), gates)
    g_row = g_col.T
    b_col = jnp.dot(tri, g_col, precision=HIGHEST, preferred_element_type=F32)
    b_row = lax.dot_general(g_row, tri, (((1,), (1,)), ((), ())), precision=HIGHEST,
                            preferred_element_type=F32)

    outs = []
    for h in range(ML_HEADS):
        sl = slice(h * ML_HDIM, (h + 1) * ML_HDIM)
        li, lf = dsel + h, 2 * ML_HEADS + dsel + h
        q = q_ref[rws, sl]
        kc = k_ref[rws, sl] * scale
        v = v_ref[rws, sl]
        qb, kb, vb = q.astype(BF16), kc.astype(BF16), v.astype(BF16)
        bc = b_col[:, lf:lf + 1]
        br = b_row[lf:lf + 1, :]
        ic = g_col[:, li:li + 1]
        ir = g_row[li:li + 1, :]
        btot = b_col[edge:edge + 1, lf:lf + 1]
        m_prev = m_ref[0:1, h:h + 1]
        n_prev = n_ref[h:h + 1, :]
        c_prev = c_ref[h]

        log_d = jnp.where(causal, bc - br + ir, -jnp.inf)
        log_inter = bc + m_prev
        m_t = jnp.maximum(log_inter, jnp.max(log_d, axis=1, keepdims=True))
        a_inter = jnp.exp(log_inter - m_t)
        s = lax.dot_general(qb, kb, (((1,), (1,)), ((), ())), preferred_element_type=F32)
        s = s * jnp.exp(log_d - m_t)
        num = a_inter * jnp.dot(qb, c_prev.astype(BF16), preferred_element_type=F32) \
            + jnp.dot(s.astype(BF16), vb, preferred_element_type=F32)
        den = a_inter * jnp.sum(q * n_prev, axis=1, keepdims=True) + jnp.sum(s, axis=1, keepdims=True)
        outs.append(num / jnp.maximum(jnp.abs(den), jnp.exp(-m_t)))

        log_w = btot - bc + ic
        m_new = jnp.maximum(btot + m_prev, jnp.max(log_w, axis=0, keepdims=True))
        a_state = jnp.exp(btot + m_prev - m_new)
        kw = kc * jnp.exp(log_w - m_new)
        c_ref[h] = a_state * c_prev + lax.dot_general(kw.astype(BF16), vb, (((0,), (0,)), ((), ())),
                                                      preferred_element_type=F32)
        n_ref[h:h + 1, :] = a_state * n_prev + jnp.sum(kw, axis=0, keepdims=True)
        m_ref[0:1, h:h + 1] = m_new

    hcur = jnp.concatenate(outs, axis=1)
    if finalize:
        hsum = hcur + hf_ref[rws, :]
        ys = []
        for h in range(ML_HEADS):
            sl = slice(h * ML_HDIM, (h + 1) * ML_HDIM)
            hh = hsum[:, sl]
            ys.append(hh * lax.rsqrt(jnp.mean(hh * hh, axis=1, keepdims=True) + EPS))
        y = jnp.concatenate(ys, axis=1) * ng_ref[...] * jax.nn.sigmoid(og_ref[rws, :])
        o_ref[rws, :] = y.astype(o_ref.dtype)
    else:
        o_ref[rws, :] = hcur


ML_TB = 2 * ML_CHUNK


def _mlstm(proj, gate_bias, norm_g, row0, B, L):
    C = ML_TB
    nT = L // C
    blk0 = row0 // C
    tri_f = jnp.asarray(np.tril(np.ones((ML_CHUNK, ML_CHUNK), np.float32)))
    tri_b = jnp.asarray(np.triu(np.ones((ML_CHUNK, ML_CHUNK), np.float32)))

    def rows(reverse):
        if reverse:
            return lambda b, t: blk0 + b * nT + nT - 1 - t
        return lambda b, t: blk0 + b * nT + t

    def in_spec(slab, reverse):
        r = rows(reverse)
        return pl.BlockSpec((C, BRANCH_W), lambda b, t: (r(b, t), slab))

    def gate_spec(reverse):
        r = rows(reverse)
        return pl.BlockSpec((C, LANES), lambda b, t: (r(b, t), GATE_BLOCK))

    def out_spec(reverse):
        if reverse:
            return pl.BlockSpec((C, BRANCH_W), lambda b, t: (b * nT + nT - 1 - t, 0))
        return pl.BlockSpec((C, BRANCH_W), lambda b, t: (b * nT + t, 0))

    const = lambda shape: pl.BlockSpec(shape, lambda b, t: (0,) * len(shape))
    scratch = [pltpu.VMEM((ML_HEADS, ML_HDIM, ML_HDIM), F32), pltpu.VMEM((8, ML_HDIM), F32),
               pltpu.VMEM((8, LANES), F32)]
    cp = _cparams(("parallel", "arbitrary"))
    h_f = pl.pallas_call(
        functools.partial(_mlstm_kernel, reverse=False, finalize=False), grid=(B, nT),
        in_specs=[in_spec(SLAB_MQ, False), in_spec(SLAB_MK, False), in_spec(SLAB_MV, False),
                  gate_spec(False), const((1, LANES)), const((ML_CHUNK, ML_CHUNK))],
        out_specs=out_spec(False), out_shape=jax.ShapeDtypeStruct((B * L, BRANCH_W), F32),
        scratch_shapes=scratch, compiler_params=cp, name="mlstm_fwd")(
            proj, proj, proj, proj, gate_bias, tri_f)
    return pl.pallas_call(
        functools.partial(_mlstm_kernel, reverse=True, finalize=True), grid=(B, nT),
        in_specs=[in_spec(SLAB_MQ, True), in_spec(SLAB_MK, True), in_spec(SLAB_MV, True),
                  gate_spec(True), const((1, LANES)), const((ML_CHUNK, ML_CHUNK)),
                  out_spec(True), in_spec(SLAB_MO, True), const((1, BRANCH_W))],
        out_specs=out_spec(True), out_shape=jax.ShapeDtypeStruct((B * L, BRANCH_W), BF16),
        scratch_shapes=scratch, compiler_params=cp, name="mlstm_bwd")(
            proj, proj, proj, proj, gate_bias, tri_b, h_f, proj, norm_g)


HY_TB = 512


def _shortconv_kernel(c_ref, p_ref, n_ref, w_ref, b_ref, x0_ref, u_ref, *, nT):
    t = pl.program_id(1)
    cur = c_ref[...]
    prev_row = jnp.where(t > 0, p_ref[7:8, :], 0.0)
    next_row = jnp.where(t < nT - 1, n_ref[0:1, :], 0.0)
    tb = cur.shape[0]
    row = lax.broadcasted_iota(jnp.int32, cur.shape, 0)
    up = jnp.where(row == 0, prev_row, pltpu.roll(cur, 1, axis=0))
    dn = jnp.where(row == tb - 1, next_row, pltpu.roll(cur, tb - 1, axis=0))
    y = up * w_ref[0:1, :] + cur * w_ref[1:2, :] + dn * w_ref[2:3, :] + b_ref[...]
    x0_ref[...] = y[:, :HY_WIDTH]
    u_ref[...] = y[:, HY_WIDTH:2 * HY_WIDTH] * y[:, 2 * HY_WIDTH:]


def _short_conv(proj, w, b, row0, B, L):
    tb = min(HY_TB, L)
    nT = L // tb
    blk0 = row0 // tb
    sub = tb // 8
    W3 = 3 * HY_WIDTH
    slab = SLAB_HY * BRANCH_W // W3
    cur = pl.BlockSpec((tb, W3), lambda bb, t: (blk0 + bb * nT + t, slab))
    prv = pl.BlockSpec((8, W3), lambda bb, t: (jnp.maximum((blk0 + bb * nT + t) * sub - 1, 0), slab))
    nxt = pl.BlockSpec((8, W3), lambda bb, t: (jnp.minimum((blk0 + bb * nT + t + 1) * sub,
                                                           (blk0 + B * nT) * sub - 1), slab))
    const = lambda shape: pl.BlockSpec(shape, lambda bb, t: (0,) * len(shape))
    out = pl.BlockSpec((tb, HY_WIDTH), lambda bb, t: (bb * nT + t, 0))
    return pl.pallas_call(
        functools.partial(_shortconv_kernel, nT=nT), grid=(B, nT),
        in_specs=[cur, prv, nxt, const((8, W3)), const((1, W3))], out_specs=[out, out],
        out_shape=[jax.ShapeDtypeStruct((B * L, HY_WIDTH), F32)] * 2,
        compiler_params=_cparams(("parallel", "parallel")), name="hyena_short_conv")(proj, proj, proj, w, b)


HYF_TB = 256


def _filter_kernel(band_ref, w1_ref, b1_ref, w2_ref, b2_ref, fr_ref, w3_ref, dl_ref, h_ref, l1_ref, *, L):
    i = pl.program_id(0)
    pos = (lax.broadcasted_iota(jnp.int32, (HYF_TB, LANES), 0) + i * HYF_TB).astype(F32)
    lane = lax.broadcasted_iota(jnp.int32, (HYF_TB, LANES), 1)
    t = pos / (L - 1)
    ang = (2.0 * math.pi * pos / L) * band_ref[...]
    z = jnp.where(lane == 0, t,
                  jnp.where(lane <= HY_BANDS, jnp.cos(ang),
                            jnp.where(lane <= 2 * HY_BANDS, -jnp.sin(ang), 0.0)))
    fr = fr_ref[...]
    h = jnp.sin(fr * (jnp.dot(z, w1_ref[...], precision=HIGHEST, preferred_element_type=F32) + b1_ref[...]))
    h = jnp.sin(fr * (jnp.dot(h, w2_ref[...], precision=HIGHEST, preferred_element_type=F32) + b2_ref[...]))
    h = jnp.dot(h, w3_ref[...], precision=HIGHEST, preferred_element_type=F32)
    tt = (lax.broadcasted_iota(jnp.int32, (HYF_TB, HY_WIDTH), 0) + i * HYF_TB).astype(F32) / (L - 1)
    window = jnp.exp(-tt * dl_ref[...])
    rowi = lax.broadcasted_iota(jnp.int32, (HYF_TB, HY_WIDTH), 0) + i * HYF_TB
    hf = h[:, :HY_WIDTH] * window
    hb = jnp.where(rowi == 0, 0.0, h[:, HY_WIDTH:] * window)
    h_ref[0] = hf
    h_ref[1] = hb

    @pl.when(i == 0)
    def _():
        l1_ref[...] = jnp.zeros_like(l1_ref)

    l1_ref[...] += jnp.sum(jnp.abs(hf) + jnp.abs(hb), axis=0, keepdims=True)


def _hyena_filter(L, w1, b1, w2, b2, freq, w3):
    band = np.zeros((1, LANES), np.float32)
    bands = np.linspace(1e-4, HY_BANDS - 1, HY_BANDS, dtype=np.float32)
    band[0, 1:1 + HY_BANDS] = bands
    band[0, 1 + HY_BANDS:1 + 2 * HY_BANDS] = bands
    max_decay = math.log(HY_DECAY_TARGET) / HY_SHORT_DECAY_PCT
    min_decay = math.log(HY_DECAY_TARGET) / HY_LONG_DECAY_PCT
    deltas = np.abs(np.linspace(min_decay, max_decay, HY_WIDTH, dtype=np.float32))[None, :]
    w1p = jnp.zeros((LANES, HY_FILTER_HIDDEN), F32).at[:HY_EMB].set(w1.astype(F32))
    const = lambda shape: pl.BlockSpec(shape, lambda i: (0,) * len(shape))
    H = HY_FILTER_HIDDEN
    return pl.pallas_call(
        functools.partial(_filter_kernel, L=L), grid=(L // HYF_TB,),
        in_specs=[const((1, LANES)), const((LANES, H)), const((1, H)), const((H, H)), const((1, H)),
                  const((1, H)), const((H, 2 * HY_WIDTH)), const((1, HY_WIDTH))],
        out_specs=[pl.BlockSpec((2, HYF_TB, HY_WIDTH), lambda i: (0, i, 0)), const((1, HY_WIDTH))],
        out_shape=[jax.ShapeDtypeStruct((2, L, HY_WIDTH), F32), jax.ShapeDtypeStruct((1, HY_WIDTH), F32)],
        compiler_params=_cparams(("arbitrary",)), name="hyena_filter")(
            jnp.asarray(band), w1p, b1.astype(F32)[None], w2.astype(F32), b2.astype(F32)[None],
            freq.astype(F32)[None], w3.astype(F32), jnp.asarray(deltas))


def _fft_factors(n):
    lg = int(round(math.log2(n)))
    n1 = 1 << (lg // 2)
    return n1, n // n1


def _dft(n):
    k = np.arange(n)
    a = -2.0 * np.pi * ((k[:, None] * k[None, :]) % n) / n
    return np.cos(a), np.sin(a)


FFT_G = 8


def _split3(f):
    f = jnp.asarray(np.asarray(f, np.float32))
    hi = f.astype(BF16)
    lo = (f - hi.astype(F32)).astype(BF16)
    return jnp.concatenate([hi, lo, hi], axis=1)


def _dot3(f3, x):
    hi = x.astype(BF16)
    lo = (x - hi.astype(F32)).astype(BF16)
    return jnp.dot(f3, jnp.concatenate([hi, hi, lo], axis=0), preferred_element_type=F32)


def _fft1_kernel(u_ref, f_ref, yr_ref, yi_ref, *, n1):
    P, _, G, C = u_ref.shape[1:]
    y = _dot3(f_ref[...], u_ref[0].reshape(P * (n1 // 2) * G, C))
    yr_ref[0] = y[:n1 * G].reshape(n1, G, C)
    yi_ref[0] = y[n1 * G:].reshape(n1, G, C)


def _fft_stage1(u, n1, n2, pair):
    B, L, C = u.shape
    P = 2 if pair else 1
    fr, fi = _dft(n1)
    G = FFT_G
    eye = np.eye(G)
    kr, ki = np.kron(fr[:, :n1 // 2], eye), np.kron(fi[:, :n1 // 2], eye)
    f3 = _split3(np.block([[kr, -ki], [ki, kr]]) if pair else np.concatenate([kr, ki], 0))
    blk = pl.BlockSpec((1, P, n1 // 2, G, C), lambda b, j: (b, 0, 0, j, 0))
    oblk = pl.BlockSpec((1, n1, G, C), lambda b, j: (b, 0, j, 0))
    shp = jax.ShapeDtypeStruct((B // P, n1, n2, C), F32)
    yr, yi = pl.pallas_call(
        functools.partial(_fft1_kernel, n1=n1), grid=(B // P, n2 // G),
        in_specs=[blk, pl.BlockSpec(f3.shape, lambda b, j: (0, 0))],
        out_specs=[oblk, oblk], out_shape=[shp, shp],
        compiler_params=_cparams(("parallel", "parallel")), name="fft_stage1")(
            u.reshape(B // P, P, n1 // 2, n2, C), f3)
    return yr.reshape(B // P, n1 * n2, C), yi.reshape(B // P, n1 * n2, C)


def _cmul(ar, ai, br, bi):
    return ar * br - ai * bi, ar * bi + ai * br


def _fft2_fwd(yr, yi, tr, ti, f2, n2, C):
    ar, ai = _cmul(yr, yi, tr, ti)
    p = _dot3(f2, jnp.concatenate([ar, ai], axis=1))
    return p[:n2, :C] - p[n2:, C:], p[:n2, C:] + p[n2:, :C]


FFT_ROWS = 512


def _fft2_filter_kernel(yr_ref, yi_ref, tr_ref, ti_ref, f_ref, l1_ref, kr_ref, ki_ref, *, n2, n):
    C = yr_ref.shape[2]
    f2 = f_ref[...]
    scale = 1.0 / (l1_ref[...] * n)
    for kb in range(FFT_ROWS // n2):
        rws = slice(kb * n2, (kb + 1) * n2)
        tr = jnp.tile(tr_ref[kb], (1, C // LANES))
        ti = jnp.tile(ti_ref[kb], (1, C // LANES))
        gr, gi = _fft2_fwd(yr_ref[0, rws, :], yi_ref[0, rws, :], tr, ti, f2, n2, C)
        hr, hi = _fft2_fwd(yr_ref[1, rws, :], yi_ref[1, rws, :], tr, ti, f2, n2, C)
        kr_ref[rws, :] = (gr + hr) * scale
        ki_ref[rws, :] = (gi - hi) * scale


def _fft2_conv_kernel(yr_ref, yi_ref, tr_ref, ti_ref, f_ref, kr_ref, ki_ref, zr_ref, zi_ref, *, n2):
    C = yr_ref.shape[2]
    f2 = f_ref[...]
    for kb in range(FFT_ROWS // n2):
        rws = slice(kb * n2, (kb + 1) * n2)
        tr = jnp.tile(tr_ref[kb], (1, C // LANES))
        ti = jnp.tile(ti_ref[kb], (1, C // LANES))
        xr, xi = _fft2_fwd(yr_ref[0, rws, :], yi_ref[0, rws, :], tr, ti, f2, n2, C)
        vr, vi = _cmul(xr, xi, kr_ref[rws, :], ki_ref[rws, :])
        q = _dot3(f2, jnp.concatenate([vr, vi], axis=1))
        wr = q[:n2, :C] + q[n2:, C:]
        wi = q[:n2, C:] - q[n2:, :C]
        zr, zi = _cmul(wr, wi, tr, -ti)
        zr_ref[0, rws, :] = zr
        zi_ref[0, rws, :] = zi


def _fft_tables(n1, n2):
    n = n1 * n2
    k1 = jnp.arange(n1, dtype=jnp.int32)[:, None]
    j2 = jnp.arange(n2, dtype=jnp.int32)[None, :]
    a = (-2.0 * math.pi / n) * (k1 * j2).astype(F32)
    tr = jnp.broadcast_to(jnp.cos(a)[:, :, None], (n1, n2, LANES))
    ti = jnp.broadcast_to(jnp.sin(a)[:, :, None], (n1, n2, LANES))
    fr, fi = _dft(n2)
    return tr, ti, _split3(np.concatenate([fr, fi], 0))


def _fft_stage2_filter(yr, yi, tables, l1, n1, n2):
    _, N, C = yr.shape
    tr, ti, f2 = tables
    kb = FFT_ROWS // n2
    blk = pl.BlockSpec((2, FFT_ROWS, C), lambda k: (0, k, 0))
    tblk = pl.BlockSpec((kb, n2, LANES), lambda k: (k, 0, 0))
    oblk = pl.BlockSpec((FFT_ROWS, C), lambda k: (k, 0))
    shp = jax.ShapeDtypeStruct((N, C), F32)
    return pl.pallas_call(
        functools.partial(_fft2_filter_kernel, n2=n2, n=N), grid=(N // FFT_ROWS,),
        in_specs=[blk, blk, tblk, tblk, pl.BlockSpec(f2.shape, lambda k: (0, 0)),
                  pl.BlockSpec((1, C), lambda k: (0, 0))],
        out_specs=[oblk, oblk], out_shape=[shp, shp],
        compiler_params=_cparams(("parallel",)), name="fft_stage2_filter")(yr, yi, tr, ti, f2, l1)


def _fft_stage2_conv(yr, yi, tables, spec_r, spec_i, n1, n2):
    B, N, C = yr.shape
    tr, ti, f2 = tables
    kb = FFT_ROWS // n2
    blk = pl.BlockSpec((1, FFT_ROWS, C), lambda b, k: (b, k, 0))
    tblk = pl.BlockSpec((kb, n2, LANES), lambda b, k: (k, 0, 0))
    sblk = pl.BlockSpec((FFT_ROWS, C), lambda b, k: (k, 0))
    shp = jax.ShapeDtypeStruct((B, N, C), F32)
    return pl.pallas_call(
        functools.partial(_fft2_conv_kernel, n2=n2), grid=(B, N // FFT_ROWS),
        in_specs=[blk, blk, tblk, tblk, pl.BlockSpec(f2.shape, lambda b, k: (0, 0)), sblk, sblk],
        out_specs=[blk, blk], out_shape=[shp, shp],
        compiler_params=_cparams(("parallel", "parallel")), name="fft_stage2_conv")(
            yr, yi, tr, ti, f2, spec_r, spec_i)


def _fft3_kernel(zr_ref, zi_ref, f_ref, x0_ref, u_ref, bias_ref, o_ref):
    n1, G, C = zr_ref.shape[1:]
    P = u_ref.shape[1]
    z = jnp.concatenate([zr_ref[0].reshape(n1 * G, C), zi_ref[0].reshape(n1 * G, C)], axis=0)
    conv = _dot3(f_ref[...], z).reshape(P, n1 // 2, G, C)
    o_ref[0] = x0_ref[0] * (conv + u_ref[0] * bias_ref[...])


def _fft_stage3(zr, zi, x0, u, bias, n1, n2, pair):
    B, L, C = u.shape
    P = 2 if pair else 1
    fr, fi = _dft(n1)
    G = FFT_G
    eye = np.eye(G)
    kr, ki = np.kron(fr[:n1 // 2], eye), np.kron(fi[:n1 // 2], eye)
    f3 = _split3(np.block([[kr, ki], [-ki, kr]]) if pair else np.concatenate([kr, ki], 1))
    zblk = pl.BlockSpec((1, n1, G, C), lambda b, j: (b, 0, j, 0))
    ublk = pl.BlockSpec((1, P, n1 // 2, G, C), lambda b, j: (b, 0, 0, j, 0))
    v5 = lambda a: a.reshape(B // P, P, n1 // 2, n2, C)
    out = pl.pallas_call(
        _fft3_kernel, grid=(B // P, n2 // G),
        in_specs=[zblk, zblk, pl.BlockSpec(f3.shape, lambda b, j: (0, 0)), ublk, ublk,
                  pl.BlockSpec((1, C), lambda b, j: (0, 0))],
        out_specs=ublk, out_shape=jax.ShapeDtypeStruct((B // P, P, n1 // 2, n2, C), F32),
        compiler_params=_cparams(("parallel", "parallel")), name="fft_stage3")(
            zr.reshape(B // P, n1, n2, C), zi.reshape(B // P, n1, n2, C), f3, v5(x0), v5(u), bias)
    return out.reshape(B * L, C)


def _hyena(proj, conv_w, conv_b, filt, hy_bias, row0, B, L):
    n1, n2 = _fft_factors(2 * L)
    pair = B % 2 == 0
    x0, u = _short_conv(proj, conv_w, conv_b, row0, B, L)
    x0 = x0.reshape(B, L, HY_WIDTH)
    u = u.reshape(B, L, HY_WIDTH)
    tables = _fft_tables(n1, n2)
    hfb, l1 = _hyena_filter(L, *filt)
    fr, fi = _fft_stage1(hfb, n1, n2, False)
    sr, si = _fft_stage2_filter(fr, fi, tables, l1, n1, n2)
    yr, yi = _fft_stage1(u, n1, n2, pair)
    zr, zi = _fft_stage2_conv(yr, yi, tables, sr, si, n1, n2)
    return _fft_stage3(zr, zi, x0, u, hy_bias, n1, n2, pair)


def _group_specs(parts, tm):
    starts, specs = [], []
    s = 0
    for a in parts:
        n = a.shape[0] // tm
        starts.append(s)
        specs.append(pl.BlockSpec((tm, a.shape[1]),
                                  (lambda s, n: lambda i, *_: (jnp.clip(i - s, 0, n - 1), 0))(s, n)))
        s += n
    return starts, specs


def _group_pick(refs, starts):
    i = pl.program_id(0)
    v = refs[0][...]
    for r, s in zip(refs[1:], starts[1:]):
        v = jnp.where(i >= s, r[...], v)
    return v


def _merge_kernel(*refs, starts):
    ng = len(starts)
    x_ref, g_ref, wg_ref = refs[:3]
    y_refs = [refs[3 + j * ng:3 + (j + 1) * ng] for j in range(N_BRANCHES)]
    wa_ref, wb_ref, wc_ref, wo_ref, o_ref = refs[3 + N_BRANCHES * ng:]
    x = x_ref[...]
    xn = _rms(x, g_ref[...]).astype(BF16)
    merged = None
    for j, w_ref in enumerate((wa_ref, wb_ref, wc_ref)):
        gate = jnp.dot(xn, wg_ref[:, j * D_MODEL:(j + 1) * D_MODEL], preferred_element_type=F32)
        y = _group_pick(y_refs[j], starts)
        br = jnp.dot(y.astype(BF16), w_ref[...], preferred_element_type=F32)
        term = jax.nn.sigmoid(gate) * br
        merged = term if merged is None else merged + term
    o_ref[...] = x + jnp.dot(merged.astype(BF16), wo_ref[...], preferred_element_type=F32)


def _merge(x, g, wg, ya, yb, yc, wa, wb, wc, wo):
    T = x.shape[0]
    tm = min(512, T)
    xspec = pl.BlockSpec((tm, D_MODEL), lambda i: (i, 0))
    const = lambda shape: pl.BlockSpec(shape, lambda i: (0,) * len(shape))
    starts, yspecs = _group_specs(ya, tm)
    return pl.pallas_call(
        functools.partial(_merge_kernel, starts=starts), grid=(T // tm,),
        in_specs=[xspec, const((1, D_MODEL)), const((D_MODEL, 3 * D_MODEL))] + yspecs * N_BRANCHES
        + [const((BRANCH_W, D_MODEL)), const((BRANCH_W, D_MODEL)), const((BRANCH_W, D_MODEL)),
           const((D_MODEL, D_MODEL))],
        out_specs=xspec, out_shape=jax.ShapeDtypeStruct((T, D_MODEL), F32),
        compiler_params=_cparams(("parallel",)), name="merge_out_projection")(
            x, g, wg, *ya, *yb, *yc, wa, wb, wc, wo)


def _route(xn, w, b):
    lg = jnp.dot(xn, w, precision=HIGHEST, preferred_element_type=F32) + b
    lane = lax.broadcasted_iota(jnp.int32, lg.shape, 1).astype(F32)
    neg = -jnp.inf
    is_g = lane < N_GROUPS
    gl = jnp.where(is_g, lg, neg)
    gmax = jnp.max(gl, axis=1, keepdims=True)
    gsel = jnp.min(jnp.where(gl == gmax, lane, float(LANES)), axis=1, keepdims=True)
    gprob = 1.0 / jnp.sum(jnp.where(is_g, jnp.exp(lg - gmax), 0.0), axis=1, keepdims=True)
    lo = N_GROUPS + gsel * EXPERTS_PER_GROUP
    el = jnp.where((lane >= lo) & (lane < lo + EXPERTS_PER_GROUP), lg, neg)
    v1 = jnp.max(el, axis=1, keepdims=True)
    i1 = jnp.min(jnp.where(el == v1, lane, float(LANES)), axis=1, keepdims=True)
    el2 = jnp.where(lane == i1, neg, el)
    v2 = jnp.max(el2, axis=1, keepdims=True)
    i2 = jnp.min(jnp.where(el2 == v2, lane, float(LANES)), axis=1, keepdims=True)
    e = jnp.exp(v2 - v1)
    w1 = gprob / (1.0 + e)
    w2 = w1 * e
    return jnp.where(lane == 0, i1 - N_GROUPS,
                     jnp.where(lane == 1, i2 - N_GROUPS,
                               jnp.where(lane == 2, w1, jnp.where(lane == 3, w2, 0.0))))


def _router_kernel(x_ref, g_ref, w_ref, b_ref, xn_ref, r_ref):
    xn = _rms(x_ref[...], g_ref[...])
    xn_ref[...] = xn
    r_ref[...] = _route(xn, w_ref[...], b_ref[...])


def _router(x, g, w, b):
    T = x.shape[0]
    tm = min(512, T)
    xspec = pl.BlockSpec((tm, D_MODEL), lambda i: (i, 0))
    const = lambda shape: pl.BlockSpec(shape, lambda i: (0,) * len(shape))
    return pl.pallas_call(
        _router_kernel, grid=(T // tm,),
        in_specs=[xspec, const((1, D_MODEL)), const((D_MODEL, LANES)), const((1, LANES))],
        out_specs=[xspec, pl.BlockSpec((tm, LANES), lambda i: (i, 0))],
        out_shape=[jax.ShapeDtypeStruct((T, D_MODEL), F32), jax.ShapeDtypeStruct((T, LANES), F32)],
        compiler_params=_cparams(("parallel",)), name="router")(x, g, w, b)


MOE_ROWS = 512


def _moe_kernel(be_ref, nu_ref, idx_hbm, xn_hbm, wg0, wu0, wd0, wg1, wu1, wd1, out_hbm,
                idx, xbuf, obuf, isem, gsem, ssem):
    R = MOE_ROWS
    j = pl.program_id(0)
    nu = nu_ref[0]
    a = 2 * j
    b = a + 1

    def idx_copy(blk, slot):
        return pltpu.make_async_copy(idx_hbm.at[blk], idx.at[slot], isem.at[slot])

    def issue_gathers(slot):
        for r in range(R):
            pltpu.make_async_copy(xn_hbm.at[pl.ds(idx[slot, r], 1)], xbuf.at[slot, pl.ds(r, 1)],
                                  gsem.at[slot]).start()

    def wait_gathers(slot):
        pltpu.make_async_copy(xn_hbm.at[pl.ds(0, R)], xbuf.at[slot], gsem.at[slot]).wait()

    def issue_scatters(slot):
        for r in range(R):
            pltpu.make_async_copy(obuf.at[slot, pl.ds(r, 1)], out_hbm.at[pl.ds(idx[slot, R + r], 1)],
                                  ssem.at[slot]).start()

    def wait_scatters(slot):
        pltpu.make_async_copy(obuf.at[slot], out_hbm.at[pl.ds(0, R)], ssem.at[slot]).wait()

    def compute(slot, wg, wu, wd):
        xb = xbuf[slot].astype(BF16)
        h = jax.nn.silu(jnp.dot(xb, wg[0], preferred_element_type=F32)) \
            * jnp.dot(xb, wu[0], preferred_element_type=F32)
        obuf[slot] = jnp.dot(h.astype(BF16), wd[0], preferred_element_type=F32)

    @pl.when(j == 0)
    def _():
        obuf[0] = jnp.zeros((R, D_MODEL), F32)
        n_rows = out_hbm.shape[0]
        for k in (2, 1):
            z = pltpu.make_async_copy(obuf.at[0], out_hbm.at[pl.ds(n_rows - k * R, R)], ssem.at[0])
            z.start()
            z.wait()
        c = idx_copy(0, 0)
        c.start()
        c.wait()
        issue_gathers(0)

        @pl.when(1 < nu)
        def _():
            idx_copy(1, 1).start()

    @pl.when(a < nu)
    def _():
        wait_gathers(0)

        @pl.when(b < nu)
        def _():
            idx_copy(b, 1).wait()
            issue_gathers(1)

        @pl.when(j >= 1)
        def _():
            wait_scatters(0)

        compute(0, wg0, wu0, wd0)
        issue_scatters(0)

        @pl.when(a + 2 < nu)
        def _():
            idx_copy(a + 2, 0).start()

    @pl.when(b < nu)
    def _():
        wait_gathers(1)

        @pl.when(b + 1 < nu)
        def _():
            idx_copy(b + 1, 0).wait()
            issue_gathers(0)

        @pl.when(j >= 1)
        def _():
            wait_scatters(1)

        compute(1, wg1, wu1, wd1)
        issue_scatters(1)

        @pl.when(b + 2 < nu)
        def _():
            idx_copy(b + 2, 1).start()

    @pl.when((a < nu) & (a + 2 >= nu))
    def _():
        wait_scatters(0)

        @pl.when((b < nu) | (j >= 1))
        def _():
            wait_scatters(1)


def _moe_experts(xn, idx, block_e, n_used, wg, wu, wd):
    T = xn.shape[0]
    R = MOE_ROWS
    n_blocks = idx.shape[0]
    assert n_blocks % 2 == 0
    wspec = lambda o, shp: pl.BlockSpec(shp, lambda j, be, nu: (be[2 * j + o], 0, 0))
    w_in, w_out = (1, D_MODEL, EXPERT_HIDDEN), (1, EXPERT_HIDDEN, D_MODEL)
    grid_spec = pltpu.PrefetchScalarGridSpec(
        num_scalar_prefetch=2, grid=(n_blocks // 2,),
        in_specs=[pl.BlockSpec(memory_space=pl.ANY), pl.BlockSpec(memory_space=pl.ANY),
                  wspec(0, w_in), wspec(0, w_in), wspec(0, w_out),
                  wspec(1, w_in), wspec(1, w_in), wspec(1, w_out)],
        out_specs=pl.BlockSpec(memory_space=pl.ANY),
        scratch_shapes=[pltpu.SMEM((2, 2 * R), jnp.int32), pltpu.VMEM((2, R, D_MODEL), F32),
                        pltpu.VMEM((2, R, D_MODEL), F32), pltpu.SemaphoreType.DMA((2,)),
                        pltpu.SemaphoreType.DMA((2,)), pltpu.SemaphoreType.DMA((2,))])
    return pl.pallas_call(
        _moe_kernel, grid_spec=grid_spec, out_shape=jax.ShapeDtypeStruct((TOP_K * T + 2 * R, D_MODEL), F32),
        compiler_params=_cparams(("arbitrary",)), name="moe_experts")(
            block_e, n_used, idx, xn, wg, wu, wd, wg, wu, wd)


def _dispatch(route, T):
    R = MOE_ROWS
    M = T * TOP_K
    expert_id = route[:, :TOP_K].astype(jnp.int32).reshape(-1)
    order = jnp.argsort(expert_id).astype(jnp.int32)
    counts =jnp.bincount(expert_id, length=N_EXPERTS).astype(jnp.int32)
    starts = jnp.cumsum(counts) - counts
    padded = (counts + R - 1) // R * R
    p_ends = jnp.cumsum(padded)
    p_starts = p_ends - padded
    n_blocks = -(-M // R) + N_EXPERTS
    first_row = jnp.arange(n_blocks, dtype=jnp.int32) * R
    block_e = jnp.minimum(jnp.sum(p_ends[None, :] <= first_row[:, None], axis=1), N_EXPERTS - 1).astype(jnp.int32)
    row = jnp.arange(n_blocks * R, dtype=jnp.int32).reshape(n_blocks, R)
    k = row - p_starts[block_e][:, None]
    valid = (k < counts[block_e][:, None]) & (row < p_ends[N_EXPERTS - 1])
    a = order[jnp.clip(starts[block_e][:, None] + k, 0, M - 1)]
    token = a >> 1
    src = jnp.where(valid, token, 0)
    scrap = M + row % (2 * R)
    dst = jnp.where(valid, (a & 1) * T + token, scrap)
    n_used = (p_ends[N_EXPERTS - 1:] // R).astype(jnp.int32)
    return jnp.concatenate([src, dst], axis=1), block_e, n_used


def _final_kernel(x_ref, e0_ref, e1_ref, r_ref, g_ref, o_ref):
    r = r_ref[...]
    x = x_ref[...] + e0_ref[...] * r[:, 2:3] + e1_ref[...] * r[:, 3:4]
    o_ref[...] = _rms(x, g_ref[...])


def _final(x, e2, route, g, row0, rows):
    T = x.shape[0]
    tm = min(512, rows)
    nT = T // tm
    b0 = row0 // tm
    xspec = pl.BlockSpec((tm, D_MODEL), lambda i: (b0 + i, 0))
    return pl.pallas_call(
        _final_kernel, grid=(rows // tm,),
        in_specs=[xspec, xspec, pl.BlockSpec((tm, D_MODEL), lambda i: (nT + b0 + i, 0)),
                  pl.BlockSpec((tm, LANES), lambda i: (b0 + i, 0)), pl.BlockSpec((1, D_MODEL), lambda i: (0, 0))],
        out_specs=pl.BlockSpec((tm, D_MODEL), lambda i: (i, 0)),
        out_shape=jax.ShapeDtypeStruct((rows, D_MODEL), F32),
        compiler_params=_cparams(("parallel",)), name="combine_final_norm")(x, e2, e2, route, g)


def _pack_layer(l, p):
    w = p['w_in'][l]
    o_gate = 3 * HG_WIDTH + 2 * HG_WIDTH + 4 * ML_WIDTH
    o_hy = o_gate + 4 * ML_HEADS
    o_g = o_hy + 3 * HY_WIDTH
    gates = jnp.pad(w[:, o_gate:o_hy], ((0, 0), (0, PROJ_W - PROJ_MAIN - 4 * ML_HEADS)))
    w_proj = jnp.concatenate([w[:, :o_gate], w[:, o_hy:o_g], gates], axis=1).astype(BF16)
    w_gate = w[:, o_g:].astype(BF16)
    gate_bias = jnp.pad(jnp.concatenate([p['ml_i_bias'][l].reshape(-1), p['ml_f_bias'][l].reshape(-1)]),
                        (0, LANES - 4 * ML_HEADS)).astype(F32)[None]
    w_router = jnp.pad(jnp.concatenate([p['router_group_w'][l], p['router_expert_w'][l]], axis=1),
                       ((0, 0), (0, LANES - N_GROUPS - N_EXPERTS))).astype(F32)
    b_router = jnp.pad(jnp.concatenate([p['router_group_b'][l], p['router_expert_b'][l]]),
                       (0, LANES - N_GROUPS - N_EXPERTS)).astype(F32)[None]
    return dict(
        norm_mix_g=p['norm_mix_g'][l].astype(F32)[None], w_proj=w_proj, w_gate=w_gate,
        hg_norm_g=p['hg_norm_g'][l].astype(F32)[None], gate_bias=gate_bias,
        ml_norm_g=p['ml_norm_g'][l].astype(F32)[None],
        conv_w=jnp.pad(p['hy_conv_w'][l].astype(F32), ((0, 5), (0, 0))), conv_b=p['hy_conv_b'][l].astype(F32)[None],
        filt=(p['filt_w1'][l], p['filt_b1'][l], p['filt_w2'][l], p['filt_b2'][l], p['filt_freq'][l],
              p['filt_w3'][l]),
        hy_bias=p['hy_bias'][l].astype(F32)[None],
        wa=p['w_branch_a'][l].astype(BF16), wb=p['w_branch_b'][l].astype(BF16),
        wc=p['w_branch_c'][l].astype(BF16), wo=p['w_out'][l].astype(BF16),
        norm_ffn_g=p['norm_ffn_g'][l].astype(F32)[None], w_router=w_router, b_router=b_router,
        wg=p['exp_w_gate'][l].astype(BF16), wu=p['exp_w_up'][l].astype(BF16),
        wd=p['exp_w_down'][l].astype(BF16))


def _trunk(x, groups, p):
    T = x.shape[0]
    lbs = jnp.cumsum(jax.nn.softmax(p['hg_lb'].astype(F32), axis=0), axis=0)
    lbs = lbs - lbs[0:1]
    moe = None
    for l in range(DEPTH):
        lp = _pack_layer(l, p)
        lb = lbs[l][None]
        lbrow = jnp.concatenate([jnp.log(lb), jnp.log1p(-lb), 1.0 - lb, jnp.zeros((5, HG_WIDTH), F32)], axis=0)
        if moe is not None:
            x = _combine(x, *moe)
        proj = _in_projection(x, lp['norm_mix_g'], lp['w_proj'])
        ya, yb, yc = [], [], []
        for row0, B, L in groups:
            ya.append(_hgrn2(proj, lbrow, lp['hg_norm_g'], row0, B, L))
            yb.append(_mlstm(proj, lp['gate_bias'], lp['ml_norm_g'], row0, B, L))
            yc.append(_hyena(proj, lp['conv_w'], lp['conv_b'], lp['filt'], lp['hy_bias'], row0, B, L))
        x = _merge(x, lp['norm_mix_g'], lp['w_gate'], ya, yb, yc, lp['wa'], lp['wb'], lp['wc'], lp['wo'])
        xn, route = _router(x, lp['norm_ffn_g'], lp['w_router'], lp['b_router'])
        rows, block_e, n_used = _dispatch(route, T)
        e2 = _moe_experts(xn, rows, block_e, n_used, lp['wg'], lp['wu'], lp['wd'])
        moe = (e2, route)
    g = p['final_norm_g'].astype(F32)[None]
    return [_final(x, moe[0], moe[1], g, row0, B * L) for row0, B, L in groups]


def kernel(x_prompt, x_sample, norm_mix_g, w_in, hg_lb, hg_norm_g, ml_i_bias, ml_f_bias, ml_norm_g, hy_conv_w, hy_conv_b, filt_w1, filt_b1, filt_w2, filt_b2, filt_freq, filt_w3, hy_bias, w_branch_a, w_branch_b, w_branch_c, w_out, norm_ffn_g, router_group_w, router_group_b, router_expert_w, router_expert_b, exp_w_gate, exp_w_up, exp_w_down, final_norm_g):
    p = dict(norm_mix_g=norm_mix_g, w_in=w_in, hg_lb=hg_lb, hg_norm_g=hg_norm_g, ml_i_bias=ml_i_bias,
             ml_f_bias=ml_f_bias, ml_norm_g=ml_norm_g, hy_conv_w=hy_conv_w, hy_conv_b=hy_conv_b,
             filt_w1=filt_w1, filt_b1=filt_b1, filt_w2=filt_w2, filt_b2=filt_b2, filt_freq=filt_freq,
             filt_w3=filt_w3, hy_bias=hy_bias, w_branch_a=w_branch_a, w_branch_b=w_branch_b,
             w_branch_c=w_branch_c, w_out=w_out, norm_ffn_g=norm_ffn_g, router_group_w=router_group_w,
             router_group_b=router_group_b, router_expert_w=router_expert_w,
             router_expert_b=router_expert_b, exp_w_gate=exp_w_gate, exp_w_up=exp_w_up,
             exp_w_down=exp_w_down, final_norm_g=final_norm_g)
    Bp, Lp, _ = x_prompt.shape
    Bs, Ls, _ = x_sample.shape
    Tp, Ts = Bp * Lp, Bs * Ls
    x = jnp.concatenate([x_prompt.reshape(Tp, D_MODEL), x_sample.reshape(Ts, D_MODEL)], axis=0).astype(F32)
    yp, ys = _trunk(x, ((0, Bp, Lp), (Tp, Bs, Ls)), p)
    return (yp.reshape(Bp, Lp, D_MODEL), ys.reshape(Bs, Ls, D_MODEL))
```
